```python
import jax, jax.numpy as jnp
from jax import lax
import numpy as np

D_MODEL = 2048
BATCH = 8
SEQ = 8192
DEPTH = 4

N_MIXERS = 3
N_A = (DEPTH + 2) // 3
N_B = (DEPTH + 1) // 3
N_C = DEPTH // 3
POOL_WINDOWS = (2, 4, 8, 16)
N_POOL_GROUPS = len(POOL_WINDOWS)
POOL_GROUP = D_MODEL // N_POOL_GROUPS
CONV_WIDTH = 3
HEAD_DIM = 128
N_HEADS = D_MODEL // HEAD_DIM
Q_BLOCK = 128
D_FF = -(-8 * D_MODEL // (3 * 256)) * 256
N_MOD = 6
EPS = 1e-6

kernel_name = "hybrid_pool_conv_stickbreak_adaln"


def rmsnorm(x, g):
    xf = x.astype(jnp.float32)
    ms = jnp.mean(xf * xf, axis=-1, keepdims=True)
    return (xf * lax.rsqrt(ms + EPS)).astype(x.dtype) * g


def modulate(h, shift, scale):
    return h * (1 + scale[:, None, :]) + shift[:, None, :]


def causal_window_mean(u, w):
    S = u.shape[1]
    cs = lax.cumsum(u, axis=1)
    cs_shift = jnp.pad(cs, ((0, 0), (w, 0), (0, 0)))[:, :S]
    count = jnp.minimum(jnp.arange(1, S + 1), w).astype(jnp.float32)
    return (cs - cs_shift) / count[None, :, None]


def pool_mixer(h, w_pool, pool_scale):
    B, S, D = h.shape
    hf = h.reshape(B, S, N_POOL_GROUPS, POOL_GROUP).astype(jnp.float32)
    pooled = jnp.stack(
        [causal_window_mean(hf[:, :, g], w) for g, w in enumerate(POOL_WINDOWS)], axis=2)
    diff = (pooled - hf).astype(h.dtype)
    y = jnp.einsum('bsgc,gcd->bsgd', diff, w_pool).reshape(B, S, D)
    return y * pool_scale


def conv_mixer(h, w_in, conv_w, w_out):
    u = h @ w_in
    b_gate, c_gate, v = jnp.split(u, 3, axis=-1)
    z = c_gate * v
    zc = lax.conv_general_dilated(
        z, conv_w, window_strides=(1,), padding=((CONV_WIDTH - 1, 0),),
        dimension_numbers=('NWC', 'WIO', 'NWC'), feature_group_count=z.shape[-1])
    return (b_gate * zc) @ w_out


def stick_breaking_attention(q, k, v):
    B, H, S, Dh = q.shape
    nb = S // Q_BLOCK
    qb = q.reshape(B, H, nb, Q_BLOCK, Dh).transpose(2, 0, 1, 3, 4)
    key_pos = jnp.arange(S)
    scale = Dh ** -0.5

    def block(args):
        q_blk, i = args
        z = jnp.einsum('bhqd,bhkd->bhqk', q_blk, k,
                       preferred_element_type=jnp.float32) * scale
        q_pos = i * Q_BLOCK + jnp.arange(Q_BLOCK)
        causal = key_pos[None, :] < q_pos[:, None]
        log_beta = jax.nn.log_sigmoid(z)
        log_1m = jnp.where(causal, jax.nn.log_sigmoid(-z), 0.0)
        suffix = lax.cumsum(log_1m, axis=3, reverse=True) - log_1m
        a = jnp.where(causal, jnp.exp(log_beta + suffix), 0.0)
        return jnp.einsum('bhqk,bhkd->bhqd', a.astype(v.dtype), v)

    o = lax.map(block, (qb, jnp.arange(nb)))
    return o.transpose(1, 2, 0, 3, 4).reshape(B, H, S, Dh)


def sb_mixer(h, w_qkv, w_o):
    B, S, D = h.shape
    qkv = (h @ w_qkv).reshape(B, S, 3, N_HEADS, HEAD_DIM)
    q = qkv[:, :, 0].transpose(0, 2, 1, 3)
    k = qkv[:, :, 1].transpose(0, 2, 1, 3)
    v = qkv[:, :, 2].transpose(0, 2, 1, 3)
    o = stick_breaking_attention(q, k, v)
    return o.transpose(0, 2, 1, 3).reshape(B, S, D) @ w_o


def swiglu(h, w_gate, w_up, w_down):
    return (jax.nn.silu(h @ w_gate) * (h @ w_up)) @ w_down


def _fwd_setup_inputs(seed: int = 0) -> dict:
    key = jax.random.key(seed)
    ks = jax.random.split(key, 20)
    D = D_MODEL
    nrm = jax.random.normal
    f32 = jnp.float32
    return {
        "x": nrm(ks[0], (BATCH, SEQ, D), f32),
        "c": nrm(ks[1], (BATCH, D), f32),
        "norm_mix_g": 1.0 + 0.05 * nrm(ks[2], (DEPTH, D), f32),
        "norm_ffn_g": 1.0 + 0.05 * nrm(ks[3], (DEPTH, D), f32),
        "w_mod": 0.5 * D ** -0.5 * nrm(ks[4], (DEPTH, D, N_MOD * D), f32),
        "b_mod": 0.1 * nrm(ks[5], (DEPTH, N_MOD * D), f32),
        "pool_w": POOL_GROUP ** -0.5 * nrm(ks[6], (N_A, N_POOL_GROUPS, POOL_GROUP, POOL_GROUP), f32),
        "pool_scale": 1.0 + 0.1 * nrm(ks[7], (N_A, D), f32),
        "conv_w_in": D ** -0.5 * nrm(ks[8], (N_B, D, 3 * D), f32),
        "conv_w": CONV_WIDTH ** -0.5 * nrm(ks[9], (N_B, CONV_WIDTH, 1, D), f32),
        "conv_w_out": D ** -0.5 * nrm(ks[10], (N_B, D, D), f32),
        "sb_w_qkv": D ** -0.5 * nrm(ks[11], (N_C, D, 3 * D), f32),
        "sb_w_o": D ** -0.5 * nrm(ks[12], (N_C, D, D), f32),
        "ffn_w_gate": D ** -0.5 * nrm(ks[13], (DEPTH, D, D_FF), f32),
        "ffn_w_up": D ** -0.5 * nrm(ks[14], (DEPTH, D, D_FF), f32),
        "ffn_w_down": D_FF ** -0.5 * nrm(ks[15], (DEPTH, D_FF, D), f32),
        "final_g": 1.0 + 0.05 * nrm(ks[16], (D,), f32),
    }


def _fwd_reference(x, c, norm_mix_g, norm_ffn_g, w_mod, b_mod, pool_w, pool_scale,
              conv_w_in, conv_w, conv_w_out, sb_w_qkv, sb_w_o,
              ffn_w_gate, ffn_w_up, ffn_w_down, final_g):
    mod = jnp.einsum('bd,lde->lbe', jax.nn.silu(c), w_mod) + b_mod[:, None, :]
    h = x
    for i in range(DEPTH):
        shift_m, scale_m, gate_m, shift_f, scale_f, gate_f = jnp.split(mod[i], N_MOD, axis=-1)
        u = modulate(rmsnorm(h, norm_mix_g[i]), shift_m, scale_m)
        kind, j = i % N_MIXERS, i // N_MIXERS
        if kind == 0:
            y = pool_mixer(u, pool_w[j], pool_scale[j])
        elif kind == 1:
            y = conv_mixer(u, conv_w_in[j], conv_w[j], conv_w_out[j])
        else:
            y = sb_mixer(u, sb_w_qkv[j], sb_w_o[j])
        h = h + gate_m[:, None, :] * y
        u = modulate(rmsnorm(h, norm_ffn_g[i]), shift_f, scale_f)
        h = h + gate_f[:, None, :] * swiglu(u, ffn_w_gate[i], ffn_w_up[i], ffn_w_down[i])
    return rmsnorm(h, final_g)


import jax as _jax
import jax.numpy as _jnp

TWIN_FORMAT = 'train_step'
FWD_PARAMS = ['x', 'c', 'norm_mix_g', 'norm_ffn_g', 'w_mod', 'b_mod', 'pool_w', 'pool_scale', 'conv_w_in', 'conv_w', 'conv_w_out', 'sb_w_qkv', 'sb_w_o', 'ffn_w_gate', 'ffn_w_up', 'ffn_w_down', 'final_g']
TWIN_WEIGHTS = ['norm_mix_g', 'norm_ffn_g', 'w_mod', 'b_mod', 'pool_w', 'pool_scale', 'conv_w_in', 'conv_w', 'conv_w_out', 'sb_w_qkv', 'sb_w_o', 'ffn_w_gate', 'ffn_w_up', 'ffn_w_down', 'final_g']
TWIN_DIFF_INPUT = 'x'
TWIN_INPUTS = ['x', 'c', 'norm_mix_g', 'norm_ffn_g', 'w_mod', 'b_mod', 'pool_w', 'pool_scale', 'conv_w_in', 'conv_w', 'conv_w_out', 'sb_w_qkv', 'sb_w_o', 'ffn_w_gate', 'ffn_w_up', 'ffn_w_down', 'final_g', 'loss_target', 'm_norm_mix_g', 'm_norm_ffn_g', 'm_w_mod', 'm_b_mod', 'm_pool_w', 'm_pool_scale', 'm_conv_w_in', 'm_conv_w', 'm_conv_w_out', 'm_sb_w_qkv', 'm_sb_w_o', 'm_ffn_w_gate', 'm_ffn_w_up', 'm_ffn_w_down', 'm_final_g', 'v_norm_mix_g', 'v_norm_ffn_g', 'v_w_mod', 'v_b_mod', 'v_pool_w', 'v_pool_scale', 'v_conv_w_in', 'v_conv_w', 'v_conv_w_out', 'v_sb_w_qkv', 'v_sb_w_o', 'v_ffn_w_gate', 'v_ffn_w_up', 'v_ffn_w_down', 'v_final_g']
TWIN_OUTPUTS = ['loss', 'grad_x', 'grad_norm_mix_g', 'grad_norm_ffn_g', 'grad_w_mod', 'grad_b_mod', 'grad_pool_w', 'grad_pool_scale', 'grad_conv_w_in', 'grad_conv_w', 'grad_conv_w_out', 'grad_sb_w_qkv', 'grad_sb_w_o', 'grad_ffn_w_gate', 'grad_ffn_w_up', 'grad_ffn_w_down', 'grad_final_g', 'delta_norm_mix_g', 'delta_norm_ffn_g', 'delta_w_mod', 'delta_b_mod', 'delta_pool_w', 'delta_pool_scale', 'delta_conv_w_in', 'delta_conv_w', 'delta_conv_w_out', 'delta_sb_w_qkv', 'delta_sb_w_o', 'delta_ffn_w_gate', 'delta_ffn_w_up', 'delta_ffn_w_down', 'delta_final_g', 'new_m_norm_mix_g', 'new_m_norm_ffn_g', 'new_m_w_mod', 'new_m_b_mod', 'new_m_pool_w', 'new_m_pool_scale', 'new_m_conv_w_in', 'new_m_conv_w', 'new_m_conv_w_out', 'new_m_sb_w_qkv', 'new_m_sb_w_o', 'new_m_ffn_w_gate', 'new_m_ffn_w_up', 'new_m_ffn_w_down', 'new_m_final_g', 'new_v_norm_mix_g', 'new_v_norm_ffn_g', 'new_v_w_mod', 'new_v_b_mod', 'new_v_pool_w', 'new_v_pool_scale', 'new_v_conv_w_in', 'new_v_conv_w', 'new_v_conv_w_out', 'new_v_sb_w_qkv', 'new_v_sb_w_o', 'new_v_ffn_w_gate', 'new_v_ffn_w_up', 'new_v_ffn_w_down', 'new_v_final_g']
TWIN_LEAF_KINDS = {'loss': 'loss', 'grad_x': 'grad_x', 'grad_norm_mix_g': 'grad_w', 'grad_norm_ffn_g': 'grad_w', 'grad_w_mod': 'grad_w', 'grad_b_mod': 'grad_w', 'grad_pool_w': 'grad_w', 'grad_pool_scale': 'grad_w', 'grad_conv_w_in': 'grad_w', 'grad_conv_w': 'grad_w', 'grad_conv_w_out': 'grad_w', 'grad_sb_w_qkv': 'grad_w', 'grad_sb_w_o': 'grad_w', 'grad_ffn_w_gate': 'grad_w', 'grad_ffn_w_up': 'grad_w', 'grad_ffn_w_down': 'grad_w', 'grad_final_g': 'grad_w', 'delta_norm_mix_g': 'delta_w', 'delta_norm_ffn_g': 'delta_w', 'delta_w_mod': 'delta_w', 'delta_b_mod': 'delta_w', 'delta_pool_w': 'delta_w', 'delta_pool_scale': 'delta_w', 'delta_conv_w_in': 'delta_w', 'delta_conv_w': 'delta_w', 'delta_conv_w_out': 'delta_w', 'delta_sb_w_qkv': 'delta_w', 'delta_sb_w_o': 'delta_w', 'delta_ffn_w_gate': 'delta_w', 'delta_ffn_w_up': 'delta_w', 'delta_ffn_w_down': 'delta_w', 'delta_final_g': 'delta_w', 'new_m_norm_mix_g': 'new_m', 'new_m_norm_ffn_g': 'new_m', 'new_m_w_mod': 'new_m', 'new_m_b_mod': 'new_m', 'new_m_pool_w': 'new_m', 'new_m_pool_scale': 'new_m', 'new_m_conv_w_in': 'new_m', 'new_m_conv_w': 'new_m', 'new_m_conv_w_out': 'new_m', 'new_m_sb_w_qkv': 'new_m', 'new_m_sb_w_o': 'new_m', 'new_m_ffn_w_gate': 'new_m', 'new_m_ffn_w_up': 'new_m', 'new_m_ffn_w_down': 'new_m', 'new_m_final_g': 'new_m', 'new_v_norm_mix_g': 'new_v', 'new_v_norm_ffn_g': 'new_v', 'new_v_w_mod': 'new_v', 'new_v_b_mod': 'new_v', 'new_v_pool_w': 'new_v', 'new_v_pool_scale': 'new_v', 'new_v_conv_w_in': 'new_v', 'new_v_conv_w': 'new_v', 'new_v_conv_w_out': 'new_v', 'new_v_sb_w_qkv': 'new_v', 'new_v_sb_w_o': 'new_v', 'new_v_ffn_w_gate': 'new_v', 'new_v_ffn_w_up': 'new_v', 'new_v_ffn_w_down': 'new_v', 'new_v_final_g': 'new_v'}


def _forward(args):
    return _fwd_reference(*[args[k] for k in FWD_PARAMS])


def _output_shape():
    def fwd():
        inp = _fwd_setup_inputs(0)
        return _fwd_reference(*[inp[k] for k in FWD_PARAMS])
    out = _jax.eval_shape(fwd)
    return out.shape, out.dtype

N_MICROBATCH = 1
ADAM_LR = 0.001
ADAM_B1 = 0.9
ADAM_B2 = 0.999
ADAM_EPS = 1e-08
ADAM_WD = 0.01
ADAM_STEP = 10
PER_EXAMPLE_BATCH_AXIS = {'x': 0, 'c': 0, 'loss_target': 0}
SHARED_INPUTS = []
_WEIGHT_DTYPES = {'norm_mix_g': _jnp.float32, 'norm_ffn_g': _jnp.float32, 'w_mod': _jnp.float32, 'b_mod': _jnp.float32, 'pool_w': _jnp.float32, 'pool_scale': _jnp.float32, 'conv_w_in': _jnp.float32, 'conv_w': _jnp.float32, 'conv_w_out': _jnp.float32, 'sb_w_qkv': _jnp.float32, 'sb_w_o': _jnp.float32, 'ffn_w_gate': _jnp.float32, 'ffn_w_up': _jnp.float32, 'ffn_w_down': _jnp.float32, 'final_g': _jnp.float32}
MOMENT_SCALE = {'norm_mix_g': 5.205941e-02, 'norm_ffn_g': 3.737361e-02, 'w_mod': 5.856565e-02, 'b_mod': 1.297983e-01, 'pool_w': 3.749085e-02, 'pool_scale': 1.525468e-01, 'conv_w_in': 4.872785e-02, 'conv_w': 4.902166e-02, 'conv_w_out': 4.921138e-02, 'sb_w_qkv': 1.859161e-02, 'sb_w_o': 2.693203e-02, 'ffn_w_gate': 1.629235e-02, 'ffn_w_up': 1.577815e-02, 'ffn_w_down': 2.620250e-02, 'final_g': 3.215832e+01}


def _to_microbatches(a, axis):
    t = _jnp.moveaxis(a, axis, 0)
    t = t.reshape((N_MICROBATCH, t.shape[0] // N_MICROBATCH) + t.shape[1:])
    return _jnp.moveaxis(t, 1, axis + 1)


def setup_inputs(seed: int = 0) -> dict:
    inp = _fwd_setup_inputs(seed)
    key = _jax.random.fold_in(_jax.random.key(seed), 7919)
    shape, _ = _output_shape()
    out = dict(inp)
    out["loss_target"] = _jax.random.normal(_jax.random.fold_in(key, 0), shape, _jnp.float32)
    for i, name in enumerate(TWIN_WEIGHTS):
        w = inp[name].astype(_jnp.float32)
        if MOMENT_SCALE is None:
            s = _jnp.sqrt(_jnp.mean(_jnp.square(w)) + 1e-30)
        else:
            s = MOMENT_SCALE[name]
        km, kv = _jax.random.split(_jax.random.fold_in(key, i + 1))
        out[name] = w
        out["m_" + name] = s * _jax.random.normal(km, w.shape, _jnp.float32)
        out["v_" + name] = (s * s) * _jax.random.uniform(kv, w.shape, _jnp.float32, 0.5, 1.5)
    if N_MICROBATCH > 1:
        for name, axis in PER_EXAMPLE_BATCH_AXIS.items():
            out[name] = _to_microbatches(out[name], axis)
    return {'x': out['x'], 'c': out['c'], 'norm_mix_g': out['norm_mix_g'], 'norm_ffn_g': out['norm_ffn_g'], 'w_mod': out['w_mod'], 'b_mod': out['b_mod'], 'pool_w': out['pool_w'], 'pool_scale': out['pool_scale'], 'conv_w_in': out['conv_w_in'], 'conv_w': out['conv_w'], 'conv_w_out': out['conv_w_out'], 'sb_w_qkv': out['sb_w_qkv'], 'sb_w_o': out['sb_w_o'], 'ffn_w_gate': out['ffn_w_gate'], 'ffn_w_up': out['ffn_w_up'], 'ffn_w_down': out['ffn_w_down'], 'final_g': out['final_g'], 'loss_target': out['loss_target'], 'm_norm_mix_g': out['m_norm_mix_g'], 'm_norm_ffn_g': out['m_norm_ffn_g'], 'm_w_mod': out['m_w_mod'], 'm_b_mod': out['m_b_mod'], 'm_pool_w': out['m_pool_w'], 'm_pool_scale': out['m_pool_scale'], 'm_conv_w_in': out['m_conv_w_in'], 'm_conv_w': out['m_conv_w'], 'm_conv_w_out': out['m_conv_w_out'], 'm_sb_w_qkv': out['m_sb_w_qkv'], 'm_sb_w_o': out['m_sb_w_o'], 'm_ffn_w_gate': out['m_ffn_w_gate'], 'm_ffn_w_up': out['m_ffn_w_up'], 'm_ffn_w_down': out['m_ffn_w_down'], 'm_final_g': out['m_final_g'], 'v_norm_mix_g': out['v_norm_mix_g'], 'v_norm_ffn_g': out['v_norm_ffn_g'], 'v_w_mod': out['v_w_mod'], 'v_b_mod': out['v_b_mod'], 'v_pool_w': out['v_pool_w'], 'v_pool_scale': out['v_pool_scale'], 'v_conv_w_in': out['v_conv_w_in'], 'v_conv_w': out['v_conv_w'], 'v_conv_w_out': out['v_conv_w_out'], 'v_sb_w_qkv': out['v_sb_w_qkv'], 'v_sb_w_o': out['v_sb_w_o'], 'v_ffn_w_gate': out['v_ffn_w_gate'], 'v_ffn_w_up': out['v_ffn_w_up'], 'v_ffn_w_down': out['v_ffn_w_down'], 'v_final_g': out['v_final_g']}


def _loss(weights, diff, rest, loss_target):
    with _jax.named_scope("forward"):
        args = {**rest, TWIN_DIFF_INPUT: diff, **{k: w.astype(_WEIGHT_DTYPES[k]) for k, w in weights.items()}}
        y = _forward(args)
    with _jax.named_scope("loss_head"):
        err = _jnp.square(y.astype(_jnp.float32) - loss_target)
        return 0.5 * _jnp.sum(_jnp.mean(err, axis=-1)) if err.ndim else 0.5 * err


def _adamw(w, g, m, v):
    m = ADAM_B1 * m + (1.0 - ADAM_B1) * g
    v = ADAM_B2 * v + (1.0 - ADAM_B2) * _jnp.square(g)
    m_hat = m / (1.0 - ADAM_B1 ** ADAM_STEP)
    v_hat = v / (1.0 - ADAM_B2 ** ADAM_STEP)
    delta = -ADAM_LR * (m_hat / (_jnp.sqrt(v_hat) + ADAM_EPS) + ADAM_WD * w)
    return delta, m, v


def reference(x, c, norm_mix_g, norm_ffn_g, w_mod, b_mod, pool_w, pool_scale, conv_w_in, conv_w, conv_w_out, sb_w_qkv, sb_w_o, ffn_w_gate, ffn_w_up, ffn_w_down, final_g, loss_target, m_norm_mix_g, m_norm_ffn_g, m_w_mod, m_b_mod, m_pool_w, m_pool_scale, m_conv_w_in, m_conv_w, m_conv_w_out, m_sb_w_qkv, m_sb_w_o, m_ffn_w_gate, m_ffn_w_up, m_ffn_w_down, m_final_g, v_norm_mix_g, v_norm_ffn_g, v_w_mod, v_b_mod, v_pool_w, v_pool_scale, v_conv_w_in, v_conv_w, v_conv_w_out, v_sb_w_qkv, v_sb_w_o, v_ffn_w_gate, v_ffn_w_up, v_ffn_w_down, v_final_g):
    given = dict(x=x, c=c, norm_mix_g=norm_mix_g, norm_ffn_g=norm_ffn_g, w_mod=w_mod, b_mod=b_mod, pool_w=pool_w, pool_scale=pool_scale, conv_w_in=conv_w_in, conv_w=conv_w, conv_w_out=conv_w_out, sb_w_qkv=sb_w_qkv, sb_w_o=sb_w_o, ffn_w_gate=ffn_w_gate, ffn_w_up=ffn_w_up, ffn_w_down=ffn_w_down, final_g=final_g, loss_target=loss_target, m_norm_mix_g=m_norm_mix_g, m_norm_ffn_g=m_norm_ffn_g, m_w_mod=m_w_mod, m_b_mod=m_b_mod, m_pool_w=m_pool_w, m_pool_scale=m_pool_scale, m_conv_w_in=m_conv_w_in, m_conv_w=m_conv_w, m_conv_w_out=m_conv_w_out, m_sb_w_qkv=m_sb_w_qkv, m_sb_w_o=m_sb_w_o, m_ffn_w_gate=m_ffn_w_gate, m_ffn_w_up=m_ffn_w_up, m_ffn_w_down=m_ffn_w_down, m_final_g=m_final_g, v_norm_mix_g=v_norm_mix_g, v_norm_ffn_g=v_norm_ffn_g, v_w_mod=v_w_mod, v_b_mod=v_b_mod, v_pool_w=v_pool_w, v_pool_scale=v_pool_scale, v_conv_w_in=v_conv_w_in, v_conv_w=v_conv_w, v_conv_w_out=v_conv_w_out, v_sb_w_qkv=v_sb_w_qkv, v_sb_w_o=v_sb_w_o, v_ffn_w_gate=v_ffn_w_gate, v_ffn_w_up=v_ffn_w_up, v_ffn_w_down=v_ffn_w_down, v_final_g=v_final_g)
    weights = {n: given[n] for n in TWIN_WEIGHTS}
    shared = {n: given[n] for n in SHARED_INPUTS}
    per_example = {n: given[n] for n in ['x', 'c']}
    grad_fn = _jax.value_and_grad(_loss, argnums=(0, 1))

    def one_microbatch(ex, loss_target):
        ex = dict(ex)
        diff = ex.pop(TWIN_DIFF_INPUT)
        return grad_fn(weights, diff, {**shared, **ex}, loss_target)

    if N_MICROBATCH == 1:
        loss, (grad_w, grad_x) = one_microbatch(per_example, given["loss_target"])
    else:
        def body(carry, xs):
            loss_sum, grad_sum = carry
            l_k, (gw_k, gx_k) = one_microbatch(xs[0], xs[1])
            with _jax.named_scope("update"):
                return (loss_sum + l_k, _jax.tree.map(_jnp.add, grad_sum, gw_k)), gx_k

        init = (_jnp.zeros((), _jnp.float32), _jax.tree.map(_jnp.zeros_like, weights))
        (loss, grad_w), grad_x = _jax.lax.scan(body, init, (per_example, given["loss_target"]))
    with _jax.named_scope("update"):
        delta_w, new_m, new_v = {}, {}, {}
        for n in TWIN_WEIGHTS:
            delta_w[n], new_m[n], new_v[n] = _adamw(weights[n], grad_w[n], given["m_" + n], given["v_" + n])
    return (loss, grad_x, *[grad_w[n] for n in TWIN_WEIGHTS], *[delta_w[n] for n in TWIN_WEIGHTS],
            *[new_m[n] for n in TWIN_WEIGHTS], *[new_v[n] for n in TWIN_WEIGHTS])
```

```python
import functools

import jax
import jax.numpy as jnp
from jax import lax
from jax.experimental import pallas as pl
from jax.experimental.pallas import tpu as pltpu

F32 = jnp.float32
BF16 = jnp.bfloat16
SDS = jax.ShapeDtypeStruct
MESH = pl.DeviceIdType.MESH

RMS_EPS = 1e-6
POOL_WINDOWS = (2, 4, 8, 16)
POOL_HALO = 16
CONV_HALO = 16
HEAD_DIM = 128
N_MOD = 6
N_CHIPS = 4
N_DEV = 8
ADAM_LR = 0.001
ADAM_B1 = 0.9
ADAM_B2 = 0.999
ADAM_EPS = 1e-08
ADAM_WD = 0.01
ADAM_STEP = 10
VMEM_LIMIT_V7X = 52 * 1024 * 1024
ANY = pl.BlockSpec(memory_space=pl.ANY)
VMEM_WHOLE = pl.BlockSpec(memory_space=pltpu.VMEM)


def _cp(*sem):
    return pltpu.CompilerParams(dimension_semantics=sem, vmem_limit_bytes=VMEM_LIMIT_V7X)


def _tile(n, pref, unit):
    if n <= pref:
        return n
    t = (pref // unit) * unit
    while t >= unit:
        if n % t == 0:
            return t
        t -= unit
    return n


def _dot(a, b):
    return jnp.dot(a, b, preferred_element_type=F32)


def _dot_nt(a, b):
    return lax.dot_general(a, b, (((1,), (1,)), ((), ())), preferred_element_type=F32)


def _dot_tn(a, b):
    return lax.dot_general(a, b, (((0,), (0,)), ((), ())), preferred_element_type=F32)


def _split_bf16(x):
    hi = x.astype(BF16)
    lo = (x - hi.astype(F32)).astype(BF16)
    return hi, lo


def _sigmoid(x):
    return 1.0 / (1.0 + jnp.exp(-x))


def _my_place():
    return lax.axis_index("x"), lax.axis_index("y"), lax.axis_index("c")


def _allgather8(blk, name):
    m, n = blk.shape

    def body(x_ref, out_ref, send_sems, recv_sems, local_sem):
        x, y, c = _my_place()
        me, sibling = (x, y, c), (x, y, 1 - c)
        chips = [(1 - x, y), (x, 1 - y), (1 - x, 1 - y)]

        def rows(px, py, pc):
            return out_ref.at[pl.ds((4 * px + 2 * py + pc) * m, m), :]

        def copy(k, block, to, src=None):
            return pltpu.make_async_remote_copy(
                src_ref=rows(*block) if src is None else src, dst_ref=rows(*block),
                send_sem=send_sems.at[k], recv_sem=recv_sems.at[k], device_id=to, device_id_type=MESH)

        mine = pltpu.make_async_copy(x_ref, rows(*me), local_sem)
        mine.start()
        first = [copy(0, me, sibling, src=x_ref)]
        first += [copy(1 + j, me, (*chip, c), src=x_ref) for j, chip in enumerate(chips)]
        for cp in first:
            cp.start()
        passed = [copy(4 + j, (*chip, c), sibling) for j, chip in enumerate(chips)]
        for j, chip in enumerate(chips):
            copy(1 + j, (*chip, c), me).wait_recv()
            passed[j].start()
        copy(0, sibling, me).wait_recv()
        for j, chip in enumerate(chips):
            copy(4 + j, (*chip, 1 - c), me).wait_recv()
        for cp in first + passed:
            cp.wait_send()
        mine.wait()

    return pl.pallas_call(
        body, name=name, out_shape=SDS((N_DEV * m, n), blk.dtype),
        in_specs=[VMEM_WHOLE], out_specs=VMEM_WHOLE,
        scratch_shapes=[pltpu.SemaphoreType.DMA((7,)), pltpu.SemaphoreType.DMA((7,)), pltpu.SemaphoreType.DMA],
    )(blk)


def _gather_shards(ws, name):
    nt = len(ws)

    def body(*refs):
        w_refs, out_refs = refs[:nt], refs[nt:2 * nt]
        send_sems, recv_sems, local_sems = refs[2 * nt:]
        x, y, c = _my_place()
        sibling = (x, y, 1 - c)
        chips = [(1 - x, y), (x, 1 - y), (1 - x, 1 - y)]
        k_me = 2 * x + y
        copies = []
        for t in range(nt):
            half = ws[t].shape[0] // 2
            w_ref, out_ref = w_refs[t], out_refs[t]

            def dst(k, hc, out_ref=out_ref, half=half):
                return out_ref.at[k, pl.ds(hc * half, half), :]

            def copy(s, src, to_dst, to, t=t):
                return pltpu.make_async_remote_copy(
                    src_ref=src, dst_ref=to_dst, send_sem=send_sems.at[6 * t + s], recv_sem=recv_sems.at[6 * t + s],
                    device_id=to, device_id_type=MESH)

            mine = pltpu.make_async_copy(w_ref, out_ref.at[k_me], local_sems.at[t])
            mine.start()
            first = [copy(j, w_ref.at[pl.ds(c * half, half), :], dst(k_me, c), (*chip, c))
                     for j, chip in enumerate(chips)]
            for cp in first:
                cp.start()
            copies.append((mine, first, dst, copy))
        sends = []
        for t in range(nt):
            mine, first, dst, copy = copies[t]
            passed = []
            for j, (px, py) in enumerate(chips):
                landed = dst(2 * px + py, c)
                copy(j, landed, landed, (px, py, c)).wait_recv()
                fwd = copy(3 + j, landed, landed, sibling)
                fwd.start()
                passed.append(fwd)
            sends.append((mine, first + passed))
        for t in range(nt):
            _, _, dst, copy = copies[t]
            for j, (px, py) in enumerate(chips):
                other = dst(2 * px + py, 1 - c)
                copy(3 + j, other, other, sibling).wait_recv()
        for mine, cps in sends:
            for cp in cps:
                cp.wait_send()
            mine.wait()

    return pl.pallas_call(
        body, name=name,
        out_shape=[SDS((N_CHIPS,) + w.shape, w.dtype) for w in ws],
        in_specs=[ANY] * nt, out_specs=[ANY] * nt,
        scratch_shapes=[pltpu.SemaphoreType.DMA((6 * nt,)), pltpu.SemaphoreType.DMA((6 * nt,)),
                        pltpu.SemaphoreType.DMA((nt,))],
    )(*ws)


def _swap_sibling_halves(gs, name):
    nt = len(gs)

    def body(*refs):
        g_refs, out_refs = refs[:nt], refs[nt:2 * nt]
        send_sems, recv_sems = refs[2 * nt:]
        x, y, c = _my_place()
        cps = []
        for t in range(nt):
            half = gs[t].shape[1] // 2
            cp = pltpu.make_async_remote_copy(
                src_ref=g_refs[t].at[:, pl.ds((1 - c) * half, half), :], dst_ref=out_refs[t],
                send_sem=send_sems.at[t], recv_sem=recv_sems.at[t], device_id=(x, y, 1 - c), device_id_type=MESH)
            cp.start()
            cps.append(cp)
        for cp in cps:
            cp.wait()

    return pl.pallas_call(
        body, name=name,
        out_shape=[SDS((g.shape[0], g.shape[1] // 2, g.shape[2]), g.dtype) for g in gs],
        in_specs=[ANY] * nt, out_specs=[ANY] * nt,
        scratch_shapes=[pltpu.SemaphoreType.DMA((nt,)), pltpu.SemaphoreType.DMA((nt,))],
    )(*gs)


def _scatter_to_chips(ps, name):
    nt = len(ps)

    def body(*refs):
        p_refs, out_refs = refs[:nt], refs[nt:2 * nt]
        send_sems, recv_sems = refs[2 * nt:]
        x, y, c = _my_place()
        chips = [(1 - x, y), (x, 1 - y), (1 - x, 1 - y)]
        cps = []
        for t in range(nt):
            for j, (px, py) in enumerate(chips):
                cp = pltpu.make_async_remote_copy(
                    src_ref=p_refs[t].at[2 * px + py], dst_ref=out_refs[t].at[j],
                    send_sem=send_sems.at[3 * t + j], recv_sem=recv_sems.at[3 * t + j], device_id=(px, py, c),
                    device_id_type=MESH)
                cp.start()
                cps.append(cp)
        for cp in cps:
            cp.wait()

    return pl.pallas_call(
        body, name=name,
        out_shape=[SDS((3,) + p.shape[1:], p.dtype) for p in ps],
        in_specs=[ANY] * nt, out_specs=[ANY] * nt,
        scratch_shapes=[pltpu.SemaphoreType.DMA((3 * nt,)), pltpu.SemaphoreType.DMA((3 * nt,))],
    )(*ps)


def _join_sibling_halves(fs, name):
    nt = len(fs)

    def body(*refs):
        f_refs, out_refs = refs[:nt], refs[nt:2 * nt]
        send_sems, recv_sems, local_sems = refs[2 * nt:]
        x, y, c = _my_place()
        cps = []
        for t in range(nt):
            r = fs[t].shape[0]
            mine = pltpu.make_async_copy(f_refs[t], out_refs[t].at[pl.ds(c * r, r), :], local_sems.at[t])
            mine.start()
            cp = pltpu.make_async_remote_copy(
                src_ref=f_refs[t], dst_ref=out_refs[t].at[pl.ds(c * r, r), :],
                send_sem=send_sems.at[t], recv_sem=recv_sems.at[t], device_id=(x, y, 1 - c), device_id_type=MESH)
            cp.start()
            cps.append((mine, cp, r))
        for t, (mine, cp, r) in enumerate(cps):
            cp.wait_send()
            other = out_refs[t].at[pl.ds((1 - c) * r, r), :]
            pltpu.make_async_remote_copy(
                src_ref=other, dst_ref=other, send_sem=send_sems.at[t], recv_sem=recv_sems.at[t],
                device_id=(x, y, 1 - c), device_id_type=MESH).wait_recv()
            mine.wait()

    return pl.pallas_call(
        body, name=name,
        out_shape=[SDS((2 * f.shape[0], f.shape[1]), f.dtype) for f in fs],
        in_specs=[ANY] * nt, out_specs=[ANY] * nt,
        scratch_shapes=[pltpu.SemaphoreType.DMA((nt,)), pltpu.SemaphoreType.DMA((nt,)),
                        pltpu.SemaphoreType.DMA((nt,))],
    )(*fs)


def _add_sibling(g, recv, place, name):
    _, R, C = g.shape
    half = R // 2
    br = _tile(half, max(16, (1 << 19) // C), 16)
    nrb = half // br

    def body(place_ref, g_ref, r_ref, bf_ref, own_ref):
        s = g_ref[...] + r_ref[...]
        bf_ref[...] = s.astype(BF16)

        @pl.when(pl.program_id(1) == place_ref[1])
        def _():
            own_ref[...] = s

    return pl.pallas_call(
        body, name=name,
        grid_spec=pltpu.PrefetchScalarGridSpec(
            num_scalar_prefetch=1, grid=(nrb, N_CHIPS),
            in_specs=[pl.BlockSpec((None, br, C), lambda i, k, pr: (k, pr[0] * nrb + i, 0)),
                      pl.BlockSpec((None, br, C), lambda i, k, pr: (k, i, 0))],
            out_specs=[pl.BlockSpec((None, br, C), lambda i, k, pr: (k, i, 0)),
                       pl.BlockSpec((br, C), lambda i, k, pr: (i, 0))]),
        out_shape=[SDS((N_CHIPS, half, C), BF16), SDS((half, C), F32)],
        compiler_params=_cp("arbitrary", "arbitrary"),
    )(place, g, recv)


def _add_chips(own, recv, name):
    r, C = own.shape
    br = _tile(r, max(16, (1 << 19) // C), 16)

    def body(own_ref, r_ref, o_ref):
        s = own_ref[...]
        for j in range(3):
            s = s + r_ref[j].astype(F32)
        o_ref[...] = s

    return pl.pallas_call(
        body, name=name, grid=(r // br,),
        in_specs=[pl.BlockSpec((br, C), lambda i: (i, 0)), pl.BlockSpec((3, br, C), lambda i: (0, i, 0))],
        out_specs=pl.BlockSpec((br, C), lambda i: (i, 0)),
        out_shape=SDS((r, C), F32), compiler_params=_cp("arbitrary"),
    )(own, recv)


def _reduce_to_owner(gs, place, name):
    recv1 = _swap_sibling_halves(gs, name + "_swap")
    parts = [_add_sibling(g, r, place, name + "_add1") for g, r in zip(gs, recv1)]
    recv2 = _scatter_to_chips([p[0] for p in parts], name + "_scatter")
    fins = [_add_chips(p[1], r, name + "_add2") for p, r in zip(parts, recv2)]
    return _join_sibling_halves(fins, name + "_join")


def _sum_devices(allv, name):
    _, r, n = allv.shape

    def body(a_ref, o_ref):
        s = a_ref[0]
        for d in range(1, N_DEV):
            s = s + a_ref[d]
        o_ref[...] = s

    return pl.pallas_call(body, name=name, out_shape=SDS((r, n), F32), in_specs=[VMEM_WHOLE],
                          out_specs=VMEM_WHOLE)(allv)


def _adamw(w, g, m, v, name):
    shape = w.shape
    C = shape[-1]
    R = w.size // C
    args = [a.reshape(R, C) for a in (w, g, m, v)]
    br = _tile(R, max(8, (1 << 18) // C), 8)

    def body(w_ref, g_ref, m_ref, v_ref, d_ref, nm_ref, nv_ref):
        g_ = g_ref[...]
        m_ = ADAM_B1 * m_ref[...] + (1.0 - ADAM_B1) * g_
        v_ = ADAM_B2 * v_ref[...] + (1.0 - ADAM_B2) * (g_ * g_)
        m_hat = m_ / (1.0 - ADAM_B1 ** ADAM_STEP)
        v_hat = v_ / (1.0 - ADAM_B2 ** ADAM_STEP)
        d_ref[...] = -ADAM_LR * (m_hat / (jnp.sqrt(v_hat) + ADAM_EPS) + ADAM_WD * w_ref[...])
        nm_ref[...] = m_
        nv_ref[...] = v_

    spec = pl.BlockSpec((br, C), lambda i: (i, 0))
    outs = pl.pallas_call(
        body, name=name, grid=(R // br,), in_specs=[spec] * 4, out_specs=[spec] * 3,
        out_shape=[SDS((R, C), F32)] * 3, compiler_params=_cp("parallel"),
    )(*args)
    return [o.reshape(shape) for o in outs]


def _mod_fwd(c_rows, w_mod, b_cols, name):
    L, D, n = w_mod.shape
    bn = _tile(n, 512, 128)

    def body(c_ref, w_ref, b_ref, o_ref):
        cc = c_ref[...]
        sc = (cc * _sigmoid(cc)).astype(BF16)
        o_ref[...] = _dot(sc, w_ref[...].astype(BF16)) + b_ref[...]

    return pl.pallas_call(
        body, name=name, grid=(L, n // bn),
        in_specs=[pl.BlockSpec((16, D), lambda l, j: (0, 0)),
                  pl.BlockSpec((None, D, bn), lambda l, j: (l, 0, j)),
                  pl.BlockSpec((None, 1, bn), lambda l, j: (l, 0, j))],
        out_specs=pl.BlockSpec((None, 16, bn), lambda l, j: (l, 0, j)),
        out_shape=SDS((L, 16, n), F32), compiler_params=_cp("parallel", "parallel"),
    )(c_rows, w_mod, b_cols)


def _mod_wgrad(c_cols, dmod, name):
    D = c_cols.shape[0]
    L, _, n = dmod.shape
    bd = _tile(D, 512, 8)
    bn = _tile(n, 512, 128)

    def body(c_ref, d_ref, o_ref):
        cc = c_ref[...]
        sc = cc * _sigmoid(cc)
        dm = d_ref[...]
        acc = sc[:, 0:1] * dm[0:1, :]
        for b in range(1, N_DEV):
            acc = acc + sc[:, b:b + 1] * dm[b:b + 1, :]
        o_ref[...] = acc

    return pl.pallas_call(
        body, name=name, grid=(L, D // bd, n // bn),
        in_specs=[pl.BlockSpec((bd, N_DEV), lambda l, i, j: (i, 0)),
                  pl.BlockSpec((None, N_DEV, bn), lambda l, i, j: (l, 0, j))],
        out_specs=pl.BlockSpec((None, bd, bn), lambda l, i, j: (l, i, j)),
        out_shape=SDS((L, D, n), F32), compiler_params=_cp("parallel", "parallel", "parallel"),
    )(c_cols, dmod)


def _mm_in(x, w4, bn, name, gate=None):
    M, K = x.shape
    nsh, _, n = w4.shape
    N = nsh * n
    nb = n // bn
    bm = _tile(M, 512, 16)
    x_spec = pl.BlockSpec((bm, K), lambda j, i: (i, 0))
    w_spec = pl.BlockSpec((None, K, bn), lambda j, i: (j // nb, 0, j % nb))
    o_spec = pl.BlockSpec((bm, bn), lambda j, i: (i, j))
    if gate is None:
        def body(x_ref, w_ref, o_ref):
            o_ref[...] = _dot(x_ref[...], w_ref[...]).astype(BF16)

        return pl.pallas_call(
            body, name=name, grid=(N // bn, M // bm), in_specs=[x_spec, w_spec], out_specs=o_spec,
            out_shape=SDS((M, N), BF16), compiler_params=_cp("parallel", "parallel"))(x, w4)

    def body_gated(x_ref, w_ref, g_ref, up_ref, a_ref):
        up = _dot(x_ref[...], w_ref[...])
        g = g_ref[...].astype(F32)
        up_ref[...] = up.astype(BF16)
        a_ref[...] = (g * _sigmoid(g) * up).astype(BF16)

    return pl.pallas_call(
        body_gated, name=name, grid=(N // bn, M // bm), in_specs=[x_spec, w_spec, o_spec],
        out_specs=[o_spec, o_spec], out_shape=[SDS((M, N), BF16)] * 2,
        compiler_params=_cp("parallel", "parallel"))(x, w4, gate)


def _mm_out_res(a, w, h, cvec, name, groups=False):
    M = a.shape[0]
    N = h.shape[1]
    if groups:
        bn = w.shape[2]
        a_spec = pl.BlockSpec((_tile(M, 512, 16), w.shape[1]), lambda j, i: (i, j))
        w_spec = pl.BlockSpec((None, w.shape[1], bn), lambda j, i: (j, 0, 0))
    else:
        bn = _tile(N, 512, 128)
        a_spec = pl.BlockSpec((_tile(M, 512, 16), a.shape[1]), lambda j, i: (i, 0))
        w_spec = pl.BlockSpec((a.shape[1], bn), lambda j, i: (0, j))
    bm = _tile(M, 512, 16)
    o_spec = pl.BlockSpec((bm, bn), lambda j, i: (i, j))

    def body(a_ref, w_ref, h_ref, c_ref, hn_ref, y_ref):
        y = _dot(a_ref[...], w_ref[...])
        hn_ref[...] = h_ref[...] + c_ref[...] * y
        y_ref[...] = y.astype(BF16)

    return pl.pallas_call(
        body, name=name, grid=(N // bn, M // bm),
        in_specs=[a_spec, w_spec, o_spec, pl.BlockSpec((1, bn), lambda j, i: (0, j))],
        out_specs=[o_spec, o_spec], out_shape=[SDS((M, N), F32), SDS((M, N), BF16)],
        compiler_params=_cp("parallel", "parallel"))(a, w, h, cvec)


def _mm_nt(dy, w, name, groups=False, swiglu=None):
    M = dy.shape[0]
    bm = _tile(M, 1024, 16)
    if groups:
        N = dy.shape[1]
        bn = w.shape[1]
        dy_spec = pl.BlockSpec((bm, w.shape[2]), lambda j, i: (i, j))
        w_spec = pl.BlockSpec((None, bn, w.shape[2]), lambda j, i: (j, 0, 0))
    else:
        N = w.shape[0]
        bn = _tile(N, 512, 128)
        dy_spec = pl.BlockSpec((bm, dy.shape[1]), lambda j, i: (i, 0))
        w_spec = pl.BlockSpec((bn, w.shape[1]), lambda j, i: (j, 0))
    o_spec = pl.BlockSpec((bm, bn), lambda j, i: (i, j))
    if swiglu is None:
        def body(dy_ref, w_ref, o_ref):
            o_ref[...] = _dot_nt(dy_ref[...], w_ref[...]).astype(BF16)

        return pl.pallas_call(
            body, name=name, grid=(N // bn, M // bm), in_specs=[dy_spec, w_spec], out_specs=o_spec,
            out_shape=SDS((M, N), BF16), compiler_params=_cp("parallel", "parallel"))(dy, w)

    def body_swiglu(dy_ref, w_ref, g_ref, u_ref, dg_ref, du_ref):
        da = _dot_nt(dy_ref[...], w_ref[...])
        g = g_ref[...].astype(F32)
        sg = _sigmoid(g)
        silu = g * sg
        dg_ref[...] = (da * u_ref[...].astype(F32) * (sg + silu * (1.0 - sg))).astype(BF16)
        du_ref[...] = (da * silu).astype(BF16)

    return pl.pallas_call(
        body_swiglu, name=name, grid=(N // bn, M // bm), in_specs=[dy_spec, w_spec, o_spec, o_spec],
        out_specs=[o_spec, o_spec], out_shape=[SDS((M, N), BF16)] * 2,
        compiler_params=_cp("parallel", "parallel"))(dy, w, *swiglu)


def _mm_nt_acc(dx, w4, name, add=None):
    M = dx.shape[0]
    nc, K, n = w4.shape
    bm = _tile(M, 512, 16)
    with_add = add is not None

    def body(*refs):
        dx_ref, w_ref = refs[:2]
        o_ref, acc_ref = refs[-2:]
        c = pl.program_id(1)
        s = _dot_nt(dx_ref[...], w_ref[...])

        @pl.when(c == 0)
        def _():
            acc_ref[...] = s + refs[2][...].astype(F32) if with_add else s

        @pl.when(c > 0)
        def _():
            acc_ref[...] += s

        @pl.when(c == nc - 1)
        def _():
            o_ref[...] = acc_ref[...].astype(BF16)

    o_spec = pl.BlockSpec((bm, K), lambda i, c: (i, 0))
    in_specs = [pl.BlockSpec((bm, n), lambda i, c: (i, c)), pl.BlockSpec((None, K, n), lambda i, c: (c, 0, 0))]
    args = [dx, w4]
    if with_add:
        in_specs.append(o_spec)
        args.append(add)
    return pl.pallas_call(
        body, name=name, grid=(M // bm, nc), in_specs=in_specs, out_specs=o_spec, out_shape=SDS((M, K), BF16),
        scratch_shapes=[pltpu.VMEM((bm, K), F32)], compiler_params=_cp("parallel", "arbitrary"))(*args)


def _mm_tn(x, dy, name, shard_cols=None, groups=None):
    M, K = x.shape
    N = dy.shape[1]
    bm = _tile(M, 1024, 16)
    if groups is not None:
        kg, ng = K // groups, N // groups
        grid = (groups, 1, M // bm)
        x_spec = pl.BlockSpec((bm, kg), lambda i, j, s: (s, i))
        dy_spec = pl.BlockSpec((bm, ng), lambda i, j, s: (s, i))
        o_spec = pl.BlockSpec((None, kg, ng), lambda i, j, s: (i, 0, 0))
        out_shape = SDS((groups, kg, ng), F32)
    else:
        bko = _tile(K, 1408, 128)
        if shard_cols is not None:
            bn = _tile(shard_cols, 1536, 128)
            nb = shard_cols // bn
            o_spec = pl.BlockSpec((None, bko, bn), lambda i, j, s: (j // nb, i, j % nb))
            out_shape = SDS((N_CHIPS, K, shard_cols), F32)
        else:
            bn = _tile(N, 1024, 128)
            o_spec = pl.BlockSpec((bko, bn), lambda i, j, s: (i, j))
            out_shape = SDS((K, N), F32)
        grid = (K // bko, N // bn, M // bm)
        x_spec = pl.BlockSpec((bm, bko), lambda i, j, s: (s, i))
        dy_spec = pl.BlockSpec((bm, bn), lambda i, j, s: (s, j))

    def body(x_ref, dy_ref, o_ref):
        p = _dot_tn(x_ref[...], dy_ref[...])

        @pl.when(pl.program_id(2) == 0)
        def _():
            o_ref[...] = p

        @pl.when(pl.program_id(2) > 0)
        def _():
            o_ref[...] += p

    return pl.pallas_call(
        body, name=name, grid=grid, in_specs=[x_spec, dy_spec], out_specs=o_spec, out_shape=out_shape,
        compiler_params=_cp("parallel", "parallel", "arbitrary"))(x, dy)


def _norm_mod_rows(h, g, scale, shift):
    r = lax.rsqrt(jnp.mean(h * h, axis=-1, keepdims=True) + RMS_EPS)
    return (h * r) * g * (1.0 + scale) + shift


def _vec_spec(D):
    return pl.BlockSpec((1, D), lambda i: (0, 0))


def _norm_mod(h, g, scale, shift, name):
    S, D = h.shape
    bs = _tile(S, 512, 16)

    def body(h_ref, g_ref, sc_ref, sh_ref, u_ref):
        u_ref[...] = _norm_mod_rows(h_ref[...], g_ref[...], sc_ref[...], sh_ref[...]).astype(BF16)

    row = pl.BlockSpec((bs, D), lambda i: (i, 0))
    return pl.pallas_call(
        body, name=name, grid=(S // bs,), in_specs=[row, _vec_spec(D), _vec_spec(D), _vec_spec(D)],
        out_specs=row, out_shape=SDS((S, D), BF16), compiler_params=_cp("parallel"))(h, g, scale, shift)


def _band(rows, cols, lo, hi):
    d = lax.broadcasted_iota(jnp.int32, (rows, cols), 1) - lax.broadcasted_iota(jnp.int32, (rows, cols), 0)
    return jnp.where((d >= lo) & (d < hi), 1.0, 0.0).astype(BF16)


def _band_apply(band, x):
    hi, lo = _split_bf16(x)
    return _dot(band, hi) + _dot(band, lo)


def _pool_pre(h, g, scale, shift, name):
    S, D = h.shape
    ng = len(POOL_WINDOWS)
    pg = D // ng
    bs = _tile(S, 256, POOL_HALO)
    hb = bs // POOL_HALO

    def body(h_ref, hh_ref, g_ref, sc_ref, sh_ref, o_ref):
        i = pl.program_id(0)
        u = _norm_mod_rows(h_ref[...], g_ref[...], sc_ref[...], sh_ref[...])
        uh = _norm_mod_rows(hh_ref[...], g_ref[...], sc_ref[...], sh_ref[...])
        uh = jnp.where(i == 0, 0.0, uh)
        ue = jnp.concatenate([uh, u], axis=0)
        t = i * bs + lax.broadcasted_iota(jnp.int32, (bs, 1), 0)
        for gi, w in enumerate(POOL_WINDOWS):
            cols = slice(gi * pg, (gi + 1) * pg)
            band = _band(bs, bs + POOL_HALO, POOL_HALO - w + 1, POOL_HALO + 1)
            inv = 1.0 / jnp.minimum(t + 1, w).astype(F32)
            o_ref[:, cols] = (_band_apply(band, ue[:, cols]) * inv - u[:, cols]).astype(BF16)

    row = pl.BlockSpec((bs, D), lambda i: (i, 0))
    halo = pl.BlockSpec((POOL_HALO, D), lambda i: (jnp.maximum(i * hb - 1, 0), 0))
    return pl.pallas_call(
        body, name=name, grid=(S // bs,),
        in_specs=[row, halo, _vec_spec(D), _vec_spec(D), _vec_spec(D)],
        out_specs=row, out_shape=SDS((S, D), BF16), compiler_params=_cp("parallel"))(h, h, g, scale, shift)


def _pool_post(dd, name):
    S, D = dd.shape
    ng = len(POOL_WINDOWS)
    pg = D // ng
    bs = _tile(S, 256, POOL_HALO)
    hb = bs // POOL_HALO
    nblk = S // bs

    def body(d_ref, dn_ref, o_ref):
        i = pl.program_id(0)
        d = d_ref[...].astype(F32)
        dn = jnp.where(i == nblk - 1, 0.0, dn_ref[...].astype(F32))
        de = jnp.concatenate([d, dn], axis=0)
        t = i * bs + lax.broadcasted_iota(jnp.int32, (bs + POOL_HALO, 1), 0)
        for gi, w in enumerate(POOL_WINDOWS):
            cols = slice(gi * pg, (gi + 1) * pg)
            inv = 1.0 / jnp.minimum(t + 1, w).astype(F32)
            band = _band(bs, bs + POOL_HALO, 0, w)
            o_ref[:, cols] = (_band_apply(band, de[:, cols] * inv) - d[:, cols]).astype(BF16)

    row = pl.BlockSpec((bs, D), lambda i: (i, 0))
    nxt = pl.BlockSpec((POOL_HALO, D), lambda i: (jnp.minimum((i + 1) * hb, S // POOL_HALO - 1), 0))
    return pl.pallas_call(
        body, name=name, grid=(nblk,), in_specs=[row, nxt], out_specs=row, out_shape=SDS((S, D), BF16),
        compiler_params=_cp("parallel"))(dd, dd)


def _colsum(x):
    return jnp.sum(x, axis=0, keepdims=True)


def _accumulate_rows(st_ref, rows, first):
    @pl.when(first)
    def _():
        st_ref[...] = jnp.zeros_like(st_ref)

    for r, row in enumerate(rows):
        st_ref[r:r + 1, :] += row


def _norm_bwd(h, g, scale, du, dh_out, name, prev=None):
    S, D = h.shape
    bs = _tile(S, 256, 16)
    with_prev = prev is not None

    def body(*refs):
        h_ref, g_ref, sc_ref, du_ref, dho_ref = refs[:5]
        if with_prev:
            y_ref, cv_ref, dh_ref, dy_ref, st_ref = refs[5:]
        else:
            dh_ref, st_ref = refs[5:]
        hh = h_ref[...]
        du_ = du_ref[...].astype(F32)
        r = lax.rsqrt(jnp.mean(hh * hh, axis=-1, keepdims=True) + RMS_EPS)
        xhat = hh * r
        dn = du_ * (1.0 + sc_ref[...])
        dxhat = dn * g_ref[...]
        dh = dho_ref[...] + r * (dxhat - xhat * jnp.mean(dxhat * xhat, axis=-1, keepdims=True))
        dh_ref[...] = dh
        rows = [_colsum(du_), _colsum(du_ * (xhat * g_ref[...])), _colsum(dn * xhat)]
        if with_prev:
            dy_ref[...] = (dh * cv_ref[...]).astype(BF16)
            rows.append(_colsum(dh * y_ref[...].astype(F32)))
        _accumulate_rows(st_ref, rows, pl.program_id(0) == 0)

    row = pl.BlockSpec((bs, D), lambda i: (i, 0))
    st_spec = pl.BlockSpec((8, D), lambda i: (0, 0))
    in_specs = [row, _vec_spec(D), _vec_spec(D), row, row]
    args = [h, g, scale, du, dh_out]
    out_specs, out_shape = [row], [SDS((S, D), F32)]
    if with_prev:
        in_specs += [row, _vec_spec(D)]
        args += list(prev)
        out_specs.append(row)
        out_shape.append(SDS((S, D), BF16))
    out_specs.append(st_spec)
    out_shape.append(SDS((8, D), F32))
    return pl.pallas_call(
        body, name=name, grid=(S // bs,), in_specs=in_specs, out_specs=out_specs, out_shape=out_shape,
        compiler_params=_cp("arbitrary"))(*args)


def _loss_head(h, g, target, y, cvec, name):
    S, D = h.shape
    bs = _tile(S, 256, 16)

    def body(h_ref, g_ref, t_ref, y_ref, cv_ref, dh_ref, dy_ref, st_ref):
        hh = h_ref[...]
        r = lax.rsqrt(jnp.mean(hh * hh, axis=-1, keepdims=True) + RMS_EPS)
        xhat = hh * r
        err = xhat * g_ref[...] - t_ref[...]
        dout = err * (1.0 / D)
        dxhat = dout * g_ref[...]
        dh = r * (dxhat - xhat * jnp.mean(dxhat * xhat, axis=-1, keepdims=True))
        dh_ref[...] = dh
        dy_ref[...] = (dh * cv_ref[...]).astype(BF16)
        rows = [_colsum(dout * xhat), _colsum(dh * y_ref[...].astype(F32)), _colsum(err * err) * (0.5 / D)]
        _accumulate_rows(st_ref, rows, pl.program_id(0) == 0)

    row = pl.BlockSpec((bs, D), lambda i: (i, 0))
    return pl.pallas_call(
        body, name=name, grid=(S // bs,), in_specs=[row, _vec_spec(D), row, row, _vec_spec(D)],
        out_specs=[row, row, pl.BlockSpec((8, D), lambda i: (0, 0))],
        out_shape=[SDS((S, D), F32), SDS((S, D), BF16), SDS((8, D), F32)],
        compiler_params=_cp("arbitrary"))(h, g, target, y, cvec)


def _sum_all(x, name):
    def body(x_ref, o_ref):
        o_ref[...] = jnp.sum(jnp.sum(x_ref[...], axis=1, keepdims=True), axis=0, keepdims=True)

    return pl.pallas_call(body, name=name, out_shape=SDS((1, 1), F32), in_specs=[VMEM_WHOLE],
                          out_specs=VMEM_WHOLE)(x)


def _conv_mid(u3, cw, name):
    S, D3 = u3.shape
    D = D3 // 3
    cb = _tile(D, 512, 128)
    nj = D // cb
    bs = _tile(S, 256, CONV_HALO)
    hb = bs // CONV_HALO

    def body(b_ref, c_ref, v_ref, ch_ref, vh_ref, w_ref, o_ref):
        i = pl.program_id(0)
        z = c_ref[...].astype(F32) * v_ref[...].astype(F32)
        zh = jnp.where(i == 0, 0.0, ch_ref[...].astype(F32) * vh_ref[...].astype(F32))
        ze = jnp.concatenate([zh, z], axis=0)
        w = w_ref[...]
        zc = w[2:3] * z
        zc = zc + w[1:2] * _band_apply(_band(bs, bs + CONV_HALO, CONV_HALO - 1, CONV_HALO), ze)
        zc = zc + w[0:1] * _band_apply(_band(bs, bs + CONV_HALO, CONV_HALO - 2, CONV_HALO - 1), ze)
        o_ref[...] = (b_ref[...].astype(F32) * zc).astype(BF16)

    def blk(off):
        return pl.BlockSpec((bs, cb), lambda i, j: (i, off + j))

    def halo(off):
        return pl.BlockSpec((CONV_HALO, cb), lambda i, j: (jnp.maximum(i * hb - 1, 0), off + j))

    return pl.pallas_call(
        body, name=name, grid=(S // bs, nj),
        in_specs=[blk(0), blk(nj), blk(2 * nj), halo(nj), halo(2 * nj), pl.BlockSpec((3, cb), lambda i, j: (0, j))],
        out_specs=pl.BlockSpec((bs, cb), lambda i, j: (i, j)), out_shape=SDS((S, D), BF16),
        compiler_params=_cp("parallel", "parallel"))(u3, u3, u3, u3, u3, cw)


def _conv_mid_bwd(u3, da, cw, name):
    S, D3 = u3.shape
    D = D3 // 3
    cb = _tile(D, 512, 128)
    nj = D // cb
    bs = _tile(S, 256, CONV_HALO)
    hb = bs // CONV_HALO
    nblk = S // bs
    last_halo = S // CONV_HALO - 1

    def body(b_ref, c_ref, v_ref, ch_ref, vh_ref, bn_ref, da_ref, dan_ref, w_ref, db_ref, dc_ref, dv_ref, dw_ref):
        i = pl.program_id(0)
        c = c_ref[...].astype(F32)
        v = v_ref[...].astype(F32)
        b = b_ref[...].astype(F32)
        da_ = da_ref[...].astype(F32)
        z = c * v
        zh = jnp.where(i == 0, 0.0, ch_ref[...].astype(F32) * vh_ref[...].astype(F32))
        ze = jnp.concatenate([zh, z], axis=0)
        z1 = _band_apply(_band(bs, bs + CONV_HALO, CONV_HALO - 1, CONV_HALO), ze)
        z2 = _band_apply(_band(bs, bs + CONV_HALO, CONV_HALO - 2, CONV_HALO - 1), ze)
        w = w_ref[...]
        zc = w[2:3] * z + w[1:2] * z1 + w[0:1] * z2
        db_ref[...] = (da_ * zc).astype(BF16)
        dzc = da_ * b
        dzn = jnp.where(i == nblk - 1, 0.0, dan_ref[...].astype(F32) * bn_ref[...].astype(F32))
        dze = jnp.concatenate([dzc, dzn], axis=0)
        dz = w[2:3] * dzc
        dz = dz + w[1:2] * _band_apply(_band(bs, bs + CONV_HALO, 1, 2), dze)
        dz = dz + w[0:1] * _band_apply(_band(bs, bs + CONV_HALO, 2, 3), dze)
        dc_ref[...] = (dz * v).astype(BF16)
        dv_ref[...] = (dz * c).astype(BF16)
        dw_ref[...] = jnp.zeros_like(dw_ref)
        dw_ref[0:1, :] = _colsum(dzc * z2)
        dw_ref[1:2, :] = _colsum(dzc * z1)
        dw_ref[2:3, :] = _colsum(dzc * z)

    def blk(off):
        return pl.BlockSpec((bs, cb), lambda i, j: (i, off + j))

    def halo(off):
        return pl.BlockSpec((CONV_HALO, cb), lambda i, j: (jnp.maximum(i * hb - 1, 0), off + j))

    def nxt(off):
        return pl.BlockSpec((CONV_HALO, cb), lambda i, j: (jnp.minimum((i + 1) * hb, last_halo), off + j))

    o_spec = pl.BlockSpec((bs, cb), lambda i, j: (i, j))
    return pl.pallas_call(
        body, name=name, grid=(nblk, nj),
        in_specs=[blk(0), blk(nj), blk(2 * nj), halo(nj), halo(2 * nj), nxt(0), o_spec, nxt(0),
                  pl.BlockSpec((3, cb), lambda i, j: (0, j))],
        out_specs=[o_spec, o_spec, o_spec, pl.BlockSpec((None, 8, cb), lambda i, j: (i, 0, j))],
        out_shape=[SDS((S, D), BF16)] * 3 + [SDS((nblk, 8, D), F32)],
        compiler_params=_cp("parallel", "parallel"))(u3, u3, u3, u3, u3, u3, da, da, cw)


def _sum_lead(x, name):
    n, r, C = x.shape

    def body(x_ref, o_ref):
        @pl.when(pl.program_id(0) == 0)
        def _():
            o_ref[...] = x_ref[...]

        @pl.when(pl.program_id(0) > 0)
        def _():
            o_ref[...] += x_ref[...]

    return pl.pallas_call(
        body, name=name, grid=(n,), in_specs=[pl.BlockSpec((None, r, C), lambda i: (i, 0, 0))],
        out_specs=pl.BlockSpec((r, C), lambda i: (0, 0)), out_shape=SDS((r, C), F32),
        compiler_params=_cp("arbitrary"))(x)


def _log_sigmoids(z):
    lb = jnp.minimum(z, 0.0) - jnp.log(1.0 + jnp.exp(-jnp.abs(z)))
    return lb, lb - z


def _sb_attention(qkv, name):
    S, D3 = qkv.shape
    D = D3 // 3
    H = D // HEAD_DIM
    bq = _tile(S, 256, 128)
    nq = S // bq
    scale = HEAD_DIM ** -0.5

    def body(q_ref, k_ref, v_ref, o_ref, lt_ref):
        i = pl.program_id(1)
        q = q_ref[...]
        rowi = lax.broadcasted_iota(jnp.int32, (bq, bq), 0)
        coli = lax.broadcasted_iota(jnp.int32, (bq, bq), 1)
        causal = coli < rowi
        after = jnp.where(rowi > coli, 1.0, 0.0).astype(BF16)

        def block(kb, carry, acc, diag):
            k = k_ref[pl.ds(pl.multiple_of(kb * bq, bq), bq), :]
            v = v_ref[pl.ds(pl.multiple_of(kb * bq, bq), bq), :]
            z = _dot_nt(q, k) * scale
            lb, l1 = _log_sigmoids(z)
            if diag:
                l1 = jnp.where(causal, l1, 0.0)
            suffix = _dot(l1.astype(BF16), after) + carry
            a = jnp.exp(lb + suffix)
            if diag:
                a = jnp.where(causal, a, 0.0)
            acc = acc + _dot(a.astype(BF16), v)
            return carry + jnp.sum(l1, axis=1, keepdims=True), acc

        carry, acc = block(i, jnp.zeros((bq, 1), F32), jnp.zeros((bq, HEAD_DIM), F32), True)

        def step(j, ca):
            return block(i - 1 - j, ca[0], ca[1], False)

        carry, acc = lax.fori_loop(0, i, step, (carry, acc))
        o_ref[...] = acc.astype(BF16)
        lt_ref[...] = jnp.broadcast_to(carry, (bq, HEAD_DIM))

    head_rows = lambda off: pl.BlockSpec((S, HEAD_DIM), lambda hd, i: (0, off + hd))
    blk = pl.BlockSpec((bq, HEAD_DIM), lambda hd, i: (i, hd))
    return pl.pallas_call(
        body, name=name, grid=(H, nq), in_specs=[blk, head_rows(H), head_rows(2 * H)],
        out_specs=[blk, blk], out_shape=[SDS((S, D), BF16), SDS((S, D), F32)],
        compiler_params=_cp("parallel", "arbitrary"))(qkv, qkv, qkv)


def _sb_attention_bwd(qkv, ltot, do, name):
    S, D3 = qkv.shape
    D = D3 // 3
    H = D // HEAD_DIM
    bq = _tile(S, 256, 128)
    nq = S // bq
    scale = HEAD_DIM ** -0.5

    def body(q_ref, k_ref, v_ref, lt_ref, do_ref, dq_ref, dk_ref, dv_ref, dk_acc, dv_acc):
        i = pl.program_id(1)
        q = q_ref[...]
        do_ = do_ref[...]
        lt = lt_ref[:, 0:1]
        rowi = lax.broadcasted_iota(jnp.int32, (bq, bq), 0)
        coli = lax.broadcasted_iota(jnp.int32, (bq, bq), 1)
        causal = coli < rowi
        after = jnp.where(rowi > coli, 1.0, 0.0).astype(BF16)
        before = jnp.where(rowi < coli, 1.0, 0.0).astype(BF16)

        @pl.when(i == 0)
        def _():
            dk_acc[...] = jnp.zeros_like(dk_acc)
            dv_acc[...] = jnp.zeros_like(dv_acc)

        def block(kb, c1, ce, dq, diag):
            rows = pl.ds(pl.multiple_of(kb * bq, bq), bq)
            k = k_ref[rows, :]
            v = v_ref[rows, :]
            z = _dot_nt(q, k) * scale
            lb, l1 = _log_sigmoids(z)
            sig = jnp.exp(lb)
            if diag:
                l1 = jnp.where(causal, l1, 0.0)
            c1 = c1 + jnp.sum(l1, axis=1, keepdims=True)
            a = jnp.exp(lb + (_dot(l1.astype(BF16), after) + (lt - c1)))
            if diag:
                a = jnp.where(causal, a, 0.0)
            e = a * _dot_nt(do_, v)
            p = _dot(e.astype(BF16), before) + ce
            dz = e * (1.0 - sig) - p * sig
            if diag:
                dz = jnp.where(causal, dz, 0.0)
            dzs = (dz * scale).astype(BF16)
            dk_acc[rows, :] += _dot_tn(dzs, q)
            dv_acc[rows, :] += _dot_tn(a.astype(BF16), do_)
            dq = dq + _dot(dzs, k)
            return c1, ce + jnp.sum(e, axis=1, keepdims=True), dq

        def step(kb, st):
            return block(kb, st[0], st[1], st[2], False)

        zero = jnp.zeros((bq, 1), F32)
        c1, ce, dq = lax.fori_loop(0, i, step, (zero, zero, jnp.zeros((bq, HEAD_DIM), F32)))
        _, _, dq = block(i, c1, ce, dq, True)
        dq_ref[...] = dq.astype(BF16)

        @pl.when(i == nq - 1)
        def _():
            dk_ref[...] = dk_acc[...].astype(BF16)
            dv_ref[...] = dv_acc[...].astype(BF16)

    head_rows = lambda off: pl.BlockSpec((S, HEAD_DIM), lambda hd, i: (0, off + hd))
    blk = pl.BlockSpec((bq, HEAD_DIM), lambda hd, i: (i, hd))
    return pl.pallas_call(
        body, name=name, grid=(H, nq), in_specs=[blk, head_rows(H), head_rows(2 * H), blk, blk],
        out_specs=[blk, head_rows(0), head_rows(0)], out_shape=[SDS((S, D), BF16)] * 3,
        scratch_shapes=[pltpu.VMEM((S, HEAD_DIM), F32), pltpu.VMEM((S, HEAD_DIM), F32)],
        compiler_params=_cp("parallel", "arbitrary"))(qkv, qkv, qkv, ltot, do)


def kernel(x, c, norm_mix_g, norm_ffn_g, w_mod, b_mod, pool_w, pool_scale, conv_w_in, conv_w, conv_w_out, sb_w_qkv, sb_w_o, ffn_w_gate, ffn_w_up, ffn_w_down, final_g, loss_target, m_norm_mix_g, m_norm_ffn_g, m_w_mod, m_b_mod, m_pool_w, m_pool_scale, m_conv_w_in, m_conv_w, m_conv_w_out, m_sb_w_qkv, m_sb_w_o, m_ffn_w_gate, m_ffn_w_up, m_ffn_w_down, m_final_g, v_norm_mix_g, v_norm_ffn_g, v_w_mod, v_b_mod, v_pool_w, v_pool_scale, v_conv_w_in, v_conv_w, v_conv_w_out, v_sb_w_qkv, v_sb_w_o, v_ffn_w_gate, v_ffn_w_up, v_ffn_w_down, v_final_g):
    S, D = x.shape[1], x.shape[2]
    L = norm_mix_g.shape[0]
    nmod = w_mod.shape[2]
    nf = ffn_w_gate.shape[2]
    n3 = conv_w_in.shape[2]
    nd = conv_w_out.shape[1]
    cb = n3 // 3
    ng = pool_w.shape[1]
    pg = pool_w.shape[3]
    n_pool = pool_w.shape[0]
    assert D % HEAD_DIM == 0 and S % 256 == 0 and nd == cb and N_CHIPS * nd == D and pg * ng == D

    mx, my, mc = lax.axis_index("x"), lax.axis_index("y"), lax.axis_index("c")
    chip = 2 * mx + my
    dev = 2 * chip + mc
    place = jnp.stack([mc, chip]).astype(jnp.int32)
    hx, ht = x[0], loss_target[0]

    c_all = _allgather8(jnp.broadcast_to(c, (8, D)), "gather_c").reshape(N_DEV, 8, D)[:, 0]
    c_rows = jnp.concatenate([c_all, jnp.zeros((8, D), F32)], axis=0)
    b_cols = lax.dynamic_slice_in_dim(b_mod, chip * nmod, nmod, axis=1).reshape(L, 1, nmod)
    mod_cols = _mod_fwd(c_rows, w_mod, b_cols, "mod_fwd")
    mod_all = _allgather8(mod_cols.reshape(L * 16, nmod), "gather_mod").reshape(N_CHIPS, 2, L, 16, nmod)
    mod = lax.dynamic_index_in_dim(mod_all[:, 0], dev, axis=2, keepdims=False)
    mod = jnp.transpose(mod, (1, 0, 2)).reshape(L, N_MOD, 1, D)

    bf = lambda w: w.astype(BF16)
    n_conv, n_sb = conv_w_in.shape[0], sb_w_qkv.shape[0]
    w_gate, w_up, w_down = [], [], []
    for l in range(L):
        g4, u4, d4 = _gather_shards([bf(ffn_w_gate[l]), bf(ffn_w_up[l]), bf(ffn_w_down[l])], f"gather_ffn{l}")
        w_gate.append(g4)
        w_up.append(u4)
        w_down.append(d4.reshape(N_CHIPS * nf, D))
    w_in, w_out, w_qkv, w_o, w_pool = [], [], [], [], []
    for j in range(n_conv):
        i4, o4 = _gather_shards([bf(conv_w_in[j]), bf(conv_w_out[j])], f"gather_conv{j}")
        w_in.append(i4)
        w_out.append(o4.reshape(D, D))
    for j in range(n_sb):
        i4, o4 = _gather_shards([bf(sb_w_qkv[j]), bf(sb_w_o[j])], f"gather_sb{j}")
        w_qkv.append(i4)
        w_o.append(o4.reshape(D, D))
    for j in range(n_pool):
        (p4,) = _gather_shards([bf(pool_w[j]).reshape(ng * (pg // N_CHIPS), pg)], f"gather_pool{j}")
        w_pool.append(jnp.transpose(p4.reshape(N_CHIPS, ng, pg // N_CHIPS, pg), (1, 0, 2, 3)).reshape(ng, pg, pg))
    taps_cols = jnp.concatenate([pool_scale, conv_w.reshape(-1, nd)], axis=0)
    n_small = taps_cols.shape[0]
    small_rows = jnp.concatenate([taps_cols, jnp.zeros((16 - n_small, nd), F32)], axis=0)
    small_all = _allgather8(small_rows, "gather_small").reshape(N_CHIPS, 2, 16, nd)[:, 0]
    small_full = jnp.transpose(small_all, (1, 0, 2)).reshape(16, D)
    pool_scale_full = small_full[:n_pool]
    conv_taps_full = small_full[n_pool:n_small].reshape(n_conv, 3, D)

    saved = []
    h = hx
    for l in range(L):
        kind, j = l % 3, l // 3
        sh_m, sc_m, gt_m, sh_f, sc_f, gt_f = (mod[l, r] for r in range(N_MOD))
        gm = norm_mix_g[l].reshape(1, D)
        gf = norm_ffn_g[l].reshape(1, D)
        s = {"h_in": h}
        if kind == 0:
            s["diff"] = _pool_pre(h, gm, sc_m, sh_m, f"pool_pre{l}")
            s["cvec_m"] = gt_m * pool_scale_full[j].reshape(1, D)
            h, s["y_m"] = _mm_out_res(s["diff"], w_pool[j], h, s["cvec_m"], f"pool_mm{l}", groups=True)
        elif kind == 1:
            s["u"] = _norm_mod(h, gm, sc_m, sh_m, f"norm_mix{l}")
            s["u3"] = _mm_in(s["u"], w_in[j], n3, f"conv_in{l}")
            s["a_m"] = _conv_mid(s["u3"], conv_taps_full[j], f"conv_mid{l}")
            s["cvec_m"] = gt_m
            h, s["y_m"] = _mm_out_res(s["a_m"], w_out[j], h, gt_m, f"conv_out{l}")
        else:
            s["u"] = _norm_mod(h, gm, sc_m, sh_m, f"norm_mix{l}")
            s["qkv"] = _mm_in(s["u"], w_qkv[j], n3, f"sb_qkv{l}")
            s["o"], s["ltot"] = _sb_attention(s["qkv"], f"sb_attn{l}")
            s["cvec_m"] = gt_m
            h, s["y_m"] = _mm_out_res(s["o"], w_o[j], h, gt_m, f"sb_out{l}")
        s["h_mid"] = h
        s["u2"] = _norm_mod(h, gf, sc_f, sh_f, f"norm_ffn{l}")
        s["gate"] = _mm_in(s["u2"], w_gate[l], nf, f"ffn_gate{l}")
        s["up"], s["a_f"] = _mm_in(s["u2"], w_up[l], nf, f"ffn_up{l}", gate=s["gate"])
        h, s["y_f"] = _mm_out_res(s["a_f"], w_down[l], h, gt_f, f"ffn_down{l}")
        saved.append(s)

    gt_f_last = mod[L - 1, 5]
    dh, dy, st = _loss_head(h, final_g.reshape(1, D), ht, saved[-1]["y_f"], gt_f_last, "loss_head")
    loss = lax.psum(_sum_all(st[2:3], "loss_sum")[0, 0], ("x", "y", "c"))
    d_final_g = st[0:1]
    p_gate_f = st[1:2]
    d_norm_mix, d_norm_ffn = [None] * L, [None] * L
    d_mod = [[None] * N_MOD for _ in range(L)]
    d_pool_scale, d_taps = [None] * n_pool, [None] * n_conv
    big = {}
    for l in reversed(range(L)):
        kind, j = l % 3, l // 3
        s = saved[l]
        sh_m, sc_m, gt_m, sh_f, sc_f, gt_f = (mod[l, r] for r in range(N_MOD))
        gm = norm_mix_g[l].reshape(1, D)
        gf = norm_ffn_g[l].reshape(1, D)
        d_mod[l][5] = p_gate_f
        dgate, dup = _mm_nt(dy, w_down[l], f"ffn_down_bwd{l}", swiglu=(s["gate"], s["up"]))
        gw_down = _mm_tn(s["a_f"], dy, f"ffn_down_wgrad{l}").reshape(N_CHIPS, nf, D)
        gw_gate = _mm_tn(s["u2"], dgate, f"ffn_gate_wgrad{l}", shard_cols=nf)
        gw_up = _mm_tn(s["u2"], dup, f"ffn_up_wgrad{l}", shard_cols=nf)
        du2 = _mm_nt_acc(dgate, w_gate[l], f"ffn_gate_bwd{l}")
        du2 = _mm_nt_acc(dup, w_up[l], f"ffn_up_bwd{l}", add=du2)
        big[("ffn", l)] = _reduce_to_owner([gw_gate, gw_up, gw_down], place, f"reduce_ffn{l}")
        dh, dy, st = _norm_bwd(s["h_mid"], gf, sc_f, du2, dh, f"norm_ffn_bwd{l}", prev=(s["y_m"], s["cvec_m"]))
        d_mod[l][3], d_mod[l][4], d_norm_ffn[l] = st[0:1], st[1:2], st[2:3]
        p_mix = st[3:4]
        if kind == 0:
            d_mod[l][2] = p_mix * pool_scale_full[j].reshape(1, D)
            d_pool_scale[j] = p_mix * gt_m
            dd = _mm_nt(dy, w_pool[j], f"pool_mm_bwd{l}", groups=True)
            gw_pool = _mm_tn(s["diff"], dy, f"pool_wgrad{l}", groups=ng)
            big[("pool", j)] = gw_pool
            du = _pool_post(dd, f"pool_post{l}")
        elif kind == 1:
            d_mod[l][2] = p_mix
            da = _mm_nt(dy, w_out[j], f"conv_out_bwd{l}")
            gw_out = _mm_tn(s["a_m"], dy, f"conv_out_wgrad{l}").reshape(N_CHIPS, nd, D)
            db, dc, dv, dtap = _conv_mid_bwd(s["u3"], da, conv_taps_full[j], f"conv_mid_bwd{l}")
            d_taps[j] = _sum_lead(dtap, f"conv_tap_sum{l}")[0:3]
            du3 = jnp.concatenate([db, dc, dv], axis=1)
            gw_in = _mm_tn(s["u"], du3, f"conv_in_wgrad{l}", shard_cols=n3)
            du = _mm_nt_acc(du3, w_in[j], f"conv_in_bwd{l}")
            big[("conv", j)] = _reduce_to_owner([gw_in, gw_out], place, f"reduce_conv{l}")
        else:
            d_mod[l][2] = p_mix
            do = _mm_nt(dy, w_o[j], f"sb_out_bwd{l}")
            gw_o = _mm_tn(s["o"], dy, f"sb_out_wgrad{l}").reshape(N_CHIPS, nd, D)
            dq, dk, dv = _sb_attention_bwd(s["qkv"], s["ltot"], do, f"sb_attn_bwd{l}")
            dqkv = jnp.concatenate([dq, dk, dv], axis=1)
            gw_qkv = _mm_tn(s["u"], dqkv, f"sb_qkv_wgrad{l}", shard_cols=n3)
            du = _mm_nt_acc(dqkv, w_qkv[j], f"sb_qkv_bwd{l}")
            big[("sb", j)] = _reduce_to_owner([gw_qkv, gw_o], place, f"reduce_sb{l}")
        if l > 0:
            prev = (saved[l - 1]["y_f"], mod[l - 1, 5])
            dh, dy, st = _norm_bwd(s["h_in"], gm, sc_m, du, dh, f"norm_mix_bwd{l}", prev=prev)
            p_gate_f = st[3:4]
        else:
            dh, st = _norm_bwd(s["h_in"], gm, sc_m, du, dh, f"norm_mix_bwd{l}")
        d_mod[l][0], d_mod[l][1], d_norm_mix[l] = st[0:1], st[1:2], st[2:3]
    grad_x = dh.reshape(1, S, D)

    gw_pool = jnp.stack([big[("pool", j)] for j in range(n_pool)])
    gw_pool = jnp.transpose(gw_pool.reshape(n_pool, ng, N_CHIPS, pg // N_CHIPS, pg), (2, 0, 1, 3, 4))
    (g_pool,) = _reduce_to_owner([gw_pool.reshape(N_CHIPS, n_pool * ng * (pg // N_CHIPS), pg)], place, "reduce_pool")

    rows = [d_final_g] + d_norm_mix + d_norm_ffn + [r for l in range(L) for r in d_mod[l]] + d_pool_scale
    rows += [d_taps[j] for j in range(n_conv)]
    vec = jnp.concatenate(rows, axis=0)
    n_rows = vec.shape[0]
    pad = -n_rows % 8
    vec = jnp.concatenate([vec, jnp.zeros((pad, D), F32)], axis=0) if pad else vec
    vec_all = _allgather8(vec, "gather_small_grads").reshape(N_DEV, n_rows + pad, D)
    tot = _sum_devices(vec_all, "sum_small_grads")
    r0 = 1 + 2 * L
    g_final = tot[0]
    g_norm_mix = tot[1:1 + L]
    g_norm_ffn = tot[1 + L:r0]
    g_b_mod = tot[r0:r0 + N_MOD * L].reshape(L, N_MOD * D)
    r1 = r0 + N_MOD * L
    g_pool_scale = lax.dynamic_slice_in_dim(tot[r1:r1 + n_pool], chip * nd, nd, axis=1)
    g_taps = lax.dynamic_slice_in_dim(tot[r1 + n_pool:r1 + n_pool + 3 * n_conv], chip * nd, nd, axis=1)
    g_conv_w = g_taps.reshape(conv_w.shape)
    dmod_all = vec_all[:, r0:r1].reshape(N_DEV, L, N_MOD * D)
    dmod_cols = jnp.transpose(lax.dynamic_slice_in_dim(dmod_all, chip * nmod, nmod, axis=2), (1, 0, 2))
    g_w_mod = _mod_wgrad(jnp.transpose(c_all), dmod_cols, "mod_wgrad")

    g_ffn_gate = jnp.stack([big[("ffn", l)][0] for l in range(L)])
    g_ffn_up = jnp.stack([big[("ffn", l)][1] for l in range(L)])
    g_ffn_down = jnp.stack([big[("ffn", l)][2] for l in range(L)])
    g_conv_in = jnp.stack([big[("conv", j)][0] for j in range(n_conv)])
    g_conv_out = jnp.stack([big[("conv", j)][1] for j in range(n_conv)])
    g_sb_qkv = jnp.stack([big[("sb", j)][0] for j in range(n_sb)])
    g_sb_o = jnp.stack([big[("sb", j)][1] for j in range(n_sb)])
    g_pool_w = g_pool.reshape(pool_w.shape)

    grads = [g_norm_mix, g_norm_ffn, g_w_mod, g_b_mod, g_pool_w, g_pool_scale, g_conv_in, g_conv_w, g_conv_out,
             g_sb_qkv, g_sb_o, g_ffn_gate, g_ffn_up, g_ffn_down, g_final]
    weights = [norm_mix_g, norm_ffn_g, w_mod, b_mod, pool_w, pool_scale, conv_w_in, conv_w, conv_w_out,
               sb_w_qkv, sb_w_o, ffn_w_gate, ffn_w_up, ffn_w_down, final_g]
    ms = [m_norm_mix_g, m_norm_ffn_g, m_w_mod, m_b_mod, m_pool_w, m_pool_scale, m_conv_w_in, m_conv_w, m_conv_w_out,
          m_sb_w_qkv, m_sb_w_o, m_ffn_w_gate, m_ffn_w_up, m_ffn_w_down, m_final_g]
    vs = [v_norm_mix_g, v_norm_ffn_g, v_w_mod, v_b_mod, v_pool_w, v_pool_scale, v_conv_w_in, v_conv_w, v_conv_w_out,
          v_sb_w_qkv, v_sb_w_o, v_ffn_w_gate, v_ffn_w_up, v_ffn_w_down, v_final_g]
    deltas, new_ms, new_vs = [], [], []
    for n, (w, g, m, v) in enumerate(zip(weights, grads, ms, vs)):
        if w.ndim == 1:
            w, g, m, v = (a.reshape(1, -1) for a in (w, g, m, v))
        g = g.reshape(w.shape)
        grads[n] = g.reshape(weights[n].shape)
        d, nm, nv = _adamw(w, g, m, v, f"adamw{n}")
        deltas.append(d.reshape(weights[n].shape))
        new_ms.append(nm.reshape(weights[n].shape))
        new_vs.append(nv.reshape(weights[n].shape))
    return (loss, grad_x, *grads, *deltas, *new_ms, *new_vs)
```

```python
import functools

import jax
import jax.numpy as jnp
from jax import lax
from jax.experimental import pallas as pl
from jax.experimental.pallas import tpu as pltpu

F32 = jnp.float32
BF16 = jnp.bfloat16
SDS = jax.ShapeDtypeStruct
MESH = pl.DeviceIdType.MESH

RMS_EPS = 1e-6
POOL_WINDOWS = (2, 4, 8, 16)
POOL_HALO = 16
CONV_HALO = 16
HEAD_DIM = 128
N_MOD = 6
N_CHIPS = 4
N_DEV = 8
ADAM_LR = 0.001
ADAM_B1 = 0.9
ADAM_B2 = 0.999
ADAM_EPS = 1e-08
ADAM_WD = 0.01
ADAM_STEP = 10
VMEM_LIMIT_V7X = 52 * 1024 * 1024
ANY = pl.BlockSpec(memory_space=pl.ANY)
VMEM_WHOLE = pl.BlockSpec(memory_space=pltpu.VMEM)


def _cp(*sem):
    return pltpu.CompilerParams(dimension_semantics=sem, vmem_limit_bytes=VMEM_LIMIT_V7X)


def _tile(n, pref, unit):
    if n <= pref:
        return n
    t = (pref // unit) * unit
    while t >= unit:
        if n % t == 0:
            return t
        t -= unit
    return n


def _dot(a, b):
    return jnp.dot(a, b, preferred_element_type=F32)


def _dot_nt(a, b):
    return lax.dot_general(a, b, (((1,), (1,)), ((), ())), preferred_element_type=F32)


def _dot_tn(a, b):
    return lax.dot_general(a, b, (((0,), (0,)), ((), ())), preferred_element_type=F32)


def _split_bf16(x):
    hi = x.astype(BF16)
    lo = (x - hi.astype(F32)).astype(BF16)
    return hi, lo


def _sigmoid(x):
    return 1.0 / (1.0 + jnp.exp(-x))


def _my_place():
    return lax.axis_index("x"), lax.axis_index("y"), lax.axis_index("c")


def _allgather8(blk, name):
    m, n = blk.shape

    def body(x_ref, out_ref, send_sems, recv_sems, local_sem):
        x, y, c = _my_place()
        me, sibling = (x, y, c), (x, y, 1 - c)
        chips = [(1 - x, y), (x, 1 - y), (1 - x, 1 - y)]

        def rows(px, py, pc):
            return out_ref.at[pl.ds((4 * px + 2 * py + pc) * m, m), :]

        def copy(k, block, to, src=None):
            return pltpu.make_async_remote_copy(
                src_ref=rows(*block) if src is None else src, dst_ref=rows(*block),
                send_sem=send_sems.at[k], recv_sem=recv_sems.at[k], device_id=to, device_id_type=MESH)

        mine = pltpu.make_async_copy(x_ref, rows(*me), local_sem)
        mine.start()
        first = [copy(0, me, sibling, src=x_ref)]
        first += [copy(1 + j, me, (*chip, c), src=x_ref) for j, chip in enumerate(chips)]
        for cp in first:
            cp.start()
        passed = [copy(4 + j, (*chip, c), sibling) for j, chip in enumerate(chips)]
        for j, chip in enumerate(chips):
            copy(1 + j, (*chip, c), me).wait_recv()
            passed[j].start()
        copy(0, sibling, me).wait_recv()
        for j, chip in enumerate(chips):
            copy(4 + j, (*chip, 1 - c), me).wait_recv()
        for cp in first + passed:
            cp.wait_send()
        mine.wait()

    return pl.pallas_call(
        body, name=name, out_shape=SDS((N_DEV * m, n), blk.dtype),
        in_specs=[VMEM_WHOLE], out_specs=VMEM_WHOLE,
        scratch_shapes=[pltpu.SemaphoreType.DMA((7,)), pltpu.SemaphoreType.DMA((7,)), pltpu.SemaphoreType.DMA],
    )(blk)


def _gather_shards(ws, name):
    nt = len(ws)

    def body(*refs):
        w_refs, out_refs = refs[:nt], refs[nt:2 * nt]
        send_sems, recv_sems, local_sems = refs[2 * nt:]
        x, y, c = _my_place()
        sibling = (x, y, 1 - c)
        chips = [(1 - x, y), (x, 1 - y), (1 - x, 1 - y)]
        k_me = 2 * x + y
        copies = []
        for t in range(nt):
            half = ws[t].shape[0] // 2
            w_ref, out_ref = w_refs[t], out_refs[t]

            def dst(k, hc, out_ref=out_ref, half=half):
                return out_ref.at[k, pl.ds(hc * half, half), :]

            def copy(s, src, to_dst, to, t=t):
                return pltpu.make_async_remote_copy(
                    src_ref=src, dst_ref=to_dst, send_sem=send_sems.at[6 * t + s], recv_sem=recv_sems.at[6 * t + s],
                    device_id=to, device_id_type=MESH)

            mine = pltpu.make_async_copy(w_ref, out_ref.at[k_me], local_sems.at[t])
            mine.start()
            first = [copy(j, w_ref.at[pl.ds(c * half, half), :], dst(k_me, c), (*chip, c))
                     for j, chip in enumerate(chips)]
            for cp in first:
                cp.start()
            copies.append((mine, first, dst, copy))
        sends = []
        for t in range(nt):
            mine, first, dst, copy = copies[t]
            passed = []
            for j, (px, py) in enumerate(chips):
                landed = dst(2 * px + py, c)
                copy(j, landed, landed, (px, py, c)).wait_recv()
                fwd = copy(3 + j, landed, landed, sibling)
                fwd.start()
                passed.append(fwd)
            sends.append((mine, first + passed))
        for t in range(nt):
            _, _, dst, copy = copies[t]
            for j, (px, py) in enumerate(chips):
                other = dst(2 * px + py, 1 - c)
                copy(3 + j, other, other, sibling).wait_recv()
        for mine, cps in sends:
            for cp in cps:
                cp.wait_send()
            mine.wait()

    return pl.pallas_call(
        body, name=name,
        out_shape=[SDS((N_CHIPS,) + w.shape, w.dtype) for w in ws],
        in_specs=[ANY] * nt, out_specs=[ANY] * nt,
        scratch_shapes=[pltpu.SemaphoreType.DMA((6 * nt,)), pltpu.SemaphoreType.DMA((6 * nt,)),
                        pltpu.SemaphoreType.DMA((nt,))],
    )(*ws)


def _swap_sibling_halves(gs, name):
    nt = len(gs)

    def body(*refs):
        g_refs, out_refs = refs[:nt], refs[nt:2 * nt]
        send_sems, recv_sems = refs[2 * nt:]
        x, y, c = _my_place()
        cps = []
        for t in range(nt):
            half = gs[t].shape[1] // 2
            cp = pltpu.make_async_remote_copy(
                src_ref=g_refs[t].at[:, pl.ds((1 - c) * half, half), :], dst_ref=out_refs[t],
                send_sem=send_sems.at[t], recv_sem=recv_sems.at[t], device_id=(x, y, 1 - c), device_id_type=MESH)
            cp.start()
            cps.append(cp)
        for cp in cps:
            cp.wait()

    return pl.pallas_call(
        body, name=name,
        out_shape=[SDS((g.shape[0], g.shape[1] // 2, g.shape[2]), g.dtype) for g in gs],
        in_specs=[ANY] * nt, out_specs=[ANY] * nt,
        scratch_shapes=[pltpu.SemaphoreType.DMA((nt,)), pltpu.SemaphoreType.DMA((nt,))],
    )(*gs)


def _scatter_to_chips(ps, name):
    nt = len(ps)

    def body(*refs):
        p_refs, out_refs = refs[:nt], refs[nt:2 * nt]
        send_sems, recv_sems = refs[2 * nt:]
        x, y, c = _my_place()
        chips = [(1 - x, y), (x, 1 - y), (1 - x, 1 - y)]
        cps = []
        for t in range(nt):
            for j, (px, py) in enumerate(chips):
                cp = pltpu.make_async_remote_copy(
                    src_ref=p_refs[t].at[2 * px + py], dst_ref=out_refs[t].at[j],
                    send_sem=send_sems.at[3 * t + j], recv_sem=recv_sems.at[3 * t + j], device_id=(px, py, c),
                    device_id_type=MESH)
                cp.start()
                cps.append(cp)
        for cp in cps:
            cp.wait()

    return pl.pallas_call(
        body, name=name,
        out_shape=[SDS((3,) + p.shape[1:], p.dtype) for p in ps],
        in_specs=[ANY] * nt, out_specs=[ANY] * nt,
        scratch_shapes=[pltpu.SemaphoreType.DMA((3 * nt,)), pltpu.SemaphoreType.DMA((3 * nt,))],
    )(*ps)


def _join_sibling_halves(fs, name):
    nt = len(fs)

    def body(*refs):
        out_refs = refs[nt:2 * nt]
        send_sems, recv_sems = refs[2 * nt:]
        x, y, c = _my_place()
        cps = []
        for t in range(nt):
            r = fs[t].shape[0] // 2
            mine = out_refs[t].at[pl.ds(c * r, r), :]
            cp = pltpu.make_async_remote_copy(
                src_ref=mine, dst_ref=mine, send_sem=send_sems.at[t], recv_sem=recv_sems.at[t],
                device_id=(x, y, 1 - c), device_id_type=MESH)
            cp.start()
            cps.append((cp, r))
        for t, (cp, r) in enumerate(cps):
            cp.wait_send()
            other = out_refs[t].at[pl.ds((1 - c) * r, r), :]
            pltpu.make_async_remote_copy(
                src_ref=other, dst_ref=other, send_sem=send_sems.at[t], recv_sem=recv_sems.at[t],
                device_id=(x, y, 1 - c), device_id_type=MESH).wait_recv()

    return pl.pallas_call(
        body, name=name, out_shape=[SDS(f.shape, f.dtype) for f in fs],
        in_specs=[ANY] * nt, out_specs=[ANY] * nt, input_output_aliases={t: t for t in range(nt)},
        scratch_shapes=[pltpu.SemaphoreType.DMA((nt,)), pltpu.SemaphoreType.DMA((nt,))],
    )(*fs)


def _add_sibling(g, recv, name):
    _, R, C = g.shape
    half = R // 2
    br = _tile(half, max(16, (1 << 19) // C), 16)
    nrb = half // br

    def body(g_ref, r_ref, bf_ref, own_ref):
        s = g_ref[...] + r_ref[...]
        bf_ref[...] = s.astype(BF16)

        @pl.when(pl.program_id(1) == 2 * lax.axis_index("x") + lax.axis_index("y"))
        def _():
            own_ref[...] = s

    return pl.pallas_call(
        body, name=name, grid=(nrb, N_CHIPS),
        in_specs=[pl.BlockSpec((None, br, C), lambda i, k: (k, lax.axis_index("c") * nrb + i, 0)),
                  pl.BlockSpec((None, br, C), lambda i, k: (k, i, 0))],
        out_specs=[pl.BlockSpec((None, br, C), lambda i, k: (k, i, 0)),
                   pl.BlockSpec((br, C), lambda i, k: (i, 0))],
        out_shape=[SDS((N_CHIPS, half, C), BF16), SDS((half, C), F32)],
        compiler_params=_cp("arbitrary", "arbitrary"),
    )(g, recv)


def _add_chips(own, recv, name):
    r, C = own.shape
    br = _tile(r, max(16, (1 << 19) // C), 16)
    nrb = r // br

    def body(own_ref, r_ref, o_ref):
        s = own_ref[...]
        for j in range(3):
            s = s + r_ref[j].astype(F32)
        o_ref[...] = s

    return pl.pallas_call(
        body, name=name, grid=(nrb,),
        in_specs=[pl.BlockSpec((br, C), lambda i: (i, 0)), pl.BlockSpec((3, br, C), lambda i: (0, i, 0))],
        out_specs=pl.BlockSpec((br, C), lambda i: (lax.axis_index("c") * nrb + i, 0)),
        out_shape=SDS((2 * r, C), F32), compiler_params=_cp("arbitrary"),
    )(own, recv)


def _reduce_to_owner(gs, name):
    recv1 = _swap_sibling_halves(gs, name + "_swap")
    parts = [_add_sibling(g, r, name + "_add1") for g, r in zip(gs, recv1)]
    recv2 = _scatter_to_chips([p[0] for p in parts], name + "_scatter")
    fins = [_add_chips(p[1], r, name + "_add2") for p, r in zip(parts, recv2)]
    return _join_sibling_halves(fins, name + "_join")


def _sum_devices(allv, name):
    _, r, n = allv.shape

    def body(a_ref, o_ref):
        s = a_ref[0]
        for d in range(1, N_DEV):
            s = s + a_ref[d]
        o_ref[...] = s

    return pl.pallas_call(body, name=name, out_shape=SDS((r, n), F32), in_specs=[VMEM_WHOLE],
                          out_specs=VMEM_WHOLE)(allv)


def _adamw(w, g, m, v, name):
    shape = w.shape
    C = shape[-1]
    R = w.size // C
    args = [a.reshape(R, C) for a in (w, g, m, v)]
    br = _tile(R, max(8, (1 << 18) // C), 8)

    def body(w_ref, g_ref, m_ref, v_ref, d_ref, nm_ref, nv_ref):
        g_ = g_ref[...]
        m_ = ADAM_B1 * m_ref[...] + (1.0 - ADAM_B1) * g_
        v_ = ADAM_B2 * v_ref[...] + (1.0 - ADAM_B2) * (g_ * g_)
        m_hat = m_ / (1.0 - ADAM_B1 ** ADAM_STEP)
        v_hat = v_ / (1.0 - ADAM_B2 ** ADAM_STEP)
        d_ref[...] = -ADAM_LR * (m_hat / (jnp.sqrt(v_hat) + ADAM_EPS) + ADAM_WD * w_ref[...])
        nm_ref[...] = m_
        nv_ref[...] = v_

    spec = pl.BlockSpec((br, C), lambda i: (i, 0))
    outs = pl.pallas_call(
        body, name=name, grid=(R // br,), in_specs=[spec] * 4, out_specs=[spec] * 3,
        out_shape=[SDS((R, C), F32)] * 3, compiler_params=_cp("parallel"),
    )(*args)
    return [o.reshape(shape) for o in outs]


def _mod_fwd(c_rows, w_mod, b_cols, name):
    L, D, n = w_mod.shape
    bn = _tile(n, 512, 128)

    def body(c_ref, w_ref, b_ref, o_ref):
        cc = c_ref[...]
        sc = (cc * _sigmoid(cc)).astype(BF16)
        o_ref[...] = _dot(sc, w_ref[...].astype(BF16)) + b_ref[...]

    return pl.pallas_call(
        body, name=name, grid=(L, n // bn),
        in_specs=[pl.BlockSpec((16, D), lambda l, j: (0, 0)),
                  pl.BlockSpec((None, D, bn), lambda l, j: (l, 0, j)),
                  pl.BlockSpec((None, 1, bn), lambda l, j: (l, 0, j))],
        out_specs=pl.BlockSpec((None, 16, bn), lambda l, j: (l, 0, j)),
        out_shape=SDS((L, 16, n), F32), compiler_params=_cp("parallel", "parallel"),
    )(c_rows, w_mod, b_cols)


def _mod_wgrad(c_cols, dmod, name):
    D = c_cols.shape[0]
    L, _, n = dmod.shape
    bd = _tile(D, 512, 8)
    bn = _tile(n, 512, 128)

    def body(c_ref, d_ref, o_ref):
        cc = c_ref[...]
        sc = cc * _sigmoid(cc)
        dm = d_ref[...]
        acc = sc[:, 0:1] * dm[0:1, :]
        for b in range(1, N_DEV):
            acc = acc + sc[:, b:b + 1] * dm[b:b + 1, :]
        o_ref[...] = acc

    return pl.pallas_call(
        body, name=name, grid=(L, D // bd, n // bn),
        in_specs=[pl.BlockSpec((bd, N_DEV), lambda l, i, j: (i, 0)),
                  pl.BlockSpec((None, N_DEV, bn), lambda l, i, j: (l, 0, j))],
        out_specs=pl.BlockSpec((None, bd, bn), lambda l, i, j: (l, i, j)),
        out_shape=SDS((L, D, n), F32), compiler_params=_cp("parallel", "parallel", "parallel"),
    )(c_cols, dmod)


def _mm_in(x, w4, bn, name, gate=None):
    M, K = x.shape
    nsh, _, n = w4.shape
    N = nsh * n
    nb = n // bn
    bm = _tile(M, 512, 16)
    x_spec = pl.BlockSpec((bm, K), lambda j, i: (i, 0))
    w_spec = pl.BlockSpec((None, K, bn), lambda j, i: (j // nb, 0, j % nb))
    o_spec = pl.BlockSpec((bm, bn), lambda j, i: (i, j))
    if gate is None:
        def body(x_ref, w_ref, o_ref):
            o_ref[...] = _dot(x_ref[...], w_ref[...]).astype(BF16)

        return pl.pallas_call(
            body, name=name, grid=(N // bn, M // bm), in_specs=[x_spec, w_spec], out_specs=o_spec,
            out_shape=SDS((M, N), BF16), compiler_params=_cp("parallel", "parallel"))(x, w4)

    def body_gated(x_ref, w_ref, g_ref, up_ref, a_ref):
        up = _dot(x_ref[...], w_ref[...])
        g = g_ref[...].astype(F32)
        up_ref[...] = up.astype(BF16)
        a_ref[...] = (g * _sigmoid(g) * up).astype(BF16)

    return pl.pallas_call(
        body_gated, name=name, grid=(N // bn, M // bm), in_specs=[x_spec, w_spec, o_spec],
        out_specs=[o_spec, o_spec], out_shape=[SDS((M, N), BF16)] * 2,
        compiler_params=_cp("parallel", "parallel"))(x, w4, gate)


def _mm_out_res(a, w, h, cvec, name, groups=False):
    M = a.shape[0]
    N = h.shape[1]
    if groups:
        bn = w.shape[2]
        a_spec = pl.BlockSpec((_tile(M, 512, 16), w.shape[1]), lambda j, i: (i, j))
        w_spec = pl.BlockSpec((None, w.shape[1], bn), lambda j, i: (j, 0, 0))
    else:
        bn = _tile(N, 512, 128)
        a_spec = pl.BlockSpec((_tile(M, 512, 16), a.shape[1]), lambda j, i: (i, 0))
        w_spec = pl.BlockSpec((a.shape[1], bn), lambda j, i: (0, j))
    bm = _tile(M, 512, 16)
    o_spec = pl.BlockSpec((bm, bn), lambda j, i: (i, j))

    def body(a_ref, w_ref, h_ref, c_ref, hn_ref, y_ref):
        y = _dot(a_ref[...], w_ref[...])
        hn_ref[...] = h_ref[...] + c_ref[...] * y
        y_ref[...] = y.astype(BF16)

    return pl.pallas_call(
        body, name=name, grid=(N // bn, M // bm),
        in_specs=[a_spec, w_spec, o_spec, pl.BlockSpec((1, bn), lambda j, i: (0, j))],
        out_specs=[o_spec, o_spec], out_shape=[SDS((M, N), F32), SDS((M, N), BF16)],
        compiler_params=_cp("parallel", "parallel"))(a, w, h, cvec)


def _mm_nt(dy, w, name, groups=False, swiglu=None):
    M = dy.shape[0]
    bm = _tile(M, 1024, 16)
    if groups:
        N = dy.shape[1]
        bn = w.shape[1]
        dy_spec = pl.BlockSpec((bm, w.shape[2]), lambda j, i: (i, j))
        w_spec = pl.BlockSpec((None, bn, w.shape[2]), lambda j, i: (j, 0, 0))
    else:
        N = w.shape[0]
        bn = _tile(N, 512, 128)
        dy_spec = pl.BlockSpec((bm, dy.shape[1]), lambda j, i: (i, 0))
        w_spec = pl.BlockSpec((bn, w.shape[1]), lambda j, i: (j, 0))
    o_spec = pl.BlockSpec((bm, bn), lambda j, i: (i, j))
    if swiglu is None:
        def body(dy_ref, w_ref, o_ref):
            o_ref[...] = _dot_nt(dy_ref[...], w_ref[...]).astype(BF16)

        return pl.pallas_call(
            body, name=name, grid=(N // bn, M // bm), in_specs=[dy_spec, w_spec], out_specs=o_spec,
            out_shape=SDS((M, N), BF16), compiler_params=_cp("parallel", "parallel"))(dy, w)

    def body_swiglu(dy_ref, w_ref, g_ref, u_ref, dg_ref, du_ref):
        da = _dot_nt(dy_ref[...], w_ref[...])
        g = g_ref[...].astype(F32)
        sg = _sigmoid(g)
        silu = g * sg
        dg_ref[...] = (da * u_ref[...].astype(F32) * (sg + silu * (1.0 - sg))).astype(BF16)
        du_ref[...] = (da * silu).astype(BF16)

    return pl.pallas_call(
        body_swiglu, name=name, grid=(N // bn, M // bm), in_specs=[dy_spec, w_spec, o_spec, o_spec],
        out_specs=[o_spec, o_spec], out_shape=[SDS((M, N), BF16)] * 2,
        compiler_params=_cp("parallel", "parallel"))(dy, w, *swiglu)


def _mm_nt_acc(dx, w4, name, add=None):
    M = dx.shape[0]
    nc, K, n = w4.shape
    bm = _tile(M, 512, 16)
    with_add = add is not None

    def body(*refs):
        dx_ref, w_ref = refs[:2]
        o_ref, acc_ref = refs[-2:]
        c = pl.program_id(1)
        s = _dot_nt(dx_ref[...], w_ref[...])

        @pl.when(c == 0)
        def _():
            acc_ref[...] = s + refs[2][...].astype(F32) if with_add else s

        @pl.when(c > 0)
        def _():
            acc_ref[...] += s

        @pl.when(c == nc - 1)
        def _():
            o_ref[...] = acc_ref[...].astype(BF16)

    o_spec = pl.BlockSpec((bm, K), lambda i, c: (i, 0))
    in_specs = [pl.BlockSpec((bm, n), lambda i, c: (i, c)), pl.BlockSpec((None, K, n), lambda i, c: (c, 0, 0))]
    args = [dx, w4]
    if with_add:
        in_specs.append(o_spec)
        args.append(add)
    return pl.pallas_call(
        body, name=name, grid=(M // bm, nc), in_specs=in_specs, out_specs=o_spec, out_shape=SDS((M, K), BF16),
        scratch_shapes=[pltpu.VMEM((bm, K), F32)], compiler_params=_cp("parallel", "arbitrary"))(*args)


def _mm_tn(x, dy, name, shard_cols=None, groups=None):
    M, K = x.shape
    N = dy.shape[1]
    bm = _tile(M, 1024, 16)
    if groups is not None:
        kg, ng = K // groups, N // groups
        grid = (groups, 1, M // bm)
        x_spec = pl.BlockSpec((bm, kg), lambda i, j, s: (s, i))
        dy_spec = pl.BlockSpec((bm, ng), lambda i, j, s: (s, i))
        o_spec = pl.BlockSpec((None, kg, ng), lambda i, j, s: (i, 0, 0))
        out_shape = SDS((groups, kg, ng), F32)
    else:
        bko = _tile(K, 1408, 128)
        if shard_cols is not None:
            bn = _tile(shard_cols, 1536, 128)
            nb = shard_cols // bn
            o_spec = pl.BlockSpec((None, bko, bn), lambda i, j, s: (j // nb, i, j % nb))
            out_shape = SDS((N_CHIPS, K, shard_cols), F32)
        else:
            bn = _tile(N, 1024, 128)
            o_spec = pl.BlockSpec((bko, bn), lambda i, j, s: (i, j))
            out_shape = SDS((K, N), F32)
        grid = (K // bko, N // bn, M // bm)
        x_spec = pl.BlockSpec((bm, bko), lambda i, j, s: (s, i))
        dy_spec = pl.BlockSpec((bm, bn), lambda i, j, s: (s, j))

    def body(x_ref, dy_ref, o_ref):
        p = _dot_tn(x_ref[...], dy_ref[...])

        @pl.when(pl.program_id(2) == 0)
        def _():
            o_ref[...] = p

        @pl.when(pl.program_id(2) > 0)
        def _():
            o_ref[...] += p

    return pl.pallas_call(
        body, name=name, grid=grid, in_specs=[x_spec, dy_spec], out_specs=o_spec, out_shape=out_shape,
        compiler_params=_cp("parallel", "parallel", "arbitrary"))(x, dy)


def _norm_mod_rows(h, g, scale, shift):
    r = lax.rsqrt(jnp.mean(h * h, axis=-1, keepdims=True) + RMS_EPS)
    return (h * r) * g * (1.0 + scale) + shift


def _vec_spec(D):
    return pl.BlockSpec((1, D), lambda i: (0, 0))


def _norm_mod(h, g, scale, shift, name):
    S, D = h.shape
    bs = _tile(S, 512, 16)

    def body(h_ref, g_ref, sc_ref, sh_ref, u_ref):
        u_ref[...] = _norm_mod_rows(h_ref[...], g_ref[...], sc_ref[...], sh_ref[...]).astype(BF16)

    row = pl.BlockSpec((bs, D), lambda i: (i, 0))
    return pl.pallas_call(
        body, name=name, grid=(S // bs,), in_specs=[row, _vec_spec(D), _vec_spec(D), _vec_spec(D)],
        out_specs=row, out_shape=SDS((S, D), BF16), compiler_params=_cp("parallel"))(h, g, scale, shift)


def _band(rows, cols, lo, hi):
    d = lax.broadcasted_iota(jnp.int32, (rows, cols), 1) - lax.broadcasted_iota(jnp.int32, (rows, cols), 0)
    return jnp.where((d >= lo) & (d < hi), 1.0, 0.0).astype(BF16)


def _band_apply(band, x):
    hi, lo = _split_bf16(x)
    return _dot(band, hi) + _dot(band, lo)


def _pool_pre(h, g, scale, shift, name):
    S, D = h.shape
    ng = len(POOL_WINDOWS)
    pg = D // ng
    bs = _tile(S, 256, POOL_HALO)
    hb = bs // POOL_HALO

    def body(h_ref, hh_ref, g_ref, sc_ref, sh_ref, o_ref):
        i = pl.program_id(0)
        u = _norm_mod_rows(h_ref[...], g_ref[...], sc_ref[...], sh_ref[...])
        uh = _norm_mod_rows(hh_ref[...], g_ref[...], sc_ref[...], sh_ref[...])
        uh = jnp.where(i == 0, 0.0, uh)
        ue = jnp.concatenate([uh, u], axis=0)
        t = i * bs + lax.broadcasted_iota(jnp.int32, (bs, 1), 0)
        for gi, w in enumerate(POOL_WINDOWS):
            cols = slice(gi * pg, (gi + 1) * pg)
            band = _band(bs, bs + POOL_HALO, POOL_HALO - w + 1, POOL_HALO + 1)
            inv = 1.0 / jnp.minimum(t + 1, w).astype(F32)
            o_ref[:, cols] = (_band_apply(band, ue[:, cols]) * inv - u[:, cols]).astype(BF16)

    row = pl.BlockSpec((bs, D), lambda i: (i, 0))
    halo = pl.BlockSpec((POOL_HALO, D), lambda i: (jnp.maximum(i * hb - 1, 0), 0))
    return pl.pallas_call(
        body, name=name, grid=(S // bs,),
        in_specs=[row, halo, _vec_spec(D), _vec_spec(D), _vec_spec(D)],
        out_specs=row, out_shape=SDS((S, D), BF16), compiler_params=_cp("parallel"))(h, h, g, scale, shift)


def _pool_post(dd, name):
    S, D = dd.shape
    ng = len(POOL_WINDOWS)
    pg = D // ng
    bs = _tile(S, 256, POOL_HALO)
    hb = bs // POOL_HALO
    nblk = S // bs

    def body(d_ref, dn_ref, o_ref):
        i = pl.program_id(0)
        d = d_ref[...].astype(F32)
        dn = jnp.where(i == nblk - 1, 0.0, dn_ref[...].astype(F32))
        de = jnp.concatenate([d, dn], axis=0)
        t = i * bs + lax.broadcasted_iota(jnp.int32, (bs + POOL_HALO, 1), 0)
        for gi, w in enumerate(POOL_WINDOWS):
            cols = slice(gi * pg, (gi + 1) * pg)
            inv = 1.0 / jnp.minimum(t + 1, w).astype(F32)
            band = _band(bs, bs + POOL_HALO, 0, w)
            o_ref[:, cols] = (_band_apply(band, de[:, cols] * inv) - d[:, cols]).astype(BF16)

    row = pl.BlockSpec((bs, D), lambda i: (i, 0))
    nxt = pl.BlockSpec((POOL_HALO, D), lambda i: (jnp.minimum((i + 1) * hb, S // POOL_HALO - 1), 0))
    return pl.pallas_call(
        body, name=name, grid=(nblk,), in_specs=[row, nxt], out_specs=row, out_shape=SDS((S, D), BF16),
        compiler_params=_cp("parallel"))(dd, dd)


def _colsum(x):
    return jnp.sum(x, axis=0, keepdims=True)


def _accumulate_rows(st_ref, rows, first):
    @pl.when(first)
    def _():
        st_ref[...] = jnp.zeros_like(st_ref)

    for r, row in enumerate(rows):
        st_ref[r:r + 1, :] += row


def _norm_bwd(h, g, scale, du, dh_out, name, prev=None):
    S, D = h.shape
    bs = _tile(S, 256, 16)
    with_prev = prev is not None

    def body(*refs):
        h_ref, g_ref, sc_ref, du_ref, dho_ref = refs[:5]
        if with_prev:
            y_ref, cv_ref, dh_ref, dy_ref, st_ref = refs[5:]
        else:
            dh_ref, st_ref = refs[5:]
        hh = h_ref[...]
        du_ = du_ref[...].astype(F32)
        r = lax.rsqrt(jnp.mean(hh * hh, axis=-1, keepdims=True) + RMS_EPS)
        xhat = hh * r
        dn = du_ * (1.0 + sc_ref[...])
        dxhat = dn * g_ref[...]
        dh = dho_ref[...] + r * (dxhat - xhat * jnp.mean(dxhat * xhat, axis=-1, keepdims=True))
        dh_ref[...] = dh
        rows = [_colsum(du_), _colsum(du_ * (xhat * g_ref[...])), _colsum(dn * xhat)]
        if with_prev:
            dy_ref[...] = (dh * cv_ref[...]).astype(BF16)
            rows.append(_colsum(dh * y_ref[...].astype(F32)))
        _accumulate_rows(st_ref, rows, pl.program_id(0) == 0)

    row = pl.BlockSpec((bs, D), lambda i: (i, 0))
    st_spec = pl.BlockSpec((8, D), lambda i: (0, 0))
    in_specs = [row, _vec_spec(D), _vec_spec(D), row, row]
    args = [h, g, scale, du, dh_out]
    out_specs, out_shape = [row], [SDS((S, D), F32)]
    if with_prev:
        in_specs += [row, _vec_spec(D)]
        args += list(prev)
        out_specs.append(row)
        out_shape.append(SDS((S, D), BF16))
    out_specs.append(st_spec)
    out_shape.append(SDS((8, D), F32))
    return pl.pallas_call(
        body, name=name, grid=(S // bs,), in_specs=in_specs, out_specs=out_specs, out_shape=out_shape,
        compiler_params=_cp("arbitrary"))(*args)


def _loss_head(h, g, target, y, cvec, name):
    S, D = h.shape
    bs = _tile(S, 256, 16)

    def body(h_ref, g_ref, t_ref, y_ref, cv_ref, dh_ref, dy_ref, st_ref):
        hh = h_ref[...]
        r = lax.rsqrt(jnp.mean(hh * hh, axis=-1, keepdims=True) + RMS_EPS)
        xhat = hh * r
        err = xhat * g_ref[...] - t_ref[...]
        dout = err * (1.0 / D)
        dxhat = dout * g_ref[...]
        dh = r * (dxhat - xhat * jnp.mean(dxhat * xhat, axis=-1, keepdims=True))
        dh_ref[...] = dh
        dy_ref[...] = (dh * cv_ref[...]).astype(BF16)
        rows = [_colsum(dout * xhat), _colsum(dh * y_ref[...].astype(F32)), _colsum(err * err) * (0.5 / D)]
        _accumulate_rows(st_ref, rows, pl.program_id(0) == 0)

    row = pl.BlockSpec((bs, D), lambda i: (i, 0))
    return pl.pallas_call(
        body, name=name, grid=(S // bs,), in_specs=[row, _vec_spec(D), row, row, _vec_spec(D)],
        out_specs=[row, row, pl.BlockSpec((8, D), lambda i: (0, 0))],
        out_shape=[SDS((S, D), F32), SDS((S, D), BF16), SDS((8, D), F32)],
        compiler_params=_cp("arbitrary"))(h, g, target, y, cvec)


def _sum_all(x, name):
    def body(x_ref, o_ref):
        o_ref[...] = jnp.sum(jnp.sum(x_ref[...], axis=1, keepdims=True), axis=0, keepdims=True)

    return pl.pallas_call(body, name=name, out_shape=SDS((1, 1), F32), in_specs=[VMEM_WHOLE],
                          out_specs=VMEM_WHOLE)(x)


def _conv_mid(u3, cw, name):
    S, D3 = u3.shape
    D = D3 // 3
    cb = _tile(D, 512, 128)
    nj = D // cb
    bs = _tile(S, 256, CONV_HALO)
    hb = bs // CONV_HALO

    def body(b_ref, c_ref, v_ref, ch_ref, vh_ref, w_ref, o_ref):
        i = pl.program_id(0)
        z = c_ref[...].astype(F32) * v_ref[...].astype(F32)
        zh = jnp.where(i == 0, 0.0, ch_ref[...].astype(F32) * vh_ref[...].astype(F32))
        ze = jnp.concatenate([zh, z], axis=0)
        w = w_ref[...]
        zc = w[2:3] * z
        zc = zc + w[1:2] * _band_apply(_band(bs, bs + CONV_HALO, CONV_HALO - 1, CONV_HALO), ze)
        zc = zc + w[0:1] * _band_apply(_band(bs, bs + CONV_HALO, CONV_HALO - 2, CONV_HALO - 1), ze)
        o_ref[...] = (b_ref[...].astype(F32) * zc).astype(BF16)

    def blk(off):
        return pl.BlockSpec((bs, cb), lambda i, j: (i, off + j))

    def halo(off):
        return pl.BlockSpec((CONV_HALO, cb), lambda i, j: (jnp.maximum(i * hb - 1, 0), off + j))

    return pl.pallas_call(
        body, name=name, grid=(S // bs, nj),
        in_specs=[blk(0), blk(nj), blk(2 * nj), halo(nj), halo(2 * nj), pl.BlockSpec((3, cb), lambda i, j: (0, j))],
        out_specs=pl.BlockSpec((bs, cb), lambda i, j: (i, j)), out_shape=SDS((S, D), BF16),
        compiler_params=_cp("parallel", "parallel"))(u3, u3, u3, u3, u3, cw)


def _conv_mid_bwd(u3, da, cw, name):
    S, D3 = u3.shape
    D = D3 // 3
    cb = _tile(D, 512, 128)
    nj = D // cb
    bs = _tile(S, 256, CONV_HALO)
    hb = bs // CONV_HALO
    nblk = S // bs
    last_halo = S // CONV_HALO - 1

    def body(b_ref, c_ref, v_ref, ch_ref, vh_ref, bn_ref, da_ref, dan_ref, w_ref, db_ref, dc_ref, dv_ref, dw_ref):
        i = pl.program_id(0)
        c = c_ref[...].astype(F32)
        v = v_ref[...].astype(F32)
        b = b_ref[...].astype(F32)
        da_ = da_ref[...].astype(F32)
        z = c * v
        zh = jnp.where(i == 0, 0.0, ch_ref[...].astype(F32) * vh_ref[...].astype(F32))
        ze = jnp.concatenate([zh, z], axis=0)
        z1 = _band_apply(_band(bs, bs + CONV_HALO, CONV_HALO - 1, CONV_HALO), ze)
        z2 = _band_apply(_band(bs, bs + CONV_HALO, CONV_HALO - 2, CONV_HALO - 1), ze)
        w = w_ref[...]
        zc = w[2:3] * z + w[1:2] * z1 + w[0:1] * z2
        db_ref[...] = (da_ * zc).astype(BF16)
        dzc = da_ * b
        dzn = jnp.where(i == nblk - 1, 0.0, dan_ref[...].astype(F32) * bn_ref[...].astype(F32))
        dze = jnp.concatenate([dzc, dzn], axis=0)
        dz = w[2:3] * dzc
        dz = dz + w[1:2] * _band_apply(_band(bs, bs + CONV_HALO, 1, 2), dze)
        dz = dz + w[0:1] * _band_apply(_band(bs, bs + CONV_HALO, 2, 3), dze)
        dc_ref[...] = (dz * v).astype(BF16)
        dv_ref[...] = (dz * c).astype(BF16)
        dw_ref[...] = jnp.zeros_like(dw_ref)
        dw_ref[0:1, :] = _colsum(dzc * z2)
        dw_ref[1:2, :] = _colsum(dzc * z1)
        dw_ref[2:3, :] = _colsum(dzc * z)

    def blk(off):
        return pl.BlockSpec((bs, cb), lambda i, j: (i, off + j))

    def halo(off):
        return pl.BlockSpec((CONV_HALO, cb), lambda i, j: (jnp.maximum(i * hb - 1, 0), off + j))

    def nxt(off):
        return pl.BlockSpec((CONV_HALO, cb), lambda i, j: (jnp.minimum((i + 1) * hb, last_halo), off + j))

    o_spec = pl.BlockSpec((bs, cb), lambda i, j: (i, j))
    return pl.pallas_call(
        body, name=name, grid=(nblk, nj),
        in_specs=[blk(0), blk(nj), blk(2 * nj), halo(nj), halo(2 * nj), nxt(0), o_spec, nxt(0),
                  pl.BlockSpec((3, cb), lambda i, j: (0, j))],
        out_specs=[o_spec, o_spec, o_spec, pl.BlockSpec((None, 8, cb), lambda i, j: (i, 0, j))],
        out_shape=[SDS((S, D), BF16)] * 3 + [SDS((nblk, 8, D), F32)],
        compiler_params=_cp("parallel", "parallel"))(u3, u3, u3, u3, u3, u3, da, da, cw)


def _sum_lead(x, name):
    n, r, C = x.shape

    def body(x_ref, o_ref):
        @pl.when(pl.program_id(0) == 0)
        def _():
            o_ref[...] = x_ref[...]

        @pl.when(pl.program_id(0) > 0)
        def _():
            o_ref[...] += x_ref[...]

    return pl.pallas_call(
        body, name=name, grid=(n,), in_specs=[pl.BlockSpec((None, r, C), lambda i: (i, 0, 0))],
        out_specs=pl.BlockSpec((r, C), lambda i: (0, 0)), out_shape=SDS((r, C), F32),
        compiler_params=_cp("arbitrary"))(x)


def _log_sigmoids(z):
    lb = jnp.minimum(z, 0.0) - jnp.log(1.0 + jnp.exp(-jnp.abs(z)))
    return lb, lb - z


def _attn_blocks(S):
    bk = _tile(S, 256, 128)
    bq = 2 * bk if S % (2 * bk) == 0 else bk
    return bq, bk


def _tri(n, pred):
    rowi = lax.broadcasted_iota(jnp.int32, (n, n), 0)
    coli = lax.broadcasted_iota(jnp.int32, (n, n), 1)
    return jnp.where(pred(rowi, coli), 1.0, 0.0).astype(BF16)


def _causal_mask(bq, bk, m):
    rowi = lax.broadcasted_iota(jnp.int32, (bq, bk), 0)
    coli = lax.broadcasted_iota(jnp.int32, (bq, bk), 1)
    return m * bk + coli < rowi


def _sb_attention(qkv, name):
    S, D3 = qkv.shape
    D = D3 // 3
    H = D // HEAD_DIM
    bq, bk = _attn_blocks(S)
    nq, r = S // bq, bq // bk
    scale = HEAD_DIM ** -0.5

    def body(q_ref, k_ref, v_ref, o_ref, lt_ref):
        i = pl.program_id(1)
        q = q_ref[...]
        after = _tri(bk, lambda j, s: j > s)

        def block(kb, carry, acc, causal):
            rows = pl.ds(pl.multiple_of(kb * bk, bk), bk)
            z = _dot_nt(q, k_ref[rows, :]) * scale
            lb, l1 = _log_sigmoids(z)
            if causal is not None:
                l1 = jnp.where(causal, l1, 0.0)
            a = jnp.exp(lb + (_dot(l1.astype(BF16), after) + carry))
            if causal is not None:
                a = jnp.where(causal, a, 0.0)
            acc = acc + _dot(a.astype(BF16), v_ref[rows, :])
            return carry + jnp.sum(l1, axis=1, keepdims=True), acc

        carry, acc = jnp.zeros((bq, 1), F32), jnp.zeros((bq, HEAD_DIM), F32)
        for m in reversed(range(r)):
            carry, acc = block(i * r + m, carry, acc, _causal_mask(bq, bk, m))

        def step(j, ca):
            return block(i * r - 1 - j, ca[0], ca[1], None)

        carry, acc = lax.fori_loop(0, i * r, step, (carry, acc))
        o_ref[...] = acc.astype(BF16)
        lt_ref[...] = jnp.broadcast_to(carry, (bq, HEAD_DIM))

    head_rows = lambda off: pl.BlockSpec((S, HEAD_DIM), lambda hd, i: (0, off + hd))
    blk = pl.BlockSpec((bq, HEAD_DIM), lambda hd, i: (i, hd))
    return pl.pallas_call(
        body, name=name, grid=(H, nq), in_specs=[blk, head_rows(H), head_rows(2 * H)],
        out_specs=[blk, blk], out_shape=[SDS((S, D), BF16), SDS((S, D), F32)],
        compiler_params=_cp("parallel", "arbitrary"))(qkv, qkv, qkv)


def _sb_attention_bwd(qkv, ltot, do, name):
    S, D3 = qkv.shape
    D = D3 // 3
    H = D // HEAD_DIM
    bq, bk = _attn_blocks(S)
    nq, r = S // bq, bq // bk
    scale = HEAD_DIM ** -0.5

    def body(q_ref, k_ref, v_ref, lt_ref, do_ref, dq_ref, dk_ref, dv_ref, dk_acc, dv_acc):
        i = pl.program_id(1)
        q = q_ref[...]
        do_ = do_ref[...]
        lt = lt_ref[:, 0:1]
        after = _tri(bk, lambda j, s: j > s)
        before = _tri(bk, lambda j, s: j < s)

        @pl.when(i == 0)
        def _():
            dk_acc[...] = jnp.zeros_like(dk_acc)
            dv_acc[...] = jnp.zeros_like(dv_acc)

        def block(kb, c1, ce, dq, causal):
            rows = pl.ds(pl.multiple_of(kb * bk, bk), bk)
            k = k_ref[rows, :]
            v = v_ref[rows, :]
            z = _dot_nt(q, k) * scale
            lb, l1 = _log_sigmoids(z)
            sig = jnp.exp(lb)
            if causal is not None:
                l1 = jnp.where(causal, l1, 0.0)
            c1 = c1 + jnp.sum(l1, axis=1, keepdims=True)
            a = jnp.exp(lb + (_dot(l1.astype(BF16), after) + (lt - c1)))
            if causal is not None:
                a = jnp.where(causal, a, 0.0)
            e = a * _dot_nt(do_, v)
            p = _dot(e.astype(BF16), before) + ce
            dz = e * (1.0 - sig) - p * sig
            if causal is not None:
                dz = jnp.where(causal, dz, 0.0)
            dzs = (dz * scale).astype(BF16)
            dk_acc[rows, :] += _dot_tn(dzs, q)
            dv_acc[rows, :] += _dot_tn(a.astype(BF16), do_)
            dq = dq + _dot(dzs, k)
            return c1, ce + jnp.sum(e, axis=1, keepdims=True), dq

        def step(kb, st):
            return block(kb, st[0], st[1], st[2], None)

        zero = jnp.zeros((bq, 1), F32)
        c1, ce, dq = lax.fori_loop(0, i * r, step, (zero, zero, jnp.zeros((bq, HEAD_DIM), F32)))
        for m in range(r):
            c1, ce, dq = block(i * r + m, c1, ce, dq, _causal_mask(bq, bk, m))
        dq_ref[...] = dq.astype(BF16)

        @pl.when(i == nq - 1)
        def _():
            dk_ref[...] = dk_acc[...].astype(BF16)
            dv_ref[...] = dv_acc[...].astype(BF16)

    head_rows = lambda off: pl.BlockSpec((S, HEAD_DIM), lambda hd, i: (0, off + hd))
    blk = pl.BlockSpec((bq, HEAD_DIM), lambda hd, i: (i, hd))
    return pl.pallas_call(
        body, name=name, grid=(H, nq), in_specs=[blk, head_rows(H), head_rows(2 * H), blk, blk],
        out_specs=[blk, head_rows(0), head_rows(0)], out_shape=[SDS((S, D), BF16)] * 3,
        scratch_shapes=[pltpu.VMEM((S, HEAD_DIM), F32), pltpu.VMEM((S, HEAD_DIM), F32)],
        compiler_params=_cp("parallel", "arbitrary"))(qkv, qkv, qkv, ltot, do)


def kernel(x, c, norm_mix_g, norm_ffn_g, w_mod, b_mod, pool_w, pool_scale, conv_w_in, conv_w, conv_w_out, sb_w_qkv, sb_w_o, ffn_w_gate, ffn_w_up, ffn_w_down, final_g, loss_target, m_norm_mix_g, m_norm_ffn_g, m_w_mod, m_b_mod, m_pool_w, m_pool_scale, m_conv_w_in, m_conv_w, m_conv_w_out, m_sb_w_qkv, m_sb_w_o, m_ffn_w_gate, m_ffn_w_up, m_ffn_w_down, m_final_g, v_norm_mix_g, v_norm_ffn_g, v_w_mod, v_b_mod, v_pool_w, v_pool_scale, v_conv_w_in, v_conv_w, v_conv_w_out, v_sb_w_qkv, v_sb_w_o, v_ffn_w_gate, v_ffn_w_up, v_ffn_w_down, v_final_g):
    S, D = x.shape[1], x.shape[2]
    L = norm_mix_g.shape[0]
    nmod = w_mod.shape[2]
    nf = ffn_w_gate.shape[2]
    n3 = conv_w_in.shape[2]
    nd = conv_w_out.shape[1]
    cb = n3 // 3
    ng = pool_w.shape[1]
    pg = pool_w.shape[3]
    n_pool = pool_w.shape[0]
    assert D % HEAD_DIM == 0 and S % 256 == 0 and nd == cb and N_CHIPS * nd == D and pg * ng == D

    mx, my, mc = lax.axis_index("x"), lax.axis_index("y"), lax.axis_index("c")
    chip = 2 * mx + my
    dev = 2 * chip + mc
    hx, ht = x[0], loss_target[0]

    c_all = _allgather8(jnp.broadcast_to(c, (8, D)), "gather_c").reshape(N_DEV, 8, D)[:, 0]
    c_rows = jnp.concatenate([c_all, jnp.zeros((8, D), F32)], axis=0)
    b_cols = lax.dynamic_slice_in_dim(b_mod, chip * nmod, nmod, axis=1).reshape(L, 1, nmod)
    mod_cols = _mod_fwd(c_rows, w_mod, b_cols, "mod_fwd")
    mod_all = _allgather8(mod_cols.reshape(L * 16, nmod), "gather_mod").reshape(N_CHIPS, 2, L, 16, nmod)
    mod = lax.dynamic_index_in_dim(mod_all[:, 0], dev, axis=2, keepdims=False)
    mod = jnp.transpose(mod, (1, 0, 2)).reshape(L, N_MOD, 1, D)

    bf = lambda w: w.astype(BF16)
    n_conv, n_sb = conv_w_in.shape[0], sb_w_qkv.shape[0]
    w_gate, w_up, w_down = [], [], []
    for l in range(L):
        g4, u4, d4 = _gather_shards([bf(ffn_w_gate[l]), bf(ffn_w_up[l]), bf(ffn_w_down[l])], f"gather_ffn{l}")
        w_gate.append(g4)
        w_up.append(u4)
        w_down.append(d4.reshape(N_CHIPS * nf, D))
    w_in, w_out, w_qkv, w_o, w_pool = [], [], [], [], []
    for j in range(n_conv):
        i4, o4 = _gather_shards([bf(conv_w_in[j]), bf(conv_w_out[j])], f"gather_conv{j}")
        w_in.append(i4)
        w_out.append(o4.reshape(D, D))
    for j in range(n_sb):
        i4, o4 = _gather_shards([bf(sb_w_qkv[j]), bf(sb_w_o[j])], f"gather_sb{j}")
        w_qkv.append(i4)
        w_o.append(o4.reshape(D, D))
    for j in range(n_pool):
        (p4,) = _gather_shards([bf(pool_w[j]).reshape(ng * (pg // N_CHIPS), pg)], f"gather_pool{j}")
        w_pool.append(jnp.transpose(p4.reshape(N_CHIPS, ng, pg // N_CHIPS, pg), (1, 0, 2, 3)).reshape(ng, pg, pg))
    taps_cols = jnp.concatenate([pool_scale, conv_w.reshape(-1, nd)], axis=0)
    n_small = taps_cols.shape[0]
    small_rows = jnp.concatenate([taps_cols, jnp.zeros((16 - n_small, nd), F32)], axis=0)
    small_all = _allgather8(small_rows, "gather_small").reshape(N_CHIPS, 2, 16, nd)[:, 0]
    small_full = jnp.transpose(small_all, (1, 0, 2)).reshape(16, D)
    pool_scale_full = small_full[:n_pool]
    conv_taps_full = small_full[n_pool:n_small].reshape(n_conv, 3, D)

    saved = []
    h = hx
    for l in range(L):
        kind, j = l % 3, l // 3
        sh_m, sc_m, gt_m, sh_f, sc_f, gt_f = (mod[l, r] for r in range(N_MOD))
        gm = norm_mix_g[l].reshape(1, D)
        gf = norm_ffn_g[l].reshape(1, D)
        s = {"h_in": h}
        if kind == 0:
            s["diff"] = _pool_pre(h, gm, sc_m, sh_m, f"pool_pre{l}")
            s["cvec_m"] = gt_m * pool_scale_full[j].reshape(1, D)
            h, s["y_m"] = _mm_out_res(s["diff"], w_pool[j], h, s["cvec_m"], f"pool_mm{l}", groups=True)
        elif kind == 1:
            s["u"] = _norm_mod(h, gm, sc_m, sh_m, f"norm_mix{l}")
            s["u3"] = _mm_in(s["u"], w_in[j], n3, f"conv_in{l}")
            s["a_m"] = _conv_mid(s["u3"], conv_taps_full[j], f"conv_mid{l}")
            s["cvec_m"] = gt_m
            h, s["y_m"] = _mm_out_res(s["a_m"], w_out[j], h, gt_m, f"conv_out{l}")
        else:
            s["u"] = _norm_mod(h, gm, sc_m, sh_m, f"norm_mix{l}")
            s["qkv"] = _mm_in(s["u"], w_qkv[j], n3, f"sb_qkv{l}")
            s["o"], s["ltot"] = _sb_attention(s["qkv"], f"sb_attn{l}")
            s["cvec_m"] = gt_m
            h, s["y_m"] = _mm_out_res(s["o"], w_o[j], h, gt_m, f"sb_out{l}")
        s["h_mid"] = h
        s["u2"] = _norm_mod(h, gf, sc_f, sh_f, f"norm_ffn{l}")
        s["gate"] = _mm_in(s["u2"], w_gate[l], nf, f"ffn_gate{l}")
        s["up"], s["a_f"] = _mm_in(s["u2"], w_up[l], nf, f"ffn_up{l}", gate=s["gate"])
        h, s["y_f"] = _mm_out_res(s["a_f"], w_down[l], h, gt_f, f"ffn_down{l}")
        saved.append(s)

    gt_f_last = mod[L - 1, 5]
    dh, dy, st = _loss_head(h, final_g.reshape(1, D), ht, saved[-1]["y_f"], gt_f_last, "loss_head")
    loss = lax.psum(_sum_all(st[2:3], "loss_sum")[0, 0], ("x", "y", "c"))
    d_final_g = st[0:1]
    p_gate_f = st[1:2]
    d_norm_mix, d_norm_ffn = [None] * L, [None] * L
    d_mod = [[None] * N_MOD for _ in range(L)]
    d_pool_scale, d_taps = [None] * n_pool, [None] * n_conv
    big = {}
    for l in reversed(range(L)):
        kind, j = l % 3, l // 3
        s = saved[l]
        sh_m, sc_m, gt_m, sh_f, sc_f, gt_f = (mod[l, r] for r in range(N_MOD))
        gm = norm_mix_g[l].reshape(1, D)
        gf = norm_ffn_g[l].reshape(1, D)
        d_mod[l][5] = p_gate_f
        dgate, dup = _mm_nt(dy, w_down[l], f"ffn_down_bwd{l}", swiglu=(s["gate"], s["up"]))
        gw_down = _mm_tn(s["a_f"], dy, f"ffn_down_wgrad{l}").reshape(N_CHIPS, nf, D)
        gw_gate = _mm_tn(s["u2"], dgate, f"ffn_gate_wgrad{l}", shard_cols=nf)
        gw_up = _mm_tn(s["u2"], dup, f"ffn_up_wgrad{l}", shard_cols=nf)
        du2 = _mm_nt_acc(dgate, w_gate[l], f"ffn_gate_bwd{l}")
        du2 = _mm_nt_acc(dup, w_up[l], f"ffn_up_bwd{l}", add=du2)
        big[("ffn", l)] = _reduce_to_owner([gw_gate, gw_up, gw_down], f"reduce_ffn{l}")
        dh, dy, st = _norm_bwd(s["h_mid"], gf, sc_f, du2, dh, f"norm_ffn_bwd{l}", prev=(s["y_m"], s["cvec_m"]))
        d_mod[l][3], d_mod[l][4], d_norm_ffn[l] = st[0:1], st[1:2], st[2:3]
        p_mix = st[3:4]
        if kind == 0:
            d_mod[l][2] = p_mix * pool_scale_full[j].reshape(1, D)
            d_pool_scale[j] = p_mix * gt_m
            dd = _mm_nt(dy, w_pool[j], f"pool_mm_bwd{l}", groups=True)
            gw_pool = _mm_tn(s["diff"], dy, f"pool_wgrad{l}", groups=ng)
            big[("pool", j)] = gw_pool
            du = _pool_post(dd, f"pool_post{l}")
        elif kind == 1:
            d_mod[l][2] = p_mix
            da = _mm_nt(dy, w_out[j], f"conv_out_bwd{l}")
            gw_out = _mm_tn(s["a_m"], dy, f"conv_out_wgrad{l}").reshape(N_CHIPS, nd, D)
            db, dc, dv, dtap = _conv_mid_bwd(s["u3"], da, conv_taps_full[j], f"conv_mid_bwd{l}")
            d_taps[j] = _sum_lead(dtap, f"conv_tap_sum{l}")[0:3]
            du3 = jnp.concatenate([db, dc, dv], axis=1)
            gw_in = _mm_tn(s["u"], du3, f"conv_in_wgrad{l}", shard_cols=n3)
            du = _mm_nt_acc(du3, w_in[j], f"conv_in_bwd{l}")
            big[("conv", j)] = _reduce_to_owner([gw_in, gw_out], f"reduce_conv{l}")
        else:
            d_mod[l][2] = p_mix
            do = _mm_nt(dy, w_o[j], f"sb_out_bwd{l}")
            gw_o = _mm_tn(s["o"], dy, f"sb_out_wgrad{l}").reshape(N_CHIPS, nd, D)
            dq, dk, dv = _sb_attention_bwd(s["qkv"], s["ltot"], do, f"sb_attn_bwd{l}")
            dqkv = jnp.concatenate([dq, dk, dv], axis=1)
            gw_qkv = _mm_tn(s["u"], dqkv, f"sb_qkv_wgrad{l}", shard_cols=n3)
            du = _mm_nt_acc(dqkv, w_qkv[j], f"sb_qkv_bwd{l}")
            big[("sb", j)] = _reduce_to_owner([gw_qkv, gw_o], f"reduce_sb{l}")
        if l > 0:
            prev = (saved[l - 1]["y_f"], mod[l - 1, 5])
            dh, dy, st = _norm_bwd(s["h_in"], gm, sc_m, du, dh, f"norm_mix_bwd{l}", prev=prev)
            p_gate_f = st[3:4]
        else:
            dh, st = _norm_bwd(s["h_in"], gm, sc_m, du, dh, f"norm_mix_bwd{l}")
        d_mod[l][0], d_mod[l][1], d_norm_mix[l] = st[0:1], st[1:2], st[2:3]
    grad_x = dh.reshape(1, S, D)

    gw_pool = jnp.stack([big[("pool", j)] for j in range(n_pool)])
    gw_pool = jnp.transpose(gw_pool.reshape(n_pool, ng, N_CHIPS, pg // N_CHIPS, pg), (2, 0, 1, 3, 4))
    (g_pool,) = _reduce_to_owner([gw_pool.reshape(N_CHIPS, n_pool * ng * (pg // N_CHIPS), pg)], "reduce_pool")

    rows = [d_final_g] + d_norm_mix + d_norm_ffn + [r for l in range(L) for r in d_mod[l]] + d_pool_scale
    rows += [d_taps[j] for j in range(n_conv)]
    vec = jnp.concatenate(rows, axis=0)
    n_rows = vec.shape[0]
    pad = -n_rows % 8
    vec = jnp.concatenate([vec, jnp.zeros((pad, D), F32)], axis=0) if pad else vec
    vec_all = _allgather8(vec, "gather_small_grads").reshape(N_DEV, n_rows + pad, D)
    tot = _sum_devices(vec_all, "sum_small_grads")
    r0 = 1 + 2 * L
    g_final = tot[0]
    g_norm_mix = tot[1:1 + L]
    g_norm_ffn = tot[1 + L:r0]
    g_b_mod = tot[r0:r0 + N_MOD * L].reshape(L, N_MOD * D)
    r1 = r0 + N_MOD * L
    g_pool_scale = lax.dynamic_slice_in_dim(tot[r1:r1 + n_pool], chip * nd, nd, axis=1)
    g_taps = lax.dynamic_slice_in_dim(tot[r1 + n_pool:r1 + n_pool + 3 * n_conv], chip * nd, nd, axis=1)
    g_conv_w = g_taps.reshape(conv_w.shape)
    dmod_all = vec_all[:, r0:r1].reshape(N_DEV, L, N_MOD * D)
    dmod_cols = jnp.transpose(lax.dynamic_slice_in_dim(dmod_all, chip * nmod, nmod, axis=2), (1, 0, 2))
    g_w_mod = _mod_wgrad(jnp.transpose(c_all), dmod_cols, "mod_wgrad")

    g_ffn_gate = jnp.stack([big[("ffn", l)][0] for l in range(L)])
    g_ffn_up = jnp.stack([big[("ffn", l)][1] for l in range(L)])
    g_ffn_down = jnp.stack([big[("ffn", l)][2] for l in range(L)])
    g_conv_in = jnp.stack([big[("conv", j)][0] for j in range(n_conv)])
    g_conv_out = jnp.stack([big[("conv", j)][1] for j in range(n_conv)])
    g_sb_qkv = jnp.stack([big[("sb", j)][0] for j in range(n_sb)])
    g_sb_o = jnp.stack([big[("sb", j)][1] for j in range(n_sb)])
    g_pool_w = g_pool.reshape(pool_w.shape)

    grads = [g_norm_mix, g_norm_ffn, g_w_mod, g_b_mod, g_pool_w, g_pool_scale, g_conv_in, g_conv_w, g_conv_out,
             g_sb_qkv, g_sb_o, g_ffn_gate, g_ffn_up, g_ffn_down, g_final]
    weights = [norm_mix_g, norm_ffn_g, w_mod, b_mod, pool_w, pool_scale, conv_w_in, conv_w, conv_w_out,
               sb_w_qkv, sb_w_o, ffn_w_gate, ffn_w_up, ffn_w_down, final_g]
    ms = [m_norm_mix_g, m_norm_ffn_g, m_w_mod, m_b_mod, m_pool_w, m_pool_scale, m_conv_w_in, m_conv_w, m_conv_w_out,
          m_sb_w_qkv, m_sb_w_o, m_ffn_w_gate, m_ffn_w_up, m_ffn_w_down, m_final_g]
    vs = [v_norm_mix_g, v_norm_ffn_g, v_w_mod, v_b_mod, v_pool_w, v_pool_scale, v_conv_w_in, v_conv_w, v_conv_w_out,
          v_sb_w_qkv, v_sb_w_o, v_ffn_w_gate, v_ffn_w_up, v_ffn_w_down, v_final_g]
    deltas, new_ms, new_vs = [], [], []
    for n, (w, g, m, v) in enumerate(zip(weights, grads, ms, vs)):
        if w.ndim == 1:
            w, g, m, v = (a.reshape(1, -1) for a in (w, g, m, v))
        g = g.reshape(w.shape)
        grads[n] = g.reshape(weights[n].shape)
        d, nm, nv = _adamw(w, g, m, v, f"adamw{n}")
        deltas.append(d.reshape(weights[n].shape))
        new_ms.append(nm.reshape(weights[n].shape))
        new_vs.append(nv.reshape(weights[n].shape))
    return (loss, grad_x, *grads, *deltas, *new_ms, *new_vs)
```

```python
import functools

import jax
import jax.numpy as jnp
from jax import lax
from jax.experimental import pallas as pl
from jax.experimental.pallas import tpu as pltpu

F32 = jnp.float32
BF16 = jnp.bfloat16
SDS = jax.ShapeDtypeStruct
MESH = pl.DeviceIdType.MESH

RMS_EPS = 1e-6
POOL_WINDOWS = (2, 4, 8, 16)
POOL_HALO = 16
CONV_HALO = 16
HEAD_DIM = 128
N_MOD = 6
N_CHIPS = 4
N_DEV = 8
ADAM_LR = 0.001
ADAM_B1 = 0.9
ADAM_B2 = 0.999
ADAM_EPS = 1e-08
ADAM_WD = 0.01
ADAM_STEP = 10
VMEM_LIMIT_V7X = 52 * 1024 * 1024
ANY = pl.BlockSpec(memory_space=pl.ANY)
VMEM_WHOLE = pl.BlockSpec(memory_space=pltpu.VMEM)


def _cp(*sem):
    return pltpu.CompilerParams(dimension_semantics=sem, vmem_limit_bytes=VMEM_LIMIT_V7X)


def _tile(n, pref, unit):
    if n <= pref:
        return n
    t = (pref // unit) * unit
    while t >= unit:
        if n % t == 0:
            return t
        t -= unit
    return n


def _dot(a, b):
    return jnp.dot(a, b, preferred_element_type=F32)


def _dot_nt(a, b):
    return lax.dot_general(a, b, (((1,), (1,)), ((), ())), preferred_element_type=F32)


def _dot_tn(a, b):
    return lax.dot_general(a, b, (((0,), (0,)), ((), ())), preferred_element_type=F32)


def _split_bf16(x):
    hi = x.astype(BF16)
    lo = (x - hi.astype(F32)).astype(BF16)
    return hi, lo


def _sigmoid(x):
    return 1.0 / (1.0 + jnp.exp(-x))


def _my_place():
    return lax.axis_index("x"), lax.axis_index("y"), lax.axis_index("c")


def _allgather8(blk, name):
    m, n = blk.shape

    def body(x_ref, out_ref, send_sems, recv_sems, local_sem):
        x, y, c = _my_place()
        me, sibling = (x, y, c), (x, y, 1 - c)
        chips = [(1 - x, y), (x, 1 - y), (1 - x, 1 - y)]

        def rows(px, py, pc):
            return out_ref.at[pl.ds((4 * px + 2 * py + pc) * m, m), :]

        def copy(k, block, to, src=None):
            return pltpu.make_async_remote_copy(
                src_ref=rows(*block) if src is None else src, dst_ref=rows(*block),
                send_sem=send_sems.at[k], recv_sem=recv_sems.at[k], device_id=to, device_id_type=MESH)

        mine = pltpu.make_async_copy(x_ref, rows(*me), local_sem)
        mine.start()
        first = [copy(0, me, sibling, src=x_ref)]
        first += [copy(1 + j, me, (*chip, c), src=x_ref) for j, chip in enumerate(chips)]
        for cp in first:
            cp.start()
        passed = [copy(4 + j, (*chip, c), sibling) for j, chip in enumerate(chips)]
        for j, chip in enumerate(chips):
            copy(1 + j, (*chip, c), me).wait_recv()
            passed[j].start()
        copy(0, sibling, me).wait_recv()
        for j, chip in enumerate(chips):
            copy(4 + j, (*chip, 1 - c), me).wait_recv()
        for cp in first + passed:
            cp.wait_send()
        mine.wait()

    return pl.pallas_call(
        body, name=name, out_shape=SDS((N_DEV * m, n), blk.dtype),
        in_specs=[VMEM_WHOLE], out_specs=VMEM_WHOLE,
        scratch_shapes=[pltpu.SemaphoreType.DMA((7,)), pltpu.SemaphoreType.DMA((7,)), pltpu.SemaphoreType.DMA],
    )(blk)


def _gather_shards(ws, name):
    nt = len(ws)

    def body(*refs):
        w_refs, out_refs = refs[:nt], refs[nt:2 * nt]
        send_sems, recv_sems, local_sems = refs[2 * nt:]
        x, y, c = _my_place()
        sibling = (x, y, 1 - c)
        chips = [(1 - x, y), (x, 1 - y), (1 - x, 1 - y)]
        k_me = 2 * x + y
        copies = []
        for t in range(nt):
            half = ws[t].shape[0] // 2
            w_ref, out_ref = w_refs[t], out_refs[t]

            def dst(k, hc, out_ref=out_ref, half=half):
                return out_ref.at[k, pl.ds(hc * half, half), :]

            def copy(s, src, to_dst, to, t=t):
                return pltpu.make_async_remote_copy(
                    src_ref=src, dst_ref=to_dst, send_sem=send_sems.at[6 * t + s], recv_sem=recv_sems.at[6 * t + s],
                    device_id=to, device_id_type=MESH)

            mine = pltpu.make_async_copy(w_ref, out_ref.at[k_me], local_sems.at[t])
            mine.start()
            first = [copy(j, w_ref.at[pl.ds(c * half, half), :], dst(k_me, c), (*chip, c))
                     for j, chip in enumerate(chips)]
            for cp in first:
                cp.start()
            copies.append((mine, first, dst, copy))
        sends = []
        for t in range(nt):
            mine, first, dst, copy = copies[t]
            passed = []
            for j, (px, py) in enumerate(chips):
                landed = dst(2 * px + py, c)
                copy(j, landed, landed, (px, py, c)).wait_recv()
                fwd = copy(3 + j, landed, landed, sibling)
                fwd.start()
                passed.append(fwd)
            sends.append((mine, first + passed))
        for t in range(nt):
            _, _, dst, copy = copies[t]
            for j, (px, py) in enumerate(chips):
                other = dst(2 * px + py, 1 - c)
                copy(3 + j, other, other, sibling).wait_recv()
        for mine, cps in sends:
            for cp in cps:
                cp.wait_send()
            mine.wait()

    return pl.pallas_call(
        body, name=name,
        out_shape=[SDS((N_CHIPS,) + w.shape, w.dtype) for w in ws],
        in_specs=[ANY] * nt, out_specs=[ANY] * nt,
        scratch_shapes=[pltpu.SemaphoreType.DMA((6 * nt,)), pltpu.SemaphoreType.DMA((6 * nt,)),
                        pltpu.SemaphoreType.DMA((nt,))],
    )(*ws)


def _swap_sibling_halves(gs, name):
    nt = len(gs)

    def body(*refs):
        g_refs, out_refs = refs[:nt], refs[nt:2 * nt]
        send_sems, recv_sems = refs[2 * nt:]
        x, y, c = _my_place()
        cps = []
        for t in range(nt):
            half = gs[t].shape[1] // 2
            cp = pltpu.make_async_remote_copy(
                src_ref=g_refs[t].at[:, pl.ds((1 - c) * half, half), :], dst_ref=out_refs[t],
                send_sem=send_sems.at[t], recv_sem=recv_sems.at[t], device_id=(x, y, 1 - c), device_id_type=MESH)
            cp.start()
            cps.append(cp)
        for cp in cps:
            cp.wait()

    return pl.pallas_call(
        body, name=name,
        out_shape=[SDS((g.shape[0], g.shape[1] // 2, g.shape[2]), g.dtype) for g in gs],
        in_specs=[ANY] * nt, out_specs=[ANY] * nt,
        scratch_shapes=[pltpu.SemaphoreType.DMA((nt,)), pltpu.SemaphoreType.DMA((nt,))],
    )(*gs)


def _scatter_to_chips(ps, name):
    nt = len(ps)

    def body(*refs):
        p_refs, out_refs = refs[:nt], refs[nt:2 * nt]
        send_sems, recv_sems = refs[2 * nt:]
        x, y, c = _my_place()
        chips = [(1 - x, y), (x, 1 - y), (1 - x, 1 - y)]
        cps = []
        for t in range(nt):
            for j, (px, py) in enumerate(chips):
                cp = pltpu.make_async_remote_copy(
                    src_ref=p_refs[t].at[2 * px + py], dst_ref=out_refs[t].at[j],
                    send_sem=send_sems.at[3 * t + j], recv_sem=recv_sems.at[3 * t + j], device_id=(px, py, c),
                    device_id_type=MESH)
                cp.start()
                cps.append(cp)
        for cp in cps:
            cp.wait()

    return pl.pallas_call(
        body, name=name,
        out_shape=[SDS((3,) + p.shape[1:], p.dtype) for p in ps],
        in_specs=[ANY] * nt, out_specs=[ANY] * nt,
        scratch_shapes=[pltpu.SemaphoreType.DMA((3 * nt,)), pltpu.SemaphoreType.DMA((3 * nt,))],
    )(*ps)


def _join_sibling_halves(fs, name):
    nt = len(fs)

    def body(*refs):
        out_refs = refs[nt:2 * nt]
        send_sems, recv_sems = refs[2 * nt:]
        x, y, c = _my_place()
        cps = []
        for t in range(nt):
            r = fs[t].shape[0] // 2
            mine = out_refs[t].at[pl.ds(c * r, r), :]
            cp = pltpu.make_async_remote_copy(
                src_ref=mine, dst_ref=mine, send_sem=send_sems.at[t], recv_sem=recv_sems.at[t],
                device_id=(x, y, 1 - c), device_id_type=MESH)
            cp.start()
            cps.append((cp, r))
        for t, (cp, r) in enumerate(cps):
            cp.wait_send()
            other = out_refs[t].at[pl.ds((1 - c) * r, r), :]
            pltpu.make_async_remote_copy(
                src_ref=other, dst_ref=other, send_sem=send_sems.at[t], recv_sem=recv_sems.at[t],
                device_id=(x, y, 1 - c), device_id_type=MESH).wait_recv()

    return pl.pallas_call(
        body, name=name, out_shape=[SDS(f.shape, f.dtype) for f in fs],
        in_specs=[ANY] * nt, out_specs=[ANY] * nt, input_output_aliases={t: t for t in range(nt)},
        scratch_shapes=[pltpu.SemaphoreType.DMA((nt,)), pltpu.SemaphoreType.DMA((nt,))],
    )(*fs)


def _add_sibling(g, recv, name):
    _, R, C = g.shape
    half = R // 2
    br = _tile(half, max(16, (1 << 19) // C), 16)
    nrb = half // br

    def body(g_ref, r_ref, bf_ref, own_ref):
        s = g_ref[...] + r_ref[...]
        bf_ref[...] = s.astype(BF16)

        @pl.when(pl.program_id(1) == 2 * lax.axis_index("x") + lax.axis_index("y"))
        def _():
            own_ref[...] = s

    return pl.pallas_call(
        body, name=name, grid=(nrb, N_CHIPS),
        in_specs=[pl.BlockSpec((None, br, C), lambda i, k: (k, lax.axis_index("c") * nrb + i, 0)),
                  pl.BlockSpec((None, br, C), lambda i, k: (k, i, 0))],
        out_specs=[pl.BlockSpec((None, br, C), lambda i, k: (k, i, 0)),
                   pl.BlockSpec((br, C), lambda i, k: (i, 0))],
        out_shape=[SDS((N_CHIPS, half, C), BF16), SDS((half, C), F32)],
        compiler_params=_cp("arbitrary", "arbitrary"),
    )(g, recv)


def _add_chips(own, recv, name):
    r, C = own.shape
    br = _tile(r, max(16, (1 << 19) // C), 16)
    nrb = r // br

    def body(own_ref, r_ref, o_ref):
        s = own_ref[...]
        for j in range(3):
            s = s + r_ref[j].astype(F32)
        o_ref[...] = s

    return pl.pallas_call(
        body, name=name, grid=(nrb,),
        in_specs=[pl.BlockSpec((br, C), lambda i: (i, 0)), pl.BlockSpec((3, br, C), lambda i: (0, i, 0))],
        out_specs=pl.BlockSpec((br, C), lambda i: (lax.axis_index("c") * nrb + i, 0)),
        out_shape=SDS((2 * r, C), F32), compiler_params=_cp("arbitrary"),
    )(own, recv)


def _reduce_to_owner(gs, name):
    recv1 = _swap_sibling_halves(gs, name + "_swap")
    parts = [_add_sibling(g, r, name + "_add1") for g, r in zip(gs, recv1)]
    recv2 = _scatter_to_chips([p[0] for p in parts], name + "_scatter")
    fins = [_add_chips(p[1], r, name + "_add2") for p, r in zip(parts, recv2)]
    return _join_sibling_halves(fins, name + "_join")


def _sum_devices(allv, name):
    _, r, n = allv.shape

    def body(a_ref, o_ref):
        s = a_ref[0]
        for d in range(1, N_DEV):
            s = s + a_ref[d]
        o_ref[...] = s

    return pl.pallas_call(body, name=name, out_shape=SDS((r, n), F32), in_specs=[VMEM_WHOLE],
                          out_specs=VMEM_WHOLE)(allv)


def _adamw(w, g, m, v, name):
    shape = w.shape
    C = shape[-1]
    R = w.size // C
    args = [a.reshape(R, C) for a in (w, g, m, v)]
    br = _tile(R, max(8, (1 << 18) // C), 8)

    def body(w_ref, g_ref, m_ref, v_ref, d_ref, nm_ref, nv_ref):
        g_ = g_ref[...]
        m_ = ADAM_B1 * m_ref[...] + (1.0 - ADAM_B1) * g_
        v_ = ADAM_B2 * v_ref[...] + (1.0 - ADAM_B2) * (g_ * g_)
        m_hat = m_ / (1.0 - ADAM_B1 ** ADAM_STEP)
        v_hat = v_ / (1.0 - ADAM_B2 ** ADAM_STEP)
        d_ref[...] = -ADAM_LR * (m_hat / (jnp.sqrt(v_hat) + ADAM_EPS) + ADAM_WD * w_ref[...])
        nm_ref[...] = m_
        nv_ref[...] = v_

    spec = pl.BlockSpec((br, C), lambda i: (i, 0))
    outs = pl.pallas_call(
        body, name=name, grid=(R // br,), in_specs=[spec] * 4, out_specs=[spec] * 3,
        out_shape=[SDS((R, C), F32)] * 3, compiler_params=_cp("parallel"),
    )(*args)
    return [o.reshape(shape) for o in outs]


def _mod_fwd(c_rows, w_mod, b_cols, name):
    L, D, n = w_mod.shape
    bn = _tile(n, 512, 128)

    def body(c_ref, w_ref, b_ref, o_ref):
        cc = c_ref[...]
        sc = (cc * _sigmoid(cc)).astype(BF16)
        o_ref[...] = _dot(sc, w_ref[...].astype(BF16)) + b_ref[...]

    return pl.pallas_call(
        body, name=name, grid=(L, n // bn),
        in_specs=[pl.BlockSpec((16, D), lambda l, j: (0, 0)),
                  pl.BlockSpec((None, D, bn), lambda l, j: (l, 0, j)),
                  pl.BlockSpec((None, 1, bn), lambda l, j: (l, 0, j))],
        out_specs=pl.BlockSpec((None, 16, bn), lambda l, j: (l, 0, j)),
        out_shape=SDS((L, 16, n), F32), compiler_params=_cp("parallel", "parallel"),
    )(c_rows, w_mod, b_cols)


def _mod_wgrad(c_cols, dmod, name):
    D = c_cols.shape[0]
    L, _, n = dmod.shape
    bd = _tile(D, 512, 8)
    bn = _tile(n, 512, 128)

    def body(c_ref, d_ref, o_ref):
        cc = c_ref[...]
        sc = cc * _sigmoid(cc)
        dm = d_ref[...]
        acc = sc[:, 0:1] * dm[0:1, :]
        for b in range(1, N_DEV):
            acc = acc + sc[:, b:b + 1] * dm[b:b + 1, :]
        o_ref[...] = acc

    return pl.pallas_call(
        body, name=name, grid=(L, D // bd, n // bn),
        in_specs=[pl.BlockSpec((bd, N_DEV), lambda l, i, j: (i, 0)),
                  pl.BlockSpec((None, N_DEV, bn), lambda l, i, j: (l, 0, j))],
        out_specs=pl.BlockSpec((None, bd, bn), lambda l, i, j: (l, i, j)),
        out_shape=SDS((L, D, n), F32), compiler_params=_cp("parallel", "parallel", "parallel"),
    )(c_cols, dmod)


def _mm_in(x, w4, bn, name, gate=None):
    M, K = x.shape
    nsh, _, n = w4.shape
    N = nsh * n
    nb = n // bn
    bm = _tile(M, 512, 16)
    x_spec = pl.BlockSpec((bm, K), lambda j, i: (i, 0))
    w_spec = pl.BlockSpec((None, K, bn), lambda j, i: (j // nb, 0, j % nb))
    o_spec = pl.BlockSpec((bm, bn), lambda j, i: (i, j))
    if gate is None:
        def body(x_ref, w_ref, o_ref):
            o_ref[...] = _dot(x_ref[...], w_ref[...]).astype(BF16)

        return pl.pallas_call(
            body, name=name, grid=(N // bn, M // bm), in_specs=[x_spec, w_spec], out_specs=o_spec,
            out_shape=SDS((M, N), BF16), compiler_params=_cp("parallel", "parallel"))(x, w4)

    def body_gated(x_ref, w_ref, g_ref, up_ref, a_ref):
        up = _dot(x_ref[...], w_ref[...])
        g = g_ref[...].astype(F32)
        up_ref[...] = up.astype(BF16)
        a_ref[...] = (g * _sigmoid(g) * up).astype(BF16)

    return pl.pallas_call(
        body_gated, name=name, grid=(N // bn, M // bm), in_specs=[x_spec, w_spec, o_spec],
        out_specs=[o_spec, o_spec], out_shape=[SDS((M, N), BF16)] * 2,
        compiler_params=_cp("parallel", "parallel"))(x, w4, gate)


def _mm_out_res(a, w, h, cvec, name, groups=False):
    M = a.shape[0]
    N = h.shape[1]
    if groups:
        bn = w.shape[2]
        a_spec = pl.BlockSpec((_tile(M, 512, 16), w.shape[1]), lambda j, i: (i, j))
        w_spec = pl.BlockSpec((None, w.shape[1], bn), lambda j, i: (j, 0, 0))
    else:
        bn = _tile(N, 512, 128)
        a_spec = pl.BlockSpec((_tile(M, 512, 16), a.shape[1]), lambda j, i: (i, 0))
        w_spec = pl.BlockSpec((a.shape[1], bn), lambda j, i: (0, j))
    bm = _tile(M, 512, 16)
    o_spec = pl.BlockSpec((bm, bn), lambda j, i: (i, j))

    def body(a_ref, w_ref, h_ref, c_ref, hn_ref, y_ref):
        y = _dot(a_ref[...], w_ref[...])
        hn_ref[...] = h_ref[...] + c_ref[...] * y
        y_ref[...] = y.astype(BF16)

    return pl.pallas_call(
        body, name=name, grid=(N // bn, M // bm),
        in_specs=[a_spec, w_spec, o_spec, pl.BlockSpec((1, bn), lambda j, i: (0, j))],
        out_specs=[o_spec, o_spec], out_shape=[SDS((M, N), F32), SDS((M, N), BF16)],
        compiler_params=_cp("parallel", "parallel"))(a, w, h, cvec)


def _mm_nt(dy, w, name, groups=False, swiglu=None):
    M = dy.shape[0]
    bm = _tile(M, 1024, 16)
    if groups:
        N = dy.shape[1]
        bn = w.shape[1]
        dy_spec = pl.BlockSpec((bm, w.shape[2]), lambda j, i: (i, j))
        w_spec = pl.BlockSpec((None, bn, w.shape[2]), lambda j, i: (j, 0, 0))
    else:
        N = w.shape[0]
        bn = _tile(N, 512, 128)
        dy_spec = pl.BlockSpec((bm, dy.shape[1]), lambda j, i: (i, 0))
        w_spec = pl.BlockSpec((bn, w.shape[1]), lambda j, i: (j, 0))
    o_spec = pl.BlockSpec((bm, bn), lambda j, i: (i, j))
    if swiglu is None:
        def body(dy_ref, w_ref, o_ref):
            o_ref[...] = _dot_nt(dy_ref[...], w_ref[...]).astype(BF16)

        return pl.pallas_call(
            body, name=name, grid=(N // bn, M // bm), in_specs=[dy_spec, w_spec], out_specs=o_spec,
            out_shape=SDS((M, N), BF16), compiler_params=_cp("parallel", "parallel"))(dy, w)

    def body_swiglu(dy_ref, w_ref, g_ref, u_ref, dg_ref, du_ref):
        da = _dot_nt(dy_ref[...], w_ref[...])
        g = g_ref[...].astype(F32)
        sg = _sigmoid(g)
        silu = g * sg
        dg_ref[...] = (da * u_ref[...].astype(F32) * (sg + silu * (1.0 - sg))).astype(BF16)
        du_ref[...] = (da * silu).astype(BF16)

    return pl.pallas_call(
        body_swiglu, name=name, grid=(N // bn, M // bm), in_specs=[dy_spec, w_spec, o_spec, o_spec],
        out_specs=[o_spec, o_spec], out_shape=[SDS((M, N), BF16)] * 2,
        compiler_params=_cp("parallel", "parallel"))(dy, w, *swiglu)


def _mm_nt_acc(dx, w4, name, add=None):
    M = dx.shape[0]
    nc, K, n = w4.shape
    bm = _tile(M, 512, 16)
    with_add = add is not None

    def body(*refs):
        dx_ref, w_ref = refs[:2]
        o_ref, acc_ref = refs[-2:]
        c = pl.program_id(1)
        s = _dot_nt(dx_ref[...], w_ref[...])

        @pl.when(c == 0)
        def _():
            acc_ref[...] = s + refs[2][...].astype(F32) if with_add else s

        @pl.when(c > 0)
        def _():
            acc_ref[...] += s

        @pl.when(c == nc - 1)
        def _():
            o_ref[...] = acc_ref[...].astype(BF16)

    o_spec = pl.BlockSpec((bm, K), lambda i, c: (i, 0))
    in_specs = [pl.BlockSpec((bm, n), lambda i, c: (i, c)), pl.BlockSpec((None, K, n), lambda i, c: (c, 0, 0))]
    args = [dx, w4]
    if with_add:
        in_specs.append(o_spec)
        args.append(add)
    return pl.pallas_call(
        body, name=name, grid=(M // bm, nc), in_specs=in_specs, out_specs=o_spec, out_shape=SDS((M, K), BF16),
        scratch_shapes=[pltpu.VMEM((bm, K), F32)], compiler_params=_cp("parallel", "arbitrary"))(*args)


def _mm_tn(x, dy, name, shard_cols=None, groups=None):
    M, K = x.shape
    N = dy.shape[1]
    bm = _tile(M, 1024, 16)
    if groups is not None:
        kg, ng = K // groups, N // groups
        grid = (groups, 1, M // bm)
        x_spec = pl.BlockSpec((bm, kg), lambda i, j, s: (s, i))
        dy_spec = pl.BlockSpec((bm, ng), lambda i, j, s: (s, i))
        o_spec = pl.BlockSpec((None, kg, ng), lambda i, j, s: (i, 0, 0))
        out_shape = SDS((groups, kg, ng), F32)
    else:
        bko = _tile(K, 1408, 128)
        if shard_cols is not None:
            bn = _tile(shard_cols, 1536, 128)
            nb = shard_cols // bn
            o_spec = pl.BlockSpec((None, bko, bn), lambda i, j, s: (j // nb, i, j % nb))
            out_shape = SDS((N_CHIPS, K, shard_cols), F32)
        else:
            bn = _tile(N, 1024, 128)
            o_spec = pl.BlockSpec((bko, bn), lambda i, j, s: (i, j))
            out_shape = SDS((K, N), F32)
        grid = (K // bko, N // bn, M // bm)
        x_spec = pl.BlockSpec((bm, bko), lambda i, j, s: (s, i))
        dy_spec = pl.BlockSpec((bm, bn), lambda i, j, s: (s, j))

    def body(x_ref, dy_ref, o_ref):
        p = _dot_tn(x_ref[...], dy_ref[...])

        @pl.when(pl.program_id(2) == 0)
        def _():
            o_ref[...] = p

        @pl.when(pl.program_id(2) > 0)
        def _():
            o_ref[...] += p

    return pl.pallas_call(
        body, name=name, grid=grid, in_specs=[x_spec, dy_spec], out_specs=o_spec, out_shape=out_shape,
        compiler_params=_cp("parallel", "parallel", "arbitrary"))(x, dy)


def _norm_mod_rows(h, g, scale, shift):
    r = lax.rsqrt(jnp.mean(h * h, axis=-1, keepdims=True) + RMS_EPS)
    return (h * r) * g * (1.0 + scale) + shift


def _vec_spec(D):
    return pl.BlockSpec((1, D), lambda i: (0, 0))


def _norm_mod(h, g, scale, shift, name):
    S, D = h.shape
    bs = _tile(S, 512, 16)

    def body(h_ref, g_ref, sc_ref, sh_ref, u_ref):
        u_ref[...] = _norm_mod_rows(h_ref[...], g_ref[...], sc_ref[...], sh_ref[...]).astype(BF16)

    row = pl.BlockSpec((bs, D), lambda i: (i, 0))
    return pl.pallas_call(
        body, name=name, grid=(S // bs,), in_specs=[row, _vec_spec(D), _vec_spec(D), _vec_spec(D)],
        out_specs=row, out_shape=SDS((S, D), BF16), compiler_params=_cp("parallel"))(h, g, scale, shift)


def _band(rows, cols, lo, hi):
    d = lax.broadcasted_iota(jnp.int32, (rows, cols), 1) - lax.broadcasted_iota(jnp.int32, (rows, cols), 0)
    return jnp.where((d >= lo) & (d < hi), 1.0, 0.0).astype(BF16)


def _band_apply(band, x):
    hi, lo = _split_bf16(x)
    return _dot(band, hi) + _dot(band, lo)


def _pool_pre(h, g, scale, shift, name):
    S, D = h.shape
    ng = len(POOL_WINDOWS)
    pg = D // ng
    bs = _tile(S, 256, POOL_HALO)
    hb = bs // POOL_HALO

    def body(h_ref, hh_ref, g_ref, sc_ref, sh_ref, o_ref):
        i = pl.program_id(0)
        u = _norm_mod_rows(h_ref[...], g_ref[...], sc_ref[...], sh_ref[...])
        uh = _norm_mod_rows(hh_ref[...], g_ref[...], sc_ref[...], sh_ref[...])
        uh = jnp.where(i == 0, 0.0, uh)
        ue = jnp.concatenate([uh, u], axis=0)
        t = i * bs + lax.broadcasted_iota(jnp.int32, (bs, 1), 0)
        for gi, w in enumerate(POOL_WINDOWS):
            cols = slice(gi * pg, (gi + 1) * pg)
            band = _band(bs, bs + POOL_HALO, POOL_HALO - w + 1, POOL_HALO + 1)
            inv = 1.0 / jnp.minimum(t + 1, w).astype(F32)
            o_ref[:, cols] = (_band_apply(band, ue[:, cols]) * inv - u[:, cols]).astype(BF16)

    row = pl.BlockSpec((bs, D), lambda i: (i, 0))
    halo = pl.BlockSpec((POOL_HALO, D), lambda i: (jnp.maximum(i * hb - 1, 0), 0))
    return pl.pallas_call(
        body, name=name, grid=(S // bs,),
        in_specs=[row, halo, _vec_spec(D), _vec_spec(D), _vec_spec(D)],
        out_specs=row, out_shape=SDS((S, D), BF16), compiler_params=_cp("parallel"))(h, h, g, scale, shift)


def _pool_post(dd, name):
    S, D = dd.shape
    ng = len(POOL_WINDOWS)
    pg = D // ng
    bs = _tile(S, 256, POOL_HALO)
    hb = bs // POOL_HALO
    nblk = S // bs

    def body(d_ref, dn_ref, o_ref):
        i = pl.program_id(0)
        d = d_ref[...].astype(F32)
        dn = jnp.where(i == nblk - 1, 0.0, dn_ref[...].astype(F32))
        de = jnp.concatenate([d, dn], axis=0)
        t = i * bs + lax.broadcasted_iota(jnp.int32, (bs + POOL_HALO, 1), 0)
        for gi, w in enumerate(POOL_WINDOWS):
            cols = slice(gi * pg, (gi + 1) * pg)
            inv = 1.0 / jnp.minimum(t + 1, w).astype(F32)
            band = _band(bs, bs + POOL_HALO, 0, w)
            o_ref[:, cols] = (_band_apply(band, de[:, cols] * inv) - d[:, cols]).astype(BF16)

    row = pl.BlockSpec((bs, D), lambda i: (i, 0))
    nxt = pl.BlockSpec((POOL_HALO, D), lambda i: (jnp.minimum((i + 1) * hb, S // POOL_HALO - 1), 0))
    return pl.pallas_call(
        body, name=name, grid=(nblk,), in_specs=[row, nxt], out_specs=row, out_shape=SDS((S, D), BF16),
        compiler_params=_cp("parallel"))(dd, dd)


def _colsum(x):
    return jnp.sum(x, axis=0, keepdims=True)


def _accumulate_rows(st_ref, rows, first):
    @pl.when(first)
    def _():
        st_ref[...] = jnp.zeros_like(st_ref)

    for r, row in enumerate(rows):
        st_ref[r:r + 1, :] += row


def _norm_bwd(h, g, scale, du, dh_out, name, prev=None):
    S, D = h.shape
    bs = _tile(S, 256, 16)
    with_prev = prev is not None

    def body(*refs):
        h_ref, g_ref, sc_ref, du_ref, dho_ref = refs[:5]
        if with_prev:
            y_ref, cv_ref, dh_ref, dy_ref, st_ref = refs[5:]
        else:
            dh_ref, st_ref = refs[5:]
        hh = h_ref[...]
        du_ = du_ref[...].astype(F32)
        r = lax.rsqrt(jnp.mean(hh * hh, axis=-1, keepdims=True) + RMS_EPS)
        xhat = hh * r
        dn = du_ * (1.0 + sc_ref[...])
        dxhat = dn * g_ref[...]
        dh = dho_ref[...] + r * (dxhat - xhat * jnp.mean(dxhat * xhat, axis=-1, keepdims=True))
        dh_ref[...] = dh
        rows = [_colsum(du_), _colsum(du_ * (xhat * g_ref[...])), _colsum(dn * xhat)]
        if with_prev:
            dy_ref[...] = (dh * cv_ref[...]).astype(BF16)
            rows.append(_colsum(dh * y_ref[...].astype(F32)))
        _accumulate_rows(st_ref, rows, pl.program_id(0) == 0)

    row = pl.BlockSpec((bs, D), lambda i: (i, 0))
    st_spec = pl.BlockSpec((8, D), lambda i: (0, 0))
    in_specs = [row, _vec_spec(D), _vec_spec(D), row, row]
    args = [h, g, scale, du, dh_out]
    out_specs, out_shape = [row], [SDS((S, D), F32)]
    if with_prev:
        in_specs += [row, _vec_spec(D)]
        args += list(prev)
        out_specs.append(row)
        out_shape.append(SDS((S, D), BF16))
    out_specs.append(st_spec)
    out_shape.append(SDS((8, D), F32))
    return pl.pallas_call(
        body, name=name, grid=(S // bs,), in_specs=in_specs, out_specs=out_specs, out_shape=out_shape,
        compiler_params=_cp("arbitrary"))(*args)


def _loss_head(h, g, target, y, cvec, name):
    S, D = h.shape
    bs = _tile(S, 256, 16)

    def body(h_ref, g_ref, t_ref, y_ref, cv_ref, dh_ref, dy_ref, st_ref):
        hh = h_ref[...]
        r = lax.rsqrt(jnp.mean(hh * hh, axis=-1, keepdims=True) + RMS_EPS)
        xhat = hh * r
        err = xhat * g_ref[...] - t_ref[...]
        dout = err * (1.0 / D)
        dxhat = dout * g_ref[...]
        dh = r * (dxhat - xhat * jnp.mean(dxhat * xhat, axis=-1, keepdims=True))
        dh_ref[...] = dh
        dy_ref[...] = (dh * cv_ref[...]).astype(BF16)
        rows = [_colsum(dout * xhat), _colsum(dh * y_ref[...].astype(F32)), _colsum(err * err) * (0.5 / D)]
        _accumulate_rows(st_ref, rows, pl.program_id(0) == 0)

    row = pl.BlockSpec((bs, D), lambda i: (i, 0))
    return pl.pallas_call(
        body, name=name, grid=(S // bs,), in_specs=[row, _vec_spec(D), row, row, _vec_spec(D)],
        out_specs=[row, row, pl.BlockSpec((8, D), lambda i: (0, 0))],
        out_shape=[SDS((S, D), F32), SDS((S, D), BF16), SDS((8, D), F32)],
        compiler_params=_cp("arbitrary"))(h, g, target, y, cvec)


def _sum_all(x, name):
    def body(x_ref, o_ref):
        o_ref[...] = jnp.sum(jnp.sum(x_ref[...], axis=1, keepdims=True), axis=0, keepdims=True)

    return pl.pallas_call(body, name=name, out_shape=SDS((1, 1), F32), in_specs=[VMEM_WHOLE],
                          out_specs=VMEM_WHOLE)(x)


def _conv_mid(u3, cw, name):
    S, D3 = u3.shape
    D = D3 // 3
    cb = _tile(D, 512, 128)
    nj = D // cb
    bs = _tile(S, 256, CONV_HALO)
    hb = bs // CONV_HALO

    def body(b_ref, c_ref, v_ref, ch_ref, vh_ref, w_ref, o_ref):
        i = pl.program_id(0)
        z = c_ref[...].astype(F32) * v_ref[...].astype(F32)
        zh = jnp.where(i == 0, 0.0, ch_ref[...].astype(F32) * vh_ref[...].astype(F32))
        ze = jnp.concatenate([zh, z], axis=0)
        w = w_ref[...]
        zc = w[2:3] * z
        zc = zc + w[1:2] * _band_apply(_band(bs, bs + CONV_HALO, CONV_HALO - 1, CONV_HALO), ze)
        zc = zc + w[0:1] * _band_apply(_band(bs, bs + CONV_HALO, CONV_HALO - 2, CONV_HALO - 1), ze)
        o_ref[...] = (b_ref[...].astype(F32) * zc).astype(BF16)

    def blk(off):
        return pl.BlockSpec((bs, cb), lambda i, j: (i, off + j))

    def halo(off):
        return pl.BlockSpec((CONV_HALO, cb), lambda i, j: (jnp.maximum(i * hb - 1, 0), off + j))

    return pl.pallas_call(
        body, name=name, grid=(S // bs, nj),
        in_specs=[blk(0), blk(nj), blk(2 * nj), halo(nj), halo(2 * nj), pl.BlockSpec((3, cb), lambda i, j: (0, j))],
        out_specs=pl.BlockSpec((bs, cb), lambda i, j: (i, j)), out_shape=SDS((S, D), BF16),
        compiler_params=_cp("parallel", "parallel"))(u3, u3, u3, u3, u3, cw)


def _conv_mid_bwd(u3, da, cw, name):
    S, D3 = u3.shape
    D = D3 // 3
    cb = _tile(D, 512, 128)
    nj = D // cb
    bs = _tile(S, 256, CONV_HALO)
    hb = bs // CONV_HALO
    nblk = S // bs
    last_halo = S // CONV_HALO - 1

    def body(b_ref, c_ref, v_ref, ch_ref, vh_ref, bn_ref, da_ref, dan_ref, w_ref, db_ref, dc_ref, dv_ref, dw_ref):
        i = pl.program_id(0)
        c = c_ref[...].astype(F32)
        v = v_ref[...].astype(F32)
        b = b_ref[...].astype(F32)
        da_ = da_ref[...].astype(F32)
        z = c * v
        zh = jnp.where(i == 0, 0.0, ch_ref[...].astype(F32) * vh_ref[...].astype(F32))
        ze = jnp.concatenate([zh, z], axis=0)
        z1 = _band_apply(_band(bs, bs + CONV_HALO, CONV_HALO - 1, CONV_HALO), ze)
        z2 = _band_apply(_band(bs, bs + CONV_HALO, CONV_HALO - 2, CONV_HALO - 1), ze)
        w = w_ref[...]
        zc = w[2:3] * z + w[1:2] * z1 + w[0:1] * z2
        db_ref[...] = (da_ * zc).astype(BF16)
        dzc = da_ * b
        dzn = jnp.where(i == nblk - 1, 0.0, dan_ref[...].astype(F32) * bn_ref[...].astype(F32))
        dze = jnp.concatenate([dzc, dzn], axis=0)
        dz = w[2:3] * dzc
        dz = dz + w[1:2] * _band_apply(_band(bs, bs + CONV_HALO, 1, 2), dze)
        dz = dz + w[0:1] * _band_apply(_band(bs, bs + CONV_HALO, 2, 3), dze)
        dc_ref[...] = (dz * v).astype(BF16)
        dv_ref[...] = (dz * c).astype(BF16)
        dw_ref[...] = jnp.zeros_like(dw_ref)
        dw_ref[0:1, :] = _colsum(dzc * z2)
        dw_ref[1:2, :] = _colsum(dzc * z1)
        dw_ref[2:3, :] = _colsum(dzc * z)

    def blk(off):
        return pl.BlockSpec((bs, cb), lambda i, j: (i, off + j))

    def halo(off):
        return pl.BlockSpec((CONV_HALO, cb), lambda i, j: (jnp.maximum(i * hb - 1, 0), off + j))

    def nxt(off):
        return pl.BlockSpec((CONV_HALO, cb), lambda i, j: (jnp.minimum((i + 1) * hb, last_halo), off + j))

    o_spec = pl.BlockSpec((bs, cb), lambda i, j: (i, j))
    return pl.pallas_call(
        body, name=name, grid=(nblk, nj),
        in_specs=[blk(0), blk(nj), blk(2 * nj), halo(nj), halo(2 * nj), nxt(0), o_spec, nxt(0),
                  pl.BlockSpec((3, cb), lambda i, j: (0, j))],
        out_specs=[o_spec, o_spec, o_spec, pl.BlockSpec((None, 8, cb), lambda i, j: (i, 0, j))],
        out_shape=[SDS((S, D), BF16)] * 3 + [SDS((nblk, 8, D), F32)],
        compiler_params=_cp("parallel", "parallel"))(u3, u3, u3, u3, u3, u3, da, da, cw)


def _sum_lead(x, name):
    n, r, C = x.shape

    def body(x_ref, o_ref):
        @pl.when(pl.program_id(0) == 0)
        def _():
            o_ref[...] = x_ref[...]

        @pl.when(pl.program_id(0) > 0)
        def _():
            o_ref[...] += x_ref[...]

    return pl.pallas_call(
        body, name=name, grid=(n,), in_specs=[pl.BlockSpec((None, r, C), lambda i: (i, 0, 0))],
        out_specs=pl.BlockSpec((r, C), lambda i: (0, 0)), out_shape=SDS((r, C), F32),
        compiler_params=_cp("arbitrary"))(x)


def _log_sigmoids(z):
    lb = jnp.minimum(z, 0.0) - jnp.log(1.0 + jnp.exp(-jnp.abs(z)))
    return lb, lb - z


def _attn_blocks(S):
    bk = _tile(S, 256, 128)
    bq = 2 * bk if S % (2 * bk) == 0 else bk
    return bq, bk


def _tri(n, pred):
    rowi = lax.broadcasted_iota(jnp.int32, (n, n), 0)
    coli = lax.broadcasted_iota(jnp.int32, (n, n), 1)
    return jnp.where(pred(rowi, coli), 1.0, 0.0).astype(BF16)


def _causal_mask(bq, bk, m):
    rowi = lax.broadcasted_iota(jnp.int32, (bq, bk), 0)
    coli = lax.broadcasted_iota(jnp.int32, (bq, bk), 1)
    return m * bk + coli < rowi


def _sb_attention(qkv, name):
    S, D3 = qkv.shape
    D = D3 // 3
    H = D // HEAD_DIM
    bq, bk = _attn_blocks(S)
    nq, r = S // bq, bq // bk
    unroll = r
    scale = HEAD_DIM ** -0.5

    def body(q_ref, k_ref, v_ref, o_ref, lt_ref):
        i = pl.program_id(1)
        q = q_ref[...]
        after = _tri(bk, lambda j, s: j > s)

        def block(kb, carry, acc, causal):
            rows = pl.ds(pl.multiple_of(kb * bk, bk), bk)
            z = _dot_nt(q, k_ref[rows, :]) * scale
            lb, l1 = _log_sigmoids(z)
            if causal is not None:
                l1 = jnp.where(causal, l1, 0.0)
            a = jnp.exp(lb + (_dot(l1.astype(BF16), after) + carry))
            if causal is not None:
                a = jnp.where(causal, a, 0.0)
            acc = acc + _dot(a.astype(BF16), v_ref[rows, :])
            return carry + jnp.sum(l1, axis=1, keepdims=True), acc

        carry, acc = jnp.zeros((bq, 1), F32), jnp.zeros((bq, HEAD_DIM), F32)
        for m in reversed(range(r)):
            carry, acc = block(i * r + m, carry, acc, _causal_mask(bq, bk, m))

        def step(j, ca):
            for n in range(unroll):
                ca = block(i * r - 1 - unroll * j - n, ca[0], ca[1], None)
            return ca

        carry, acc = lax.fori_loop(0, i * (r // unroll), step, (carry, acc))
        o_ref[...] = acc.astype(BF16)
        lt_ref[...] = jnp.broadcast_to(carry, (bq, HEAD_DIM))

    head_rows = lambda off: pl.BlockSpec((S, HEAD_DIM), lambda hd, i: (0, off + hd))
    blk = pl.BlockSpec((bq, HEAD_DIM), lambda hd, i: (i, hd))
    return pl.pallas_call(
        body, name=name, grid=(H, nq), in_specs=[blk, head_rows(H), head_rows(2 * H)],
        out_specs=[blk, blk], out_shape=[SDS((S, D), BF16), SDS((S, D), F32)],
        compiler_params=_cp("parallel", "arbitrary"))(qkv, qkv, qkv)


def _sb_attention_bwd(qkv, ltot, do, name):
    S, D3 = qkv.shape
    D = D3 // 3
    H = D // HEAD_DIM
    bq, bk = _attn_blocks(S)
    nq, r = S // bq, bq // bk
    unroll = r
    scale = HEAD_DIM ** -0.5

    def body(q_ref, k_ref, v_ref, lt_ref, do_ref, dq_ref, dk_ref, dv_ref, dkt_acc, dvt_acc):
        i = pl.program_id(1)
        q = q_ref[...]
        do_ = do_ref[...]
        qt = jnp.transpose(q.astype(F32)).astype(BF16)
        dot = jnp.transpose(do_.astype(F32)).astype(BF16)
        lt = lt_ref[:, 0:1]
        after = _tri(bk, lambda j, s: j > s)
        before = _tri(bk, lambda j, s: j < s)

        @pl.when(i == 0)
        def _():
            dkt_acc[...] = jnp.zeros_like(dkt_acc)
            dvt_acc[...] = jnp.zeros_like(dvt_acc)

        def block(kb, c1, ce, dq, causal):
            rows = pl.ds(pl.multiple_of(kb * bk, bk), bk)
            k = k_ref[rows, :]
            v = v_ref[rows, :]
            z = _dot_nt(q, k) * scale
            lb, l1 = _log_sigmoids(z)
            sig = jnp.exp(lb)
            if causal is not None:
                l1 = jnp.where(causal, l1, 0.0)
            c1 = c1 + jnp.sum(l1, axis=1, keepdims=True)
            a = jnp.exp(lb + (_dot(l1.astype(BF16), after) + (lt - c1)))
            if causal is not None:
                a = jnp.where(causal, a, 0.0)
            e = a * _dot_nt(do_, v)
            p = _dot(e.astype(BF16), before) + ce
            dz = e - sig * (e + p)
            if causal is not None:
                dz = jnp.where(causal, dz, 0.0)
            dzb = dz.astype(BF16)
            dkt_acc[kb] += _dot(qt, dzb)
            dvt_acc[kb] += _dot(dot, a.astype(BF16))
            dq = dq + _dot(dzb, k)
            return c1, ce + jnp.sum(e, axis=1, keepdims=True), dq

        def step(j, st):
            for n in range(unroll):
                st = block(unroll * j + n, st[0], st[1], st[2], None)
            return st

        zero = jnp.zeros((bq, 1), F32)
        c1, ce, dq = lax.fori_loop(0, i * (r // unroll), step, (zero, zero, jnp.zeros((bq, HEAD_DIM), F32)))
        for m in range(r):
            c1, ce, dq = block(i * r + m, c1, ce, dq, _causal_mask(bq, bk, m))
        dq_ref[...] = (dq * scale).astype(BF16)

        @pl.when(i == nq - 1)
        def _():
            def flush(kb, _):
                rows = pl.ds(pl.multiple_of(kb * bk, bk), bk)
                dk_ref[rows, :] = (jnp.transpose(dkt_acc[kb]) * scale).astype(BF16)
                dv_ref[rows, :] = jnp.transpose(dvt_acc[kb]).astype(BF16)
                return 0

            lax.fori_loop(0, S // bk, flush, 0)

    head_rows = lambda off: pl.BlockSpec((S, HEAD_DIM), lambda hd, i: (0, off + hd))
    blk = pl.BlockSpec((bq, HEAD_DIM), lambda hd, i: (i, hd))
    return pl.pallas_call(
        body, name=name, grid=(H, nq), in_specs=[blk, head_rows(H), head_rows(2 * H), blk, blk],
        out_specs=[blk, head_rows(0), head_rows(0)], out_shape=[SDS((S, D), BF16)] * 3,
        scratch_shapes=[pltpu.VMEM((S // bk, HEAD_DIM, bk), F32), pltpu.VMEM((S // bk, HEAD_DIM, bk), F32)],
        compiler_params=_cp("arbitrary", "arbitrary"))(qkv, qkv, qkv, ltot, do)


def kernel(x, c, norm_mix_g, norm_ffn_g, w_mod, b_mod, pool_w, pool_scale, conv_w_in, conv_w, conv_w_out, sb_w_qkv, sb_w_o, ffn_w_gate, ffn_w_up, ffn_w_down, final_g, loss_target, m_norm_mix_g, m_norm_ffn_g, m_w_mod, m_b_mod, m_pool_w, m_pool_scale, m_conv_w_in, m_conv_w, m_conv_w_out, m_sb_w_qkv, m_sb_w_o, m_ffn_w_gate, m_ffn_w_up, m_ffn_w_down, m_final_g, v_norm_mix_g, v_norm_ffn_g, v_w_mod, v_b_mod, v_pool_w, v_pool_scale, v_conv_w_in, v_conv_w, v_conv_w_out, v_sb_w_qkv, v_sb_w_o, v_ffn_w_gate, v_ffn_w_up, v_ffn_w_down, v_final_g):
    S, D = x.shape[1], x.shape[2]
    L = norm_mix_g.shape[0]
    nmod = w_mod.shape[2]
    nf = ffn_w_gate.shape[2]
    n3 = conv_w_in.shape[2]
    nd = conv_w_out.shape[1]
    cb = n3 // 3
    ng = pool_w.shape[1]
    pg = pool_w.shape[3]
    n_pool = pool_w.shape[0]
    assert D % HEAD_DIM == 0 and S % 256 == 0 and nd == cb and N_CHIPS * nd == D and pg * ng == D

    mx, my, mc = lax.axis_index("x"), lax.axis_index("y"), lax.axis_index("c")
    chip = 2 * mx + my
    dev = 2 * chip + mc
    hx, ht = x[0], loss_target[0]

    c_all = _allgather8(jnp.broadcast_to(c, (8, D)), "gather_c").reshape(N_DEV, 8, D)[:, 0]
    c_rows = jnp.concatenate([c_all, jnp.zeros((8, D), F32)], axis=0)
    b_cols = lax.dynamic_slice_in_dim(b_mod, chip * nmod, nmod, axis=1).reshape(L, 1, nmod)
    mod_cols = _mod_fwd(c_rows, w_mod, b_cols, "mod_fwd")
    mod_all = _allgather8(mod_cols.reshape(L * 16, nmod), "gather_mod").reshape(N_CHIPS, 2, L, 16, nmod)
    mod = lax.dynamic_index_in_dim(mod_all[:, 0], dev, axis=2, keepdims=False)
    mod = jnp.transpose(mod, (1, 0, 2)).reshape(L, N_MOD, 1, D)

    bf = lambda w: w.astype(BF16)
    n_conv, n_sb = conv_w_in.shape[0], sb_w_qkv.shape[0]
    w_gate, w_up, w_down = [], [], []
    for l in range(L):
        g4, u4, d4 = _gather_shards([bf(ffn_w_gate[l]), bf(ffn_w_up[l]), bf(ffn_w_down[l])], f"gather_ffn{l}")
        w_gate.append(g4)
        w_up.append(u4)
        w_down.append(d4.reshape(N_CHIPS * nf, D))
    w_in, w_out, w_qkv, w_o, w_pool = [], [], [], [], []
    for j in range(n_conv):
        i4, o4 = _gather_shards([bf(conv_w_in[j]), bf(conv_w_out[j])], f"gather_conv{j}")
        w_in.append(i4)
        w_out.append(o4.reshape(D, D))
    for j in range(n_sb):
        i4, o4 = _gather_shards([bf(sb_w_qkv[j]), bf(sb_w_o[j])], f"gather_sb{j}")
        w_qkv.append(i4)
        w_o.append(o4.reshape(D, D))
    for j in range(n_pool):
        (p4,) = _gather_shards([bf(pool_w[j]).reshape(ng * (pg // N_CHIPS), pg)], f"gather_pool{j}")
        w_pool.append(jnp.transpose(p4.reshape(N_CHIPS, ng, pg // N_CHIPS, pg), (1, 0, 2, 3)).reshape(ng, pg, pg))
    taps_cols = jnp.concatenate([pool_scale, conv_w.reshape(-1, nd)], axis=0)
    n_small = taps_cols.shape[0]
    small_rows = jnp.concatenate([taps_cols, jnp.zeros((16 - n_small, nd), F32)], axis=0)
    small_all = _allgather8(small_rows, "gather_small").reshape(N_CHIPS, 2, 16, nd)[:, 0]
    small_full = jnp.transpose(small_all, (1, 0, 2)).reshape(16, D)
    pool_scale_full = small_full[:n_pool]
    conv_taps_full = small_full[n_pool:n_small].reshape(n_conv, 3, D)

    saved = []
    h = hx
    for l in range(L):
        kind, j = l % 3, l // 3
        sh_m, sc_m, gt_m, sh_f, sc_f, gt_f = (mod[l, r] for r in range(N_MOD))
        gm = norm_mix_g[l].reshape(1, D)
        gf = norm_ffn_g[l].reshape(1, D)
        s = {"h_in": h}
        if kind == 0:
            s["diff"] = _pool_pre(h, gm, sc_m, sh_m, f"pool_pre{l}")
            s["cvec_m"] = gt_m * pool_scale_full[j].reshape(1, D)
            h, s["y_m"] = _mm_out_res(s["diff"], w_pool[j], h, s["cvec_m"], f"pool_mm{l}", groups=True)
        elif kind == 1:
            s["u"] = _norm_mod(h, gm, sc_m, sh_m, f"norm_mix{l}")
            s["u3"] = _mm_in(s["u"], w_in[j], n3, f"conv_in{l}")
            s["a_m"] = _conv_mid(s["u3"], conv_taps_full[j], f"conv_mid{l}")
            s["cvec_m"] = gt_m
            h, s["y_m"] = _mm_out_res(s["a_m"], w_out[j], h, gt_m, f"conv_out{l}")
        else:
            s["u"] = _norm_mod(h, gm, sc_m, sh_m, f"norm_mix{l}")
            s["qkv"] = _mm_in(s["u"], w_qkv[j], n3, f"sb_qkv{l}")
            s["o"], s["ltot"] = _sb_attention(s["qkv"], f"sb_attn{l}")
            s["cvec_m"] = gt_m
            h, s["y_m"] = _mm_out_res(s["o"], w_o[j], h, gt_m, f"sb_out{l}")
        s["h_mid"] = h
        s["u2"] = _norm_mod(h, gf, sc_f, sh_f, f"norm_ffn{l}")
        s["gate"] = _mm_in(s["u2"], w_gate[l], nf, f"ffn_gate{l}")
        s["up"], s["a_f"] = _mm_in(s["u2"], w_up[l], nf, f"ffn_up{l}", gate=s["gate"])
        h, s["y_f"] = _mm_out_res(s["a_f"], w_down[l], h, gt_f, f"ffn_down{l}")
        saved.append(s)

    gt_f_last = mod[L - 1, 5]
    dh, dy, st = _loss_head(h, final_g.reshape(1, D), ht, saved[-1]["y_f"], gt_f_last, "loss_head")
    loss = lax.psum(_sum_all(st[2:3], "loss_sum")[0, 0], ("x", "y", "c"))
    d_final_g = st[0:1]
    p_gate_f = st[1:2]
    d_norm_mix, d_norm_ffn = [None] * L, [None] * L
    d_mod = [[None] * N_MOD for _ in range(L)]
    d_pool_scale, d_taps = [None] * n_pool, [None] * n_conv
    big = {}
    for l in reversed(range(L)):
        kind, j = l % 3, l // 3
        s = saved[l]
        sh_m, sc_m, gt_m, sh_f, sc_f, gt_f = (mod[l, r] for r in range(N_MOD))
        gm = norm_mix_g[l].reshape(1, D)
        gf = norm_ffn_g[l].reshape(1, D)
        d_mod[l][5] = p_gate_f
        dgate, dup = _mm_nt(dy, w_down[l], f"ffn_down_bwd{l}", swiglu=(s["gate"], s["up"]))
        gw_down = _mm_tn(s["a_f"], dy, f"ffn_down_wgrad{l}").reshape(N_CHIPS, nf, D)
        gw_gate = _mm_tn(s["u2"], dgate, f"ffn_gate_wgrad{l}", shard_cols=nf)
        gw_up = _mm_tn(s["u2"], dup, f"ffn_up_wgrad{l}", shard_cols=nf)
        du2 = _mm_nt_acc(dgate, w_gate[l], f"ffn_gate_bwd{l}")
        du2 = _mm_nt_acc(dup, w_up[l], f"ffn_up_bwd{l}", add=du2)
        big[("ffn", l)] = _reduce_to_owner([gw_gate, gw_up, gw_down], f"reduce_ffn{l}")
        dh, dy, st = _norm_bwd(s["h_mid"], gf, sc_f, du2, dh, f"norm_ffn_bwd{l}", prev=(s["y_m"], s["cvec_m"]))
        d_mod[l][3], d_mod[l][4], d_norm_ffn[l] = st[0:1], st[1:2], st[2:3]
        p_mix = st[3:4]
        if kind == 0:
            d_mod[l][2] = p_mix * pool_scale_full[j].reshape(1, D)
            d_pool_scale[j] = p_mix * gt_m
            dd = _mm_nt(dy, w_pool[j], f"pool_mm_bwd{l}", groups=True)
            gw_pool = _mm_tn(s["diff"], dy, f"pool_wgrad{l}", groups=ng)
            big[("pool", j)] = gw_pool
            du = _pool_post(dd, f"pool_post{l}")
        elif kind == 1:
            d_mod[l][2] = p_mix
            da = _mm_nt(dy, w_out[j], f"conv_out_bwd{l}")
            gw_out = _mm_tn(s["a_m"], dy, f"conv_out_wgrad{l}").reshape(N_CHIPS, nd, D)
            db, dc, dv, dtap = _conv_mid_bwd(s["u3"], da, conv_taps_full[j], f"conv_mid_bwd{l}")
            d_taps[j] = _sum_lead(dtap, f"conv_tap_sum{l}")[0:3]
            du3 = jnp.concatenate([db, dc, dv], axis=1)
            gw_in = _mm_tn(s["u"], du3, f"conv_in_wgrad{l}", shard_cols=n3)
            du = _mm_nt_acc(du3, w_in[j], f"conv_in_bwd{l}")
            big[("conv", j)] = _reduce_to_owner([gw_in, gw_out], f"reduce_conv{l}")
        else:
            d_mod[l][2] = p_mix
            do = _mm_nt(dy, w_o[j], f"sb_out_bwd{l}")
            gw_o = _mm_tn(s["o"], dy, f"sb_out_wgrad{l}").reshape(N_CHIPS, nd, D)
            dq, dk, dv = _sb_attention_bwd(s["qkv"], s["ltot"], do, f"sb_attn_bwd{l}")
            dqkv = jnp.concatenate([dq, dk, dv], axis=1)
            gw_qkv = _mm_tn(s["u"], dqkv, f"sb_qkv_wgrad{l}", shard_cols=n3)
            du = _mm_nt_acc(dqkv, w_qkv[j], f"sb_qkv_bwd{l}")
            big[("sb", j)] = _reduce_to_owner([gw_qkv, gw_o], f"reduce_sb{l}")
        if l > 0:
            prev = (saved[l - 1]["y_f"], mod[l - 1, 5])
            dh, dy, st = _norm_bwd(s["h_in"], gm, sc_m, du, dh, f"norm_mix_bwd{l}", prev=prev)
            p_gate_f = st[3:4]
        else:
            dh, st = _norm_bwd(s["h_in"], gm, sc_m, du, dh, f"norm_mix_bwd{l}")
        d_mod[l][0], d_mod[l][1], d_norm_mix[l] = st[0:1], st[1:2], st[2:3]
    grad_x = dh.reshape(1, S, D)

    gw_pool = jnp.stack([big[("pool", j)] for j in range(n_pool)])
    gw_pool = jnp.transpose(gw_pool.reshape(n_pool, ng, N_CHIPS, pg // N_CHIPS, pg), (2, 0, 1, 3, 4))
    (g_pool,) = _reduce_to_owner([gw_pool.reshape(N_CHIPS, n_pool * ng * (pg // N_CHIPS), pg)], "reduce_pool")

    rows = [d_final_g] + d_norm_mix + d_norm_ffn + [r for l in range(L) for r in d_mod[l]] + d_pool_scale
    rows += [d_taps[j] for j in range(n_conv)]
    vec = jnp.concatenate(rows, axis=0)
    n_rows = vec.shape[0]
    pad = -n_rows % 8
    vec = jnp.concatenate([vec, jnp.zeros((pad, D), F32)], axis=0) if pad else vec
    vec_all = _allgather8(vec, "gather_small_grads").reshape(N_DEV, n_rows + pad, D)
    tot = _sum_devices(vec_all, "sum_small_grads")
    r0 = 1 + 2 * L
    g_final = tot[0]
    g_norm_mix = tot[1:1 + L]
    g_norm_ffn = tot[1 + L:r0]
    g_b_mod = tot[r0:r0 + N_MOD * L].reshape(L, N_MOD * D)
    r1 = r0 + N_MOD * L
    g_pool_scale = lax.dynamic_slice_in_dim(tot[r1:r1 + n_pool], chip * nd, nd, axis=1)
    g_taps = lax.dynamic_slice_in_dim(tot[r1 + n_pool:r1 + n_pool + 3 * n_conv], chip * nd, nd, axis=1)
    g_conv_w = g_taps.reshape(conv_w.shape)
    dmod_all = vec_all[:, r0:r1].reshape(N_DEV, L, N_MOD * D)
    dmod_cols = jnp.transpose(lax.dynamic_slice_in_dim(dmod_all, chip * nmod, nmod, axis=2), (1, 0, 2))
    g_w_mod = _mod_wgrad(jnp.transpose(c_all), dmod_cols, "mod_wgrad")

    g_ffn_gate = jnp.stack([big[("ffn", l)][0] for l in range(L)])
    g_ffn_up = jnp.stack([big[("ffn", l)][1] for l in range(L)])
    g_ffn_down = jnp.stack([big[("ffn", l)][2] for l in range(L)])
    g_conv_in = jnp.stack([big[("conv", j)][0] for j in range(n_conv)])
    g_conv_out = jnp.stack([big[("conv", j)][1] for j in range(n_conv)])
    g_sb_qkv = jnp.stack([big[("sb", j)][0] for j in range(n_sb)])
    g_sb_o = jnp.stack([big[("sb", j)][1] for j in range(n_sb)])
    g_pool_w = g_pool.reshape(pool_w.shape)

    grads = [g_norm_mix, g_norm_ffn, g_w_mod, g_b_mod, g_pool_w, g_pool_scale, g_conv_in, g_conv_w, g_conv_out,
             g_sb_qkv, g_sb_o, g_ffn_gate, g_ffn_up, g_ffn_down, g_final]
    weights = [norm_mix_g, norm_ffn_g, w_mod, b_mod, pool_w, pool_scale, conv_w_in, conv_w, conv_w_out,
               sb_w_qkv, sb_w_o, ffn_w_gate, ffn_w_up, ffn_w_down, final_g]
    ms = [m_norm_mix_g, m_norm_ffn_g, m_w_mod, m_b_mod, m_pool_w, m_pool_scale, m_conv_w_in, m_conv_w, m_conv_w_out,
          m_sb_w_qkv, m_sb_w_o, m_ffn_w_gate, m_ffn_w_up, m_ffn_w_down, m_final_g]
    vs = [v_norm_mix_g, v_norm_ffn_g, v_w_mod, v_b_mod, v_pool_w, v_pool_scale, v_conv_w_in, v_conv_w, v_conv_w_out,
          v_sb_w_qkv, v_sb_w_o, v_ffn_w_gate, v_ffn_w_up, v_ffn_w_down, v_final_g]
    deltas, new_ms, new_vs = [], [], []
    for n, (w, g, m, v) in enumerate(zip(weights, grads, ms, vs)):
        if w.ndim == 1:
            w, g, m, v = (a.reshape(1, -1) for a in (w, g, m, v))
        g = g.reshape(w.shape)
        grads[n] = g.reshape(weights[n].shape)
        d, nm, nv = _adamw(w, g, m, v, f"adamw{n}")
        deltas.append(d.reshape(weights[n].shape))
        new_ms.append(nm.reshape(weights[n].shape))
        new_vs.append(nv.reshape(weights[n].shape))
    return (loss, grad_x, *grads, *deltas, *new_ms, *new_vs)
```

```python
import functools

import jax
import jax.numpy as jnp
from jax import lax
from jax.experimental import pallas as pl
from jax.experimental.pallas import tpu as pltpu

F32 = jnp.float32
BF16 = jnp.bfloat16
SDS = jax.ShapeDtypeStruct
MESH = pl.DeviceIdType.MESH

RMS_EPS = 1e-6
POOL_WINDOWS = (2, 4, 8, 16)
POOL_HALO = 16
CONV_HALO = 16
HEAD_DIM = 128
N_MOD = 6
N_CHIPS = 4
N_DEV = 8
ADAM_LR = 0.001
ADAM_B1 = 0.9
ADAM_B2 = 0.999
ADAM_EPS = 1e-08
ADAM_WD = 0.01
ADAM_STEP = 10
VMEM_LIMIT_V7X = 52 * 1024 * 1024
ANY = pl.BlockSpec(memory_space=pl.ANY)
VMEM_WHOLE = pl.BlockSpec(memory_space=pltpu.VMEM)


def _cp(*sem):
    return pltpu.CompilerParams(dimension_semantics=sem, vmem_limit_bytes=VMEM_LIMIT_V7X)


def _tile(n, pref, unit):
    if n <= pref:
        return n
    t = (pref // unit) * unit
    while t >= unit:
        if n % t == 0:
            return t
        t -= unit
    return n


def _dot(a, b):
    return jnp.dot(a, b, preferred_element_type=F32)


def _dot_nt(a, b):
    return lax.dot_general(a, b, (((1,), (1,)), ((), ())), preferred_element_type=F32)


def _dot_tn(a, b):
    return lax.dot_general(a, b, (((0,), (0,)), ((), ())), preferred_element_type=F32)


def _split_bf16(x):
    hi = x.astype(BF16)
    lo = (x - hi.astype(F32)).astype(BF16)
    return hi, lo


def _sigmoid(x):
    return 1.0 / (1.0 + jnp.exp(-x))


def _my_place():
    return lax.axis_index("x"), lax.axis_index("y"), lax.axis_index("c")


def _allgather8(blk, name):
    m, n = blk.shape

    def body(x_ref, out_ref, send_sems, recv_sems, local_sem):
        x, y, c = _my_place()
        me, sibling = (x, y, c), (x, y, 1 - c)
        chips = [(1 - x, y), (x, 1 - y), (1 - x, 1 - y)]

        def rows(px, py, pc):
            return out_ref.at[pl.ds((4 * px + 2 * py + pc) * m, m), :]

        def copy(k, block, to, src=None):
            return pltpu.make_async_remote_copy(
                src_ref=rows(*block) if src is None else src, dst_ref=rows(*block),
                send_sem=send_sems.at[k], recv_sem=recv_sems.at[k], device_id=to, device_id_type=MESH)

        mine = pltpu.make_async_copy(x_ref, rows(*me), local_sem)
        mine.start()
        first = [copy(0, me, sibling, src=x_ref)]
        first += [copy(1 + j, me, (*chip, c), src=x_ref) for j, chip in enumerate(chips)]
        for cp in first:
            cp.start()
        passed = [copy(4 + j, (*chip, c), sibling) for j, chip in enumerate(chips)]
        for j, chip in enumerate(chips):
            copy(1 + j, (*chip, c), me).wait_recv()
            passed[j].start()
        copy(0, sibling, me).wait_recv()
        for j, chip in enumerate(chips):
            copy(4 + j, (*chip, 1 - c), me).wait_recv()
        for cp in first + passed:
            cp.wait_send()
        mine.wait()

    return pl.pallas_call(
        body, name=name, out_shape=SDS((N_DEV * m, n), blk.dtype),
        in_specs=[VMEM_WHOLE], out_specs=VMEM_WHOLE,
        scratch_shapes=[pltpu.SemaphoreType.DMA((7,)), pltpu.SemaphoreType.DMA((7,)), pltpu.SemaphoreType.DMA],
    )(blk)


class _GatherShards:
    def __init__(self, ws):
        nt = len(ws)
        self.ws = ws
        self.inputs = list(ws)
        self.out_shapes = [SDS((N_CHIPS,) + w.shape, w.dtype) for w in ws]
        self.sem_shapes = [pltpu.SemaphoreType.DMA((6 * nt,)), pltpu.SemaphoreType.DMA((6 * nt,)),
                           pltpu.SemaphoreType.DMA((nt,))]

    def _copies(self, w_refs, out_refs, sems):
        send_sems, recv_sems, local_sems = sems
        x, y, c = _my_place()
        chips = [(1 - x, y), (x, 1 - y), (1 - x, 1 - y)]
        per_tensor = []
        for t, w in enumerate(self.ws):
            half = w.shape[0] // 2
            w_ref, out_ref = w_refs[t], out_refs[t]

            def dst(k, hc, out_ref=out_ref, half=half):
                return out_ref.at[k, pl.ds(hc * half, half), :]

            def copy(s, src, to_dst, to, t=t):
                return pltpu.make_async_remote_copy(
                    src_ref=src, dst_ref=to_dst, send_sem=send_sems.at[6 * t + s], recv_sem=recv_sems.at[6 * t + s],
                    device_id=to, device_id_type=MESH)

            mine = pltpu.make_async_copy(w_ref, out_ref.at[2 * x + y], local_sems.at[t])
            first = [copy(j, w_ref.at[pl.ds(c * half, half), :], dst(2 * x + y, c), (*chip, c))
                     for j, chip in enumerate(chips)]
            landed = [dst(2 * px + py, c) for px, py in chips]
            arrive = [copy(j, landed[j], landed[j], (*chips[j], c)) for j in range(3)]
            passed = [copy(3 + j, landed[j], landed[j], (x, y, 1 - c)) for j in range(3)]
            other = [dst(2 * px + py, 1 - c) for px, py in chips]
            from_sibling = [copy(3 + j, other[j], other[j], (x, y, 1 - c)) for j in range(3)]
            per_tensor.append((mine, first, arrive, passed, from_sibling))
        return per_tensor

    def start(self, w_refs, out_refs, sems):
        for mine, first, _, _, _ in self._copies(w_refs, out_refs, sems):
            mine.start()
            for cp in first:
                cp.start()

    def finish(self, w_refs, out_refs, sems):
        per_tensor = self._copies(w_refs, out_refs, sems)
        for _, _, arrive, passed, _ in per_tensor:
            for j in range(3):
                arrive[j].wait_recv()
                passed[j].start()
        for _, _, _, _, from_sibling in per_tensor:
            for cp in from_sibling:
                cp.wait_recv()
        for mine, first, _, passed, _ in per_tensor:
            for cp in first + passed:
                cp.wait_send()
            mine.wait()


class _ScatterToChips:
    def __init__(self, ps):
        nt = len(ps)
        self.ps = ps
        self.inputs = list(ps)
        self.out_shapes = [SDS((3,) + p.shape[1:], p.dtype) for p in ps]
        self.sem_shapes = [pltpu.SemaphoreType.DMA((3 * nt,)), pltpu.SemaphoreType.DMA((3 * nt,))]

    def _copies(self, p_refs, out_refs, sems):
        send_sems, recv_sems = sems
        x, y, c = _my_place()
        chips = [(1 - x, y), (x, 1 - y), (1 - x, 1 - y)]
        return [pltpu.make_async_remote_copy(
            src_ref=p_refs[t].at[2 * px + py], dst_ref=out_refs[t].at[j], send_sem=send_sems.at[3 * t + j],
            recv_sem=recv_sems.at[3 * t + j], device_id=(px, py, c), device_id_type=MESH)
            for t in range(len(self.ps)) for j, (px, py) in enumerate(chips)]

    def start(self, p_refs, out_refs, sems):
        for cp in self._copies(p_refs, out_refs, sems):
            cp.start()

    def finish(self, p_refs, out_refs, sems):
        for cp in self._copies(p_refs, out_refs, sems):
            cp.wait()


def _run_comm(comm, name):
    ni, no = len(comm.inputs), len(comm.out_shapes)

    def body(*refs):
        comm.start(refs[:ni], refs[ni:ni + no], refs[ni + no:])
        comm.finish(refs[:ni], refs[ni:ni + no], refs[ni + no:])

    return pl.pallas_call(body, name=name, out_shape=comm.out_shapes, in_specs=[ANY] * ni, out_specs=[ANY] * no,
                          scratch_shapes=comm.sem_shapes)(*comm.inputs)


def _launch(body, name, grid, in_specs, out_specs, out_shape, args, sem, scratch_shapes=(), comm=None):
    if comm is None:
        return pl.pallas_call(body, name=name, grid=grid, in_specs=in_specs, out_specs=out_specs,
                              out_shape=out_shape, scratch_shapes=list(scratch_shapes),
                              compiler_params=_cp(*sem))(*args), None
    single = not isinstance(out_shape, (list, tuple))
    out_specs_l = [out_specs] if single else list(out_specs)
    out_shape_l = [out_shape] if single else list(out_shape)
    n_in, n_out, n_scr = len(in_specs), len(out_shape_l), len(scratch_shapes)
    nci, nco = len(comm.inputs), len(comm.out_shapes)

    def carried(*refs):
        ins, refs = refs[:n_in], refs[n_in:]
        cins, refs = refs[:nci], refs[nci:]
        outs, refs = refs[:n_out], refs[n_out:]
        couts, refs = refs[:nco], refs[nco:]
        scr, sems = refs[:n_scr], refs[n_scr:]
        ids = [pl.program_id(ax) for ax in range(len(grid))]
        first = functools.reduce(jnp.logical_and, [i == 0 for i in ids])
        last = functools.reduce(jnp.logical_and, [i == g - 1 for i, g in zip(ids, grid)])

        @pl.when(first)
        def _():
            comm.start(cins, couts, sems)

        body(*ins, *outs, *scr)

        @pl.when(last)
        def _():
            comm.finish(cins, couts, sems)

    res = pl.pallas_call(
        carried, name=name, grid=grid, in_specs=list(in_specs) + [ANY] * nci, out_specs=out_specs_l + [ANY] * nco,
        out_shape=out_shape_l + list(comm.out_shapes), scratch_shapes=list(scratch_shapes) + list(comm.sem_shapes),
        compiler_params=_cp(*["arbitrary"] * len(grid)))(*args, *comm.inputs)
    main = res[:n_out]
    return (main[0] if single else main), res[n_out:]


def _swap_sibling_halves(gs, name):
    nt = len(gs)

    def body(*refs):
        g_refs, out_refs = refs[:nt], refs[nt:2 * nt]
        send_sems, recv_sems = refs[2 * nt:]
        x, y, c = _my_place()
        cps = []
        for t in range(nt):
            half = gs[t].shape[1] // 2
            cp = pltpu.make_async_remote_copy(
                src_ref=g_refs[t].at[:, pl.ds((1 - c) * half, half), :], dst_ref=out_refs[t],
                send_sem=send_sems.at[t], recv_sem=recv_sems.at[t], device_id=(x, y, 1 - c), device_id_type=MESH)
            cp.start()
            cps.append(cp)
        for cp in cps:
            cp.wait()

    return pl.pallas_call(
        body, name=name,
        out_shape=[SDS((g.shape[0], g.shape[1] // 2, g.shape[2]), g.dtype) for g in gs],
        in_specs=[ANY] * nt, out_specs=[ANY] * nt,
        scratch_shapes=[pltpu.SemaphoreType.DMA((nt,)), pltpu.SemaphoreType.DMA((nt,))],
    )(*gs)


def _join_sibling_halves(fs, name):
    nt = len(fs)

    def body(*refs):
        out_refs = refs[nt:2 * nt]
        send_sems, recv_sems = refs[2 * nt:]
        x, y, c = _my_place()
        cps = []
        for t in range(nt):
            r = fs[t].shape[0] // 2
            mine = out_refs[t].at[pl.ds(c * r, r), :]
            cp = pltpu.make_async_remote_copy(
                src_ref=mine, dst_ref=mine, send_sem=send_sems.at[t], recv_sem=recv_sems.at[t],
                device_id=(x, y, 1 - c), device_id_type=MESH)
            cp.start()
            cps.append((cp, r))
        for t, (cp, r) in enumerate(cps):
            cp.wait_send()
            other = out_refs[t].at[pl.ds((1 - c) * r, r), :]
            pltpu.make_async_remote_copy(
                src_ref=other, dst_ref=other, send_sem=send_sems.at[t], recv_sem=recv_sems.at[t],
                device_id=(x, y, 1 - c), device_id_type=MESH).wait_recv()

    return pl.pallas_call(
        body, name=name, out_shape=[SDS(f.shape, f.dtype) for f in fs],
        in_specs=[ANY] * nt, out_specs=[ANY] * nt, input_output_aliases={t: t for t in range(nt)},
        scratch_shapes=[pltpu.SemaphoreType.DMA((nt,)), pltpu.SemaphoreType.DMA((nt,))],
    )(*fs)


def _add_sibling(g, recv, name):
    _, R, C = g.shape
    half = R // 2
    br = _tile(half, max(16, (1 << 19) // C), 16)
    nrb = half // br

    def body(g_ref, r_ref, bf_ref, own_ref):
        s = g_ref[...] + r_ref[...]
        bf_ref[...] = s.astype(BF16)

        @pl.when(pl.program_id(1) == 2 * lax.axis_index("x") + lax.axis_index("y"))
        def _():
            own_ref[...] = s

    return pl.pallas_call(
        body, name=name, grid=(nrb, N_CHIPS),
        in_specs=[pl.BlockSpec((None, br, C), lambda i, k: (k, lax.axis_index("c") * nrb + i, 0)),
                  pl.BlockSpec((None, br, C), lambda i, k: (k, i, 0))],
        out_specs=[pl.BlockSpec((None, br, C), lambda i, k: (k, i, 0)),
                   pl.BlockSpec((br, C), lambda i, k: (i, 0))],
        out_shape=[SDS((N_CHIPS, half, C), BF16), SDS((half, C), F32)],
        compiler_params=_cp("arbitrary", "arbitrary"),
    )(g, recv)


def _add_chips(own, recv, name):
    r, C = own.shape
    br = _tile(r, max(16, (1 << 19) // C), 16)
    nrb = r // br

    def body(own_ref, r_ref, o_ref):
        s = own_ref[...]
        for j in range(3):
            s = s + r_ref[j].astype(F32)
        o_ref[...] = s

    return pl.pallas_call(
        body, name=name, grid=(nrb,),
        in_specs=[pl.BlockSpec((br, C), lambda i: (i, 0)), pl.BlockSpec((3, br, C), lambda i: (0, i, 0))],
        out_specs=pl.BlockSpec((br, C), lambda i: (lax.axis_index("c") * nrb + i, 0)),
        out_shape=SDS((2 * r, C), F32), compiler_params=_cp("arbitrary"),
    )(own, recv)


class _Reduce:
    def __init__(self, gs, name):
        self.name = name
        recv = _swap_sibling_halves(gs, name + "_swap")
        self.parts = [_add_sibling(g, r, name + "_add1") for g, r in zip(gs, recv)]
        self.scatter = _ScatterToChips([p[0] for p in self.parts])

    def finish(self, brought=None):
        if brought is None:
            brought = _run_comm(self.scatter, self.name + "_scatter")
        fins = [_add_chips(p[1], r, self.name + "_add2") for p, r in zip(self.parts, brought)]
        return _join_sibling_halves(fins, self.name + "_join")


def _sum_devices(allv, name):
    _, r, n = allv.shape

    def body(a_ref, o_ref):
        s = a_ref[0]
        for d in range(1, N_DEV):
            s = s + a_ref[d]
        o_ref[...] = s

    return pl.pallas_call(body, name=name, out_shape=SDS((r, n), F32), in_specs=[VMEM_WHOLE],
                          out_specs=VMEM_WHOLE)(allv)


def _adamw(w, g, m, v, name):
    shape = w.shape
    C = shape[-1]
    R = w.size // C
    args = [a.reshape(R, C) for a in (w, g, m, v)]
    br = _tile(R, max(8, (1 << 18) // C), 8)

    def body(w_ref, g_ref, m_ref, v_ref, d_ref, nm_ref, nv_ref):
        g_ = g_ref[...]
        m_ = ADAM_B1 * m_ref[...] + (1.0 - ADAM_B1) * g_
        v_ = ADAM_B2 * v_ref[...] + (1.0 - ADAM_B2) * (g_ * g_)
        m_hat = m_ / (1.0 - ADAM_B1 ** ADAM_STEP)
        v_hat = v_ / (1.0 - ADAM_B2 ** ADAM_STEP)
        d_ref[...] = -ADAM_LR * (m_hat / (jnp.sqrt(v_hat) + ADAM_EPS) + ADAM_WD * w_ref[...])
        nm_ref[...] = m_
        nv_ref[...] = v_

    spec = pl.BlockSpec((br, C), lambda i: (i, 0))
    outs = pl.pallas_call(
        body, name=name, grid=(R // br,), in_specs=[spec] * 4, out_specs=[spec] * 3,
        out_shape=[SDS((R, C), F32)] * 3, compiler_params=_cp("parallel"),
    )(*args)
    return [o.reshape(shape) for o in outs]


def _mod_fwd(c_rows, w_mod, b_cols, name):
    L, D, n = w_mod.shape
    bn = _tile(n, 512, 128)

    def body(c_ref, w_ref, b_ref, o_ref):
        cc = c_ref[...]
        sc = (cc * _sigmoid(cc)).astype(BF16)
        o_ref[...] = _dot(sc, w_ref[...].astype(BF16)) + b_ref[...]

    return pl.pallas_call(
        body, name=name, grid=(L, n // bn),
        in_specs=[pl.BlockSpec((16, D), lambda l, j: (0, 0)),
                  pl.BlockSpec((None, D, bn), lambda l, j: (l, 0, j)),
                  pl.BlockSpec((None, 1, bn), lambda l, j: (l, 0, j))],
        out_specs=pl.BlockSpec((None, 16, bn), lambda l, j: (l, 0, j)),
        out_shape=SDS((L, 16, n), F32), compiler_params=_cp("parallel", "parallel"),
    )(c_rows, w_mod, b_cols)


def _mod_wgrad(c_cols, dmod, name):
    D = c_cols.shape[0]
    L, _, n = dmod.shape
    bd = _tile(D, 512, 8)
    bn = _tile(n, 512, 128)

    def body(c_ref, d_ref, o_ref):
        cc = c_ref[...]
        sc = cc * _sigmoid(cc)
        dm = d_ref[...]
        acc = sc[:, 0:1] * dm[0:1, :]
        for b in range(1, N_DEV):
            acc = acc + sc[:, b:b + 1] * dm[b:b + 1, :]
        o_ref[...] = acc

    return pl.pallas_call(
        body, name=name, grid=(L, D // bd, n // bn),
        in_specs=[pl.BlockSpec((bd, N_DEV), lambda l, i, j: (i, 0)),
                  pl.BlockSpec((None, N_DEV, bn), lambda l, i, j: (l, 0, j))],
        out_specs=pl.BlockSpec((None, bd, bn), lambda l, i, j: (l, i, j)),
        out_shape=SDS((L, D, n), F32), compiler_params=_cp("parallel", "parallel", "parallel"),
    )(c_cols, dmod)


def _mm_in(x, w4, bn, name, gate=None, comm=None):
    M, K = x.shape
    nsh, _, n = w4.shape
    N = nsh * n
    nb = n // bn
    bm = _tile(M, 512, 16)
    x_spec = pl.BlockSpec((bm, K), lambda j, i: (i, 0))
    w_spec = pl.BlockSpec((None, K, bn), lambda j, i: (j // nb, 0, j % nb))
    o_spec = pl.BlockSpec((bm, bn), lambda j, i: (i, j))
    if gate is None:
        def body(x_ref, w_ref, o_ref):
            o_ref[...] = _dot(x_ref[...], w_ref[...]).astype(BF16)

        return _launch(body, name, (N // bn, M // bm), [x_spec, w_spec], o_spec, SDS((M, N), BF16), (x, w4),
                       ("parallel", "parallel"), comm=comm)

    def body_gated(x_ref, w_ref, g_ref, up_ref, a_ref):
        up = _dot(x_ref[...], w_ref[...])
        g = g_ref[...].astype(F32)
        up_ref[...] = up.astype(BF16)
        a_ref[...] = (g * _sigmoid(g) * up).astype(BF16)

    return _launch(body_gated, name, (N // bn, M // bm), [x_spec, w_spec, o_spec], [o_spec, o_spec],
                   [SDS((M, N), BF16)] * 2, (x, w4, gate), ("parallel", "parallel"), comm=comm)


def _mm_out_res(a, w, h, cvec, name, groups=False, comm=None):
    M = a.shape[0]
    N = h.shape[1]
    if groups:
        bn = w.shape[2]
        a_spec = pl.BlockSpec((_tile(M, 512, 16), w.shape[1]), lambda j, i: (i, j))
        w_spec = pl.BlockSpec((None, w.shape[1], bn), lambda j, i: (j, 0, 0))
    else:
        bn = _tile(N, 512, 128)
        a_spec = pl.BlockSpec((_tile(M, 512, 16), a.shape[1]), lambda j, i: (i, 0))
        w_spec = pl.BlockSpec((a.shape[1], bn), lambda j, i: (0, j))
    bm = _tile(M, 512, 16)
    o_spec = pl.BlockSpec((bm, bn), lambda j, i: (i, j))

    def body(a_ref, w_ref, h_ref, c_ref, hn_ref, y_ref):
        y = _dot(a_ref[...], w_ref[...])
        hn_ref[...] = h_ref[...] + c_ref[...] * y
        y_ref[...] = y.astype(BF16)

    return _launch(body, name, (N // bn, M // bm),
                   [a_spec, w_spec, o_spec, pl.BlockSpec((1, bn), lambda j, i: (0, j))], [o_spec, o_spec],
                   [SDS((M, N), F32), SDS((M, N), BF16)], (a, w, h, cvec), ("parallel", "parallel"), comm=comm)


def _mm_nt(dy, w, name, groups=False, swiglu=None, comm=None):
    M = dy.shape[0]
    bm = _tile(M, 1024, 16)
    if groups:
        N = dy.shape[1]
        bn = w.shape[1]
        dy_spec = pl.BlockSpec((bm, w.shape[2]), lambda j, i: (i, j))
        w_spec = pl.BlockSpec((None, bn, w.shape[2]), lambda j, i: (j, 0, 0))
    else:
        N = w.shape[0]
        bn = _tile(N, 512, 128)
        dy_spec = pl.BlockSpec((bm, dy.shape[1]), lambda j, i: (i, 0))
        w_spec = pl.BlockSpec((bn, w.shape[1]), lambda j, i: (j, 0))
    o_spec = pl.BlockSpec((bm, bn), lambda j, i: (i, j))
    if swiglu is None:
        def body(dy_ref, w_ref, o_ref):
            o_ref[...] = _dot_nt(dy_ref[...], w_ref[...]).astype(BF16)

        return _launch(body, name, (N // bn, M // bm), [dy_spec, w_spec], o_spec, SDS((M, N), BF16), (dy, w),
                       ("parallel", "parallel"), comm=comm)

    def body_swiglu(dy_ref, w_ref, g_ref, u_ref, dg_ref, du_ref):
        da = _dot_nt(dy_ref[...], w_ref[...])
        g = g_ref[...].astype(F32)
        sg = _sigmoid(g)
        silu = g * sg
        dg_ref[...] = (da * u_ref[...].astype(F32) * (sg + silu * (1.0 - sg))).astype(BF16)
        du_ref[...] = (da * silu).astype(BF16)

    return _launch(body_swiglu, name, (N // bn, M // bm), [dy_spec, w_spec, o_spec, o_spec], [o_spec, o_spec],
                   [SDS((M, N), BF16)] * 2, (dy, w, *swiglu), ("parallel", "parallel"), comm=comm)


def _mm_nt_acc(dx, w4, name, add=None):
    M = dx.shape[0]
    nc, K, n = w4.shape
    bm = _tile(M, 512, 16)
    with_add = add is not None

    def body(*refs):
        dx_ref, w_ref = refs[:2]
        o_ref, acc_ref = refs[-2:]
        c = pl.program_id(1)
        s = _dot_nt(dx_ref[...], w_ref[...])

        @pl.when(c == 0)
        def _():
            acc_ref[...] = s + refs[2][...].astype(F32) if with_add else s

        @pl.when(c > 0)
        def _():
            acc_ref[...] += s

        @pl.when(c == nc - 1)
        def _():
            o_ref[...] = acc_ref[...].astype(BF16)

    o_spec = pl.BlockSpec((bm, K), lambda i, c: (i, 0))
    in_specs = [pl.BlockSpec((bm, n), lambda i, c: (i, c)), pl.BlockSpec((None, K, n), lambda i, c: (c, 0, 0))]
    args = [dx, w4]
    if with_add:
        in_specs.append(o_spec)
        args.append(add)
    return pl.pallas_call(
        body, name=name, grid=(M // bm, nc), in_specs=in_specs, out_specs=o_spec, out_shape=SDS((M, K), BF16),
        scratch_shapes=[pltpu.VMEM((bm, K), F32)], compiler_params=_cp("parallel", "arbitrary"))(*args)


def _mm_tn(x, dy, name, shard_cols=None, groups=None, comm=None):
    M, K = x.shape
    N = dy.shape[1]
    bm = _tile(M, 1024, 16)
    if groups is not None:
        kg, ng = K // groups, N // groups
        grid = (groups, 1, M // bm)
        x_spec = pl.BlockSpec((bm, kg), lambda i, j, s: (s, i))
        dy_spec = pl.BlockSpec((bm, ng), lambda i, j, s: (s, i))
        o_spec = pl.BlockSpec((None, kg, ng), lambda i, j, s: (i, 0, 0))
        out_shape = SDS((groups, kg, ng), F32)
    else:
        bko = _tile(K, 1408, 128)
        if shard_cols is not None:
            bn = _tile(shard_cols, 1536, 128)
            nb = shard_cols // bn
            o_spec = pl.BlockSpec((None, bko, bn), lambda i, j, s: (j // nb, i, j % nb))
            out_shape = SDS((N_CHIPS, K, shard_cols), F32)
        else:
            bn = _tile(N, 1024, 128)
            o_spec = pl.BlockSpec((bko, bn), lambda i, j, s: (i, j))
            out_shape = SDS((K, N), F32)
        grid = (K // bko, N // bn, M // bm)
        x_spec = pl.BlockSpec((bm, bko), lambda i, j, s: (s, i))
        dy_spec = pl.BlockSpec((bm, bn), lambda i, j, s: (s, j))

    def body(x_ref, dy_ref, o_ref):
        p = _dot_tn(x_ref[...], dy_ref[...])

        @pl.when(pl.program_id(2) == 0)
        def _():
            o_ref[...] = p

        @pl.when(pl.program_id(2) > 0)
        def _():
            o_ref[...] += p

    return _launch(body, name, grid, [x_spec, dy_spec], o_spec, out_shape, (x, dy),
                   ("parallel", "parallel", "arbitrary"), comm=comm)


def _norm_mod_rows(h, g, scale, shift):
    r = lax.rsqrt(jnp.mean(h * h, axis=-1, keepdims=True) + RMS_EPS)
    return (h * r) * g * (1.0 + scale) + shift


def _vec_spec(D):
    return pl.BlockSpec((1, D), lambda i: (0, 0))


def _norm_mod(h, g, scale, shift, name):
    S, D = h.shape
    bs = _tile(S, 512, 16)

    def body(h_ref, g_ref, sc_ref, sh_ref, u_ref):
        u_ref[...] = _norm_mod_rows(h_ref[...], g_ref[...], sc_ref[...], sh_ref[...]).astype(BF16)

    row = pl.BlockSpec((bs, D), lambda i: (i, 0))
    return pl.pallas_call(
        body, name=name, grid=(S // bs,), in_specs=[row, _vec_spec(D), _vec_spec(D), _vec_spec(D)],
        out_specs=row, out_shape=SDS((S, D), BF16), compiler_params=_cp("parallel"))(h, g, scale, shift)


def _band(rows, cols, lo, hi):
    d = lax.broadcasted_iota(jnp.int32, (rows, cols), 1) - lax.broadcasted_iota(jnp.int32, (rows, cols), 0)
    return jnp.where((d >= lo) & (d < hi), 1.0, 0.0).astype(BF16)


def _band_apply(band, x):
    hi, lo = _split_bf16(x)
    return _dot(band, hi) + _dot(band, lo)


def _pool_pre(h, g, scale, shift, name):
    S, D = h.shape
    ng = len(POOL_WINDOWS)
    pg = D // ng
    bs = _tile(S, 256, POOL_HALO)
    hb = bs // POOL_HALO

    def body(h_ref, hh_ref, g_ref, sc_ref, sh_ref, o_ref):
        i = pl.program_id(0)
        u = _norm_mod_rows(h_ref[...], g_ref[...], sc_ref[...], sh_ref[...])
        uh = _norm_mod_rows(hh_ref[...], g_ref[...], sc_ref[...], sh_ref[...])
        uh = jnp.where(i == 0, 0.0, uh)
        ue = jnp.concatenate([uh, u], axis=0)
        t = i * bs + lax.broadcasted_iota(jnp.int32, (bs, 1), 0)
        for gi, w in enumerate(POOL_WINDOWS):
            cols = slice(gi * pg, (gi + 1) * pg)
            band = _band(bs, bs + POOL_HALO, POOL_HALO - w + 1, POOL_HALO + 1)
            inv = 1.0 / jnp.minimum(t + 1, w).astype(F32)
            o_ref[:, cols] = (_band_apply(band, ue[:, cols]) * inv - u[:, cols]).astype(BF16)

    row = pl.BlockSpec((bs, D), lambda i: (i, 0))
    halo = pl.BlockSpec((POOL_HALO, D), lambda i: (jnp.maximum(i * hb - 1, 0), 0))
    return pl.pallas_call(
        body, name=name, grid=(S // bs,),
        in_specs=[row, halo, _vec_spec(D), _vec_spec(D), _vec_spec(D)],
        out_specs=row, out_shape=SDS((S, D), BF16), compiler_params=_cp("parallel"))(h, h, g, scale, shift)


def _pool_post(dd, name):
    S, D = dd.shape
    ng = len(POOL_WINDOWS)
    pg = D // ng
    bs = _tile(S, 256, POOL_HALO)
    hb = bs // POOL_HALO
    nblk = S // bs

    def body(d_ref, dn_ref, o_ref):
        i = pl.program_id(0)
        d = d_ref[...].astype(F32)
        dn = jnp.where(i == nblk - 1, 0.0, dn_ref[...].astype(F32))
        de = jnp.concatenate([d, dn], axis=0)
        t = i * bs + lax.broadcasted_iota(jnp.int32, (bs + POOL_HALO, 1), 0)
        for gi, w in enumerate(POOL_WINDOWS):
            cols = slice(gi * pg, (gi + 1) * pg)
            inv = 1.0 / jnp.minimum(t + 1, w).astype(F32)
            band = _band(bs, bs + POOL_HALO, 0, w)
            o_ref[:, cols] = (_band_apply(band, de[:, cols] * inv) - d[:, cols]).astype(BF16)

    row = pl.BlockSpec((bs, D), lambda i: (i, 0))
    nxt = pl.BlockSpec((POOL_HALO, D), lambda i: (jnp.minimum((i + 1) * hb, S // POOL_HALO - 1), 0))
    return pl.pallas_call(
        body, name=name, grid=(nblk,), in_specs=[row, nxt], out_specs=row, out_shape=SDS((S, D), BF16),
        compiler_params=_cp("parallel"))(dd, dd)


def _colsum(x):
    return jnp.sum(x, axis=0, keepdims=True)


def _accumulate_rows(st_ref, rows, first):
    @pl.when(first)
    def _():
        st_ref[...] = jnp.zeros_like(st_ref)

    for r, row in enumerate(rows):
        st_ref[r:r + 1, :] += row


def _norm_bwd(h, g, scale, du, dh_out, name, prev=None):
    S, D = h.shape
    bs = _tile(S, 256, 16)
    with_prev = prev is not None

    def body(*refs):
        h_ref, g_ref, sc_ref, du_ref, dho_ref = refs[:5]
        if with_prev:
            y_ref, cv_ref, dh_ref, dy_ref, st_ref = refs[5:]
        else:
            dh_ref, st_ref = refs[5:]
        hh = h_ref[...]
        du_ = du_ref[...].astype(F32)
        r = lax.rsqrt(jnp.mean(hh * hh, axis=-1, keepdims=True) + RMS_EPS)
        xhat = hh * r
        dn = du_ * (1.0 + sc_ref[...])
        dxhat = dn * g_ref[...]
        dh = dho_ref[...] + r * (dxhat - xhat * jnp.mean(dxhat * xhat, axis=-1, keepdims=True))
        dh_ref[...] = dh
        rows = [_colsum(du_), _colsum(du_ * (xhat * g_ref[...])), _colsum(dn * xhat)]
        if with_prev:
            dy_ref[...] = (dh * cv_ref[...]).astype(BF16)
            rows.append(_colsum(dh * y_ref[...].astype(F32)))
        _accumulate_rows(st_ref, rows, pl.program_id(0) == 0)

    row = pl.BlockSpec((bs, D), lambda i: (i, 0))
    st_spec = pl.BlockSpec((8, D), lambda i: (0, 0))
    in_specs = [row, _vec_spec(D), _vec_spec(D), row, row]
    args = [h, g, scale, du, dh_out]
    out_specs, out_shape = [row], [SDS((S, D), F32)]
    if with_prev:
        in_specs += [row, _vec_spec(D)]
        args += list(prev)
        out_specs.append(row)
        out_shape.append(SDS((S, D), BF16))
    out_specs.append(st_spec)
    out_shape.append(SDS((8, D), F32))
    return pl.pallas_call(
        body, name=name, grid=(S // bs,), in_specs=in_specs, out_specs=out_specs, out_shape=out_shape,
        compiler_params=_cp("arbitrary"))(*args)


def _loss_head(h, g, target, y, cvec, name):
    S, D = h.shape
    bs = _tile(S, 256, 16)

    def body(h_ref, g_ref, t_ref, y_ref, cv_ref, dh_ref, dy_ref, st_ref):
        hh = h_ref[...]
        r = lax.rsqrt(jnp.mean(hh * hh, axis=-1, keepdims=True) + RMS_EPS)
        xhat = hh * r
        err = xhat * g_ref[...] - t_ref[...]
        dout = err * (1.0 / D)
        dxhat = dout * g_ref[...]
        dh = r * (dxhat - xhat * jnp.mean(dxhat * xhat, axis=-1, keepdims=True))
        dh_ref[...] = dh
        dy_ref[...] = (dh * cv_ref[...]).astype(BF16)
        rows = [_colsum(dout * xhat), _colsum(dh * y_ref[...].astype(F32)), _colsum(err * err) * (0.5 / D)]
        _accumulate_rows(st_ref, rows, pl.program_id(0) == 0)

    row = pl.BlockSpec((bs, D), lambda i: (i, 0))
    return pl.pallas_call(
        body, name=name, grid=(S // bs,), in_specs=[row, _vec_spec(D), row, row, _vec_spec(D)],
        out_specs=[row, row, pl.BlockSpec((8, D), lambda i: (0, 0))],
        out_shape=[SDS((S, D), F32), SDS((S, D), BF16), SDS((8, D), F32)],
        compiler_params=_cp("arbitrary"))(h, g, target, y, cvec)


def _sum_all(x, name):
    def body(x_ref, o_ref):
        o_ref[...] = jnp.sum(jnp.sum(x_ref[...], axis=1, keepdims=True), axis=0, keepdims=True)

    return pl.pallas_call(body, name=name, out_shape=SDS((1, 1), F32), in_specs=[VMEM_WHOLE],
                          out_specs=VMEM_WHOLE)(x)


def _conv_mid(u3, cw, name):
    S, D3 = u3.shape
    D = D3 // 3
    cb = _tile(D, 512, 128)
    nj = D // cb
    bs = _tile(S, 256, CONV_HALO)
    hb = bs // CONV_HALO

    def body(b_ref, c_ref, v_ref, ch_ref, vh_ref, w_ref, o_ref):
        i = pl.program_id(0)
        z = c_ref[...].astype(F32) * v_ref[...].astype(F32)
        zh = jnp.where(i == 0, 0.0, ch_ref[...].astype(F32) * vh_ref[...].astype(F32))
        ze = jnp.concatenate([zh, z], axis=0)
        w = w_ref[...]
        zc = w[2:3] * z
        zc = zc + w[1:2] * _band_apply(_band(bs, bs + CONV_HALO, CONV_HALO - 1, CONV_HALO), ze)
        zc = zc + w[0:1] * _band_apply(_band(bs, bs + CONV_HALO, CONV_HALO - 2, CONV_HALO - 1), ze)
        o_ref[...] = (b_ref[...].astype(F32) * zc).astype(BF16)

    def blk(off):
        return pl.BlockSpec((bs, cb), lambda i, j: (i, off + j))

    def halo(off):
        return pl.BlockSpec((CONV_HALO, cb), lambda i, j: (jnp.maximum(i * hb - 1, 0), off + j))

    return pl.pallas_call(
        body, name=name, grid=(S // bs, nj),
        in_specs=[blk(0), blk(nj), blk(2 * nj), halo(nj), halo(2 * nj), pl.BlockSpec((3, cb), lambda i, j: (0, j))],
        out_specs=pl.BlockSpec((bs, cb), lambda i, j: (i, j)), out_shape=SDS((S, D), BF16),
        compiler_params=_cp("parallel", "parallel"))(u3, u3, u3, u3, u3, cw)


def _conv_mid_bwd(u3, da, cw, name):
    S, D3 = u3.shape
    D = D3 // 3
    cb = _tile(D, 512, 128)
    nj = D // cb
    bs = _tile(S, 256, CONV_HALO)
    hb = bs // CONV_HALO
    nblk = S // bs
    last_halo = S // CONV_HALO - 1

    def body(b_ref, c_ref, v_ref, ch_ref, vh_ref, bn_ref, da_ref, dan_ref, w_ref, db_ref, dc_ref, dv_ref, dw_ref):
        i = pl.program_id(0)
        c = c_ref[...].astype(F32)
        v = v_ref[...].astype(F32)
        b = b_ref[...].astype(F32)
        da_ = da_ref[...].astype(F32)
        z = c * v
        zh = jnp.where(i == 0, 0.0, ch_ref[...].astype(F32) * vh_ref[...].astype(F32))
        ze = jnp.concatenate([zh, z], axis=0)
        z1 = _band_apply(_band(bs, bs + CONV_HALO, CONV_HALO - 1, CONV_HALO), ze)
        z2 = _band_apply(_band(bs, bs + CONV_HALO, CONV_HALO - 2, CONV_HALO - 1), ze)
        w = w_ref[...]
        zc = w[2:3] * z + w[1:2] * z1 + w[0:1] * z2
        db_ref[...] = (da_ * zc).astype(BF16)
        dzc = da_ * b
        dzn = jnp.where(i == nblk - 1, 0.0, dan_ref[...].astype(F32) * bn_ref[...].astype(F32))
        dze = jnp.concatenate([dzc, dzn], axis=0)
        dz = w[2:3] * dzc
        dz = dz + w[1:2] * _band_apply(_band(bs, bs + CONV_HALO, 1, 2), dze)
        dz = dz + w[0:1] * _band_apply(_band(bs, bs + CONV_HALO, 2, 3), dze)
        dc_ref[...] = (dz * v).astype(BF16)
        dv_ref[...] = (dz * c).astype(BF16)
        dw_ref[...] = jnp.zeros_like(dw_ref)
        dw_ref[0:1, :] = _colsum(dzc * z2)
        dw_ref[1:2, :] = _colsum(dzc * z1)
        dw_ref[2:3, :] = _colsum(dzc * z)

    def blk(off):
        return pl.BlockSpec((bs, cb), lambda i, j: (i, off + j))

    def halo(off):
        return pl.BlockSpec((CONV_HALO, cb), lambda i, j: (jnp.maximum(i * hb - 1, 0), off + j))

    def nxt(off):
        return pl.BlockSpec((CONV_HALO, cb), lambda i, j: (jnp.minimum((i + 1) * hb, last_halo), off + j))

    o_spec = pl.BlockSpec((bs, cb), lambda i, j: (i, j))
    return pl.pallas_call(
        body, name=name, grid=(nblk, nj),
        in_specs=[blk(0), blk(nj), blk(2 * nj), halo(nj), halo(2 * nj), nxt(0), o_spec, nxt(0),
                  pl.BlockSpec((3, cb), lambda i, j: (0, j))],
        out_specs=[o_spec, o_spec, o_spec, pl.BlockSpec((None, 8, cb), lambda i, j: (i, 0, j))],
        out_shape=[SDS((S, D), BF16)] * 3 + [SDS((nblk, 8, D), F32)],
        compiler_params=_cp("parallel", "parallel"))(u3, u3, u3, u3, u3, u3, da, da, cw)


def _sum_lead(x, name):
    n, r, C = x.shape

    def body(x_ref, o_ref):
        @pl.when(pl.program_id(0) == 0)
        def _():
            o_ref[...] = x_ref[...]

        @pl.when(pl.program_id(0) > 0)
        def _():
            o_ref[...] += x_ref[...]

    return pl.pallas_call(
        body, name=name, grid=(n,), in_specs=[pl.BlockSpec((None, r, C), lambda i: (i, 0, 0))],
        out_specs=pl.BlockSpec((r, C), lambda i: (0, 0)), out_shape=SDS((r, C), F32),
        compiler_params=_cp("arbitrary"))(x)


def _log_sigmoids(z):
    lb = jnp.minimum(z, 0.0) - jnp.log(1.0 + jnp.exp(-jnp.abs(z)))
    return lb, lb - z


def _attn_blocks(S):
    bk = _tile(S, 256, 128)
    bq = 2 * bk if S % (2 * bk) == 0 else bk
    return bq, bk


def _tri(n, pred):
    rowi = lax.broadcasted_iota(jnp.int32, (n, n), 0)
    coli = lax.broadcasted_iota(jnp.int32, (n, n), 1)
    return jnp.where(pred(rowi, coli), 1.0, 0.0).astype(BF16)


def _causal_mask(bq, bk, m):
    rowi = lax.broadcasted_iota(jnp.int32, (bq, bk), 0)
    coli = lax.broadcasted_iota(jnp.int32, (bq, bk), 1)
    return m * bk + coli < rowi


def _sb_attention(qkv, name):
    S, D3 = qkv.shape
    D = D3 // 3
    H = D // HEAD_DIM
    bq, bk = _attn_blocks(S)
    nq, r = S // bq, bq // bk
    unroll = r
    scale = HEAD_DIM ** -0.5

    def body(q_ref, k_ref, v_ref, o_ref, lt_ref):
        i = pl.program_id(1)
        q = q_ref[...]
        after = _tri(bk, lambda j, s: j > s)

        def block(kb, carry, acc, causal):
            rows = pl.ds(pl.multiple_of(kb * bk, bk), bk)
            z = _dot_nt(q, k_ref[rows, :]) * scale
            lb, l1 = _log_sigmoids(z)
            if causal is not None:
                l1 = jnp.where(causal, l1, 0.0)
            a = jnp.exp(lb + (_dot(l1.astype(BF16), after) + carry))
            if causal is not None:
                a = jnp.where(causal, a, 0.0)
            acc = acc + _dot(a.astype(BF16), v_ref[rows, :])
            return carry + jnp.sum(l1, axis=1, keepdims=True), acc

        carry, acc = jnp.zeros((bq, 1), F32), jnp.zeros((bq, HEAD_DIM), F32)
        for m in reversed(range(r)):
            carry, acc = block(i * r + m, carry, acc, _causal_mask(bq, bk, m))

        def step(j, ca):
            for n in range(unroll):
                ca = block(i * r - 1 - unroll * j - n, ca[0], ca[1], None)
            return ca

        carry, acc = lax.fori_loop(0, i * (r // unroll), step, (carry, acc))
        o_ref[...] = acc.astype(BF16)
        lt_ref[...] = jnp.broadcast_to(carry, (bq, HEAD_DIM))

    head_rows = lambda off: pl.BlockSpec((S, HEAD_DIM), lambda hd, i: (0, off + hd))
    blk = pl.BlockSpec((bq, HEAD_DIM), lambda hd, i: (i, hd))
    return pl.pallas_call(
        body, name=name, grid=(H, nq), in_specs=[blk, head_rows(H), head_rows(2 * H)],
        out_specs=[blk, blk], out_shape=[SDS((S, D), BF16), SDS((S, D), F32)],
        compiler_params=_cp("parallel", "arbitrary"))(qkv, qkv, qkv)


def _sb_attention_bwd(qkv, ltot, do, name):
    S, D3 = qkv.shape
    D = D3 // 3
    H = D // HEAD_DIM
    bq, bk = _attn_blocks(S)
    nq, r = S // bq, bq // bk
    unroll = r
    scale = HEAD_DIM ** -0.5

    def body(q_ref, k_ref, v_ref, lt_ref, do_ref, dq_ref, dk_ref, dv_ref, dkt_acc, dvt_acc):
        i = pl.program_id(1)
        q = q_ref[...]
        do_ = do_ref[...]
        qt = jnp.transpose(q.astype(F32)).astype(BF16)
        dot = jnp.transpose(do_.astype(F32)).astype(BF16)
        lt = lt_ref[:, 0:1]
        after = _tri(bk, lambda j, s: j > s)
        before = _tri(bk, lambda j, s: j < s)

        @pl.when(i == 0)
        def _():
            dkt_acc[...] = jnp.zeros_like(dkt_acc)
            dvt_acc[...] = jnp.zeros_like(dvt_acc)

        def block(kb, c1, ce, dq, causal):
            rows = pl.ds(pl.multiple_of(kb * bk, bk), bk)
            k = k_ref[rows, :]
            v = v_ref[rows, :]
            z = _dot_nt(q, k) * scale
            lb, l1 = _log_sigmoids(z)
            sig = jnp.exp(lb)
            if causal is not None:
                l1 = jnp.where(causal, l1, 0.0)
            c1 = c1 + jnp.sum(l1, axis=1, keepdims=True)
            a = jnp.exp(lb + (_dot(l1.astype(BF16), after) + (lt - c1)))
            if causal is not None:
                a = jnp.where(causal, a, 0.0)
            e = a * _dot_nt(do_, v)
            p = _dot(e.astype(BF16), before) + ce
            dz = e - sig * (e + p)
            if causal is not None:
                dz = jnp.where(causal, dz, 0.0)
            dzb = dz.astype(BF16)
            dkt_acc[kb] += _dot(qt, dzb)
            dvt_acc[kb] += _dot(dot, a.astype(BF16))
            dq = dq + _dot(dzb, k)
            return c1, ce + jnp.sum(e, axis=1, keepdims=True), dq

        def step(j, st):
            for n in range(unroll):
                st = block(unroll * j + n, st[0], st[1], st[2], None)
            return st

        zero = jnp.zeros((bq, 1), F32)
        c1, ce, dq = lax.fori_loop(0, i * (r // unroll), step, (zero, zero, jnp.zeros((bq, HEAD_DIM), F32)))
        for m in range(r):
            c1, ce, dq = block(i * r + m, c1, ce, dq, _causal_mask(bq, bk, m))
        dq_ref[...] = (dq * scale).astype(BF16)

        @pl.when(i == nq - 1)
        def _():
            def flush(kb, _):
                rows = pl.ds(pl.multiple_of(kb * bk, bk), bk)
                dk_ref[rows, :] = (jnp.transpose(dkt_acc[kb]) * scale).astype(BF16)
                dv_ref[rows, :] = jnp.transpose(dvt_acc[kb]).astype(BF16)
                return 0

            lax.fori_loop(0, S // bk, flush, 0)

    head_rows = lambda off: pl.BlockSpec((S, HEAD_DIM), lambda hd, i: (0, off + hd))
    blk = pl.BlockSpec((bq, HEAD_DIM), lambda hd, i: (i, hd))
    return pl.pallas_call(
        body, name=name, grid=(H, nq), in_specs=[blk, head_rows(H), head_rows(2 * H), blk, blk],
        out_specs=[blk, head_rows(0), head_rows(0)], out_shape=[SDS((S, D), BF16)] * 3,
        scratch_shapes=[pltpu.VMEM((S // bk, HEAD_DIM, bk), F32), pltpu.VMEM((S // bk, HEAD_DIM, bk), F32)],
        compiler_params=_cp("arbitrary", "arbitrary"))(qkv, qkv, qkv, ltot, do)


def kernel(x, c, norm_mix_g, norm_ffn_g, w_mod, b_mod, pool_w, pool_scale, conv_w_in, conv_w, conv_w_out, sb_w_qkv, sb_w_o, ffn_w_gate, ffn_w_up, ffn_w_down, final_g, loss_target, m_norm_mix_g, m_norm_ffn_g, m_w_mod, m_b_mod, m_pool_w, m_pool_scale, m_conv_w_in, m_conv_w, m_conv_w_out, m_sb_w_qkv, m_sb_w_o, m_ffn_w_gate, m_ffn_w_up, m_ffn_w_down, m_final_g, v_norm_mix_g, v_norm_ffn_g, v_w_mod, v_b_mod, v_pool_w, v_pool_scale, v_conv_w_in, v_conv_w, v_conv_w_out, v_sb_w_qkv, v_sb_w_o, v_ffn_w_gate, v_ffn_w_up, v_ffn_w_down, v_final_g):
    S, D = x.shape[1], x.shape[2]
    L = norm_mix_g.shape[0]
    nmod = w_mod.shape[2]
    nf = ffn_w_gate.shape[2]
    n3 = conv_w_in.shape[2]
    nd = conv_w_out.shape[1]
    cb = n3 // 3
    ng = pool_w.shape[1]
    pg = pool_w.shape[3]
    n_pool = pool_w.shape[0]
    assert D % HEAD_DIM == 0 and S % 256 == 0 and nd == cb and N_CHIPS * nd == D and pg * ng == D

    mx, my, mc = lax.axis_index("x"), lax.axis_index("y"), lax.axis_index("c")
    chip = 2 * mx + my
    dev = 2 * chip + mc
    hx, ht = x[0], loss_target[0]

    c_all = _allgather8(jnp.broadcast_to(c, (8, D)), "gather_c").reshape(N_DEV, 8, D)[:, 0]
    c_rows = jnp.concatenate([c_all, jnp.zeros((8, D), F32)], axis=0)
    b_cols = lax.dynamic_slice_in_dim(b_mod, chip * nmod, nmod, axis=1).reshape(L, 1, nmod)
    mod_cols = _mod_fwd(c_rows, w_mod, b_cols, "mod_fwd")
    mod_all = _allgather8(mod_cols.reshape(L * 16, nmod), "gather_mod").reshape(N_CHIPS, 2, L, 16, nmod)
    mod = lax.dynamic_index_in_dim(mod_all[:, 0], dev, axis=2, keepdims=False)
    mod = jnp.transpose(mod, (1, 0, 2)).reshape(L, N_MOD, 1, D)

    bf = lambda w: w.astype(BF16)
    n_conv, n_sb = conv_w_in.shape[0], sb_w_qkv.shape[0]

    def mixer_shards(l):
        kind, j = l % 3, l // 3
        if kind == 0:
            return [bf(pool_w[j]).reshape(ng * (pg // N_CHIPS), pg)]
        return [bf(conv_w_in[j]), bf(conv_w_out[j])] if kind == 1 else [bf(sb_w_qkv[j]), bf(sb_w_o[j])]

    def mixer_weights(l, got):
        if l % 3 == 0:
            return [jnp.transpose(got[0].reshape(N_CHIPS, ng, pg // N_CHIPS, pg), (1, 0, 2, 3)).reshape(ng, pg, pg)]
        return [got[0], got[1].reshape(D, D)]

    first = mixer_shards(0)
    got = _run_comm(_GatherShards(first + [bf(ffn_w_gate[0]), bf(ffn_w_up[0]), bf(ffn_w_down[0])]), "gather_layer0")
    w_mix = {0: mixer_weights(0, got[:len(first)])}
    w_gate, w_up = {0: got[-3]}, {0: got[-2]}
    w_down = {0: got[-1].reshape(N_CHIPS * nf, D)}
    taps_cols = jnp.concatenate([pool_scale, conv_w.reshape(-1, nd)], axis=0)
    n_small = taps_cols.shape[0]
    small_rows = jnp.concatenate([taps_cols, jnp.zeros((16 - n_small, nd), F32)], axis=0)
    small_all = _allgather8(small_rows, "gather_small").reshape(N_CHIPS, 2, 16, nd)[:, 0]
    small_full = jnp.transpose(small_all, (1, 0, 2)).reshape(16, D)
    pool_scale_full = small_full[:n_pool]
    conv_taps_full = small_full[n_pool:n_small].reshape(n_conv, 3, D)

    saved = []
    h = hx
    for l in range(L):
        kind, j = l % 3, l // 3
        sh_m, sc_m, gt_m, sh_f, sc_f, gt_f = (mod[l, r] for r in range(N_MOD))
        gm = norm_mix_g[l].reshape(1, D)
        gf = norm_ffn_g[l].reshape(1, D)
        s = {"h_in": h}
        if kind == 0:
            s["diff"] = _pool_pre(h, gm, sc_m, sh_m, f"pool_pre{l}")
            s["cvec_m"] = gt_m * pool_scale_full[j].reshape(1, D)
            (h, s["y_m"]), _ = _mm_out_res(s["diff"], w_mix[l][0], h, s["cvec_m"], f"pool_mm{l}", groups=True)
        elif kind == 1:
            s["u"] = _norm_mod(h, gm, sc_m, sh_m, f"norm_mix{l}")
            s["u3"], _ = _mm_in(s["u"], w_mix[l][0], n3, f"conv_in{l}")
            s["a_m"] = _conv_mid(s["u3"], conv_taps_full[j], f"conv_mid{l}")
            s["cvec_m"] = gt_m
            (h, s["y_m"]), _ = _mm_out_res(s["a_m"], w_mix[l][1], h, gt_m, f"conv_out{l}")
        else:
            s["u"] = _norm_mod(h, gm, sc_m, sh_m, f"norm_mix{l}")
            s["qkv"], _ = _mm_in(s["u"], w_mix[l][0], n3, f"sb_qkv{l}")
            s["o"], s["ltot"] = _sb_attention(s["qkv"], f"sb_attn{l}")
            s["cvec_m"] = gt_m
            (h, s["y_m"]), _ = _mm_out_res(s["o"], w_mix[l][1], h, gt_m, f"sb_out{l}")
        s["h_mid"] = h
        s["u2"] = _norm_mod(h, gf, sc_f, sh_f, f"norm_ffn{l}")
        more = l + 1 < L
        s["gate"], got = _mm_in(s["u2"], w_gate[l], nf, f"ffn_gate{l}",
                                comm=_GatherShards(mixer_shards(l + 1)) if more else None)
        if more:
            w_mix[l + 1] = mixer_weights(l + 1, got)
        (s["up"], s["a_f"]), got = _mm_in(
            s["u2"], w_up[l], nf, f"ffn_up{l}", gate=s["gate"],
            comm=_GatherShards([bf(ffn_w_gate[l + 1]), bf(ffn_w_up[l + 1])]) if more else None)
        if more:
            w_gate[l + 1], w_up[l + 1] = got
        (h, s["y_f"]), got = _mm_out_res(s["a_f"], w_down[l], h, gt_f, f"ffn_down{l}",
                                         comm=_GatherShards([bf(ffn_w_down[l + 1])]) if more else None)
        if more:
            w_down[l + 1] = got[0].reshape(N_CHIPS * nf, D)
        saved.append(s)

    gt_f_last = mod[L - 1, 5]
    dh, dy, st = _loss_head(h, final_g.reshape(1, D), ht, saved[-1]["y_f"], gt_f_last, "loss_head")
    loss = lax.psum(_sum_all(st[2:3], "loss_sum")[0, 0], ("x", "y", "c"))
    d_final_g = st[0:1]
    p_gate_f = st[1:2]
    d_norm_mix, d_norm_ffn = [None] * L, [None] * L
    d_mod = [[None] * N_MOD for _ in range(L)]
    d_pool_scale, d_taps = [None] * n_pool, [None] * n_conv
    big = {}
    ffn_reduce = mix_reduce = None
    for l in reversed(range(L)):
        kind, j = l % 3, l // 3
        s = saved[l]
        sh_m, sc_m, gt_m, sh_f, sc_f, gt_f = (mod[l, r] for r in range(N_MOD))
        gm = norm_mix_g[l].reshape(1, D)
        gf = norm_ffn_g[l].reshape(1, D)
        d_mod[l][5] = p_gate_f
        (dgate, dup), brought = _mm_nt(dy, w_down[l], f"ffn_down_bwd{l}", swiglu=(s["gate"], s["up"]),
                                       comm=ffn_reduce.scatter if ffn_reduce else None)
        if ffn_reduce:
            big[("ffn", l + 1)] = ffn_reduce.finish(brought)
        gw_down, _ = _mm_tn(s["a_f"], dy, f"ffn_down_wgrad{l}")
        gw_gate, brought = _mm_tn(s["u2"], dgate, f"ffn_gate_wgrad{l}", shard_cols=nf,
                                  comm=mix_reduce.scatter if mix_reduce else None)
        if mix_reduce:
            big[("mix", l + 1)] = mix_reduce.finish(brought)
        gw_up, _ = _mm_tn(s["u2"], dup, f"ffn_up_wgrad{l}", shard_cols=nf)
        du2 = _mm_nt_acc(dgate, w_gate[l], f"ffn_gate_bwd{l}")
        du2 = _mm_nt_acc(dup, w_up[l], f"ffn_up_bwd{l}", add=du2)
        ffn_reduce = _Reduce([gw_gate, gw_up, gw_down.reshape(N_CHIPS, nf, D)], f"reduce_ffn{l}")
        mix_reduce = None
        dh, dy, st = _norm_bwd(s["h_mid"], gf, sc_f, du2, dh, f"norm_ffn_bwd{l}", prev=(s["y_m"], s["cvec_m"]))
        d_mod[l][3], d_mod[l][4], d_norm_ffn[l] = st[0:1], st[1:2], st[2:3]
        p_mix = st[3:4]
        if kind == 0:
            d_mod[l][2] = p_mix * pool_scale_full[j].reshape(1, D)
            d_pool_scale[j] = p_mix * gt_m
            dd, _ = _mm_nt(dy, w_mix[l][0], f"pool_mm_bwd{l}", groups=True)
            big[("pool", j)], _ = _mm_tn(s["diff"], dy, f"pool_wgrad{l}", groups=ng)
            du = _pool_post(dd, f"pool_post{l}")
        elif kind == 1:
            d_mod[l][2] = p_mix
            da, _ = _mm_nt(dy, w_mix[l][1], f"conv_out_bwd{l}")
            gw_out, _ = _mm_tn(s["a_m"], dy, f"conv_out_wgrad{l}")
            db, dc, dv, dtap = _conv_mid_bwd(s["u3"], da, conv_taps_full[j], f"conv_mid_bwd{l}")
            d_taps[j] = _sum_lead(dtap, f"conv_tap_sum{l}")[0:3]
            du3 = jnp.concatenate([db, dc, dv], axis=1)
            gw_in, _ = _mm_tn(s["u"], du3, f"conv_in_wgrad{l}", shard_cols=n3)
            du = _mm_nt_acc(du3, w_mix[l][0], f"conv_in_bwd{l}")
            mix_reduce = _Reduce([gw_in, gw_out.reshape(N_CHIPS, nd, D)], f"reduce_conv{l}")
        else:
            d_mod[l][2] = p_mix
            do, _ = _mm_nt(dy, w_mix[l][1], f"sb_out_bwd{l}")
            gw_o, _ = _mm_tn(s["o"], dy, f"sb_out_wgrad{l}")
            dq, dk, dv = _sb_attention_bwd(s["qkv"], s["ltot"], do, f"sb_attn_bwd{l}")
            dqkv = jnp.concatenate([dq, dk, dv], axis=1)
            gw_qkv, _ = _mm_tn(s["u"], dqkv, f"sb_qkv_wgrad{l}", shard_cols=n3)
            du = _mm_nt_acc(dqkv, w_mix[l][0], f"sb_qkv_bwd{l}")
            mix_reduce = _Reduce([gw_qkv, gw_o.reshape(N_CHIPS, nd, D)], f"reduce_sb{l}")
        if l > 0:
            prev = (saved[l - 1]["y_f"], mod[l - 1, 5])
            dh, dy, st = _norm_bwd(s["h_in"], gm, sc_m, du, dh, f"norm_mix_bwd{l}", prev=prev)
            p_gate_f = st[3:4]
        else:
            dh, st = _norm_bwd(s["h_in"], gm, sc_m, du, dh, f"norm_mix_bwd{l}")
        d_mod[l][0], d_mod[l][1], d_norm_mix[l] = st[0:1], st[1:2], st[2:3]
    grad_x = dh.reshape(1, S, D)
    big[("ffn", 0)] = ffn_reduce.finish()
    if mix_reduce:
        big[("mix", 0)] = mix_reduce.finish()

    gw_pool = jnp.stack([big[("pool", j)] for j in range(n_pool)])
    gw_pool = jnp.transpose(gw_pool.reshape(n_pool, ng, N_CHIPS, pg // N_CHIPS, pg), (2, 0, 1, 3, 4))
    (g_pool,) = _Reduce([gw_pool.reshape(N_CHIPS, n_pool * ng * (pg // N_CHIPS), pg)], "reduce_pool").finish()

    rows = [d_final_g] + d_norm_mix + d_norm_ffn + [r for l in range(L) for r in d_mod[l]] + d_pool_scale
    rows += [d_taps[j] for j in range(n_conv)]
    vec = jnp.concatenate(rows, axis=0)
    n_rows = vec.shape[0]
    pad = -n_rows % 8
    vec = jnp.concatenate([vec, jnp.zeros((pad, D), F32)], axis=0) if pad else vec
    vec_all = _allgather8(vec, "gather_small_grads").reshape(N_DEV, n_rows + pad, D)
    tot = _sum_devices(vec_all, "sum_small_grads")
    r0 = 1 + 2 * L
    g_final = tot[0]
    g_norm_mix = tot[1:1 + L]
    g_norm_ffn = tot[1 + L:r0]
    g_b_mod = tot[r0:r0 + N_MOD * L].reshape(L, N_MOD * D)
    r1 = r0 + N_MOD * L
    g_pool_scale = lax.dynamic_slice_in_dim(tot[r1:r1 + n_pool], chip * nd, nd, axis=1)
    g_taps = lax.dynamic_slice_in_dim(tot[r1 + n_pool:r1 + n_pool + 3 * n_conv], chip * nd, nd, axis=1)
    g_conv_w = g_taps.reshape(conv_w.shape)
    dmod_all = vec_all[:, r0:r1].reshape(N_DEV, L, N_MOD * D)
    dmod_cols = jnp.transpose(lax.dynamic_slice_in_dim(dmod_all, chip * nmod, nmod, axis=2), (1, 0, 2))
    g_w_mod = _mod_wgrad(jnp.transpose(c_all), dmod_cols, "mod_wgrad")

    g_ffn_gate = jnp.stack([big[("ffn", l)][0] for l in range(L)])
    g_ffn_up = jnp.stack([big[("ffn", l)][1] for l in range(L)])
    g_ffn_down = jnp.stack([big[("ffn", l)][2] for l in range(L)])
    g_conv_in = jnp.stack([big[("mix", 3 * j + 1)][0] for j in range(n_conv)])
    g_conv_out = jnp.stack([big[("mix", 3 * j + 1)][1] for j in range(n_conv)])
    g_sb_qkv = jnp.stack([big[("mix", 3 * j + 2)][0] for j in range(n_sb)])
    g_sb_o = jnp.stack([big[("mix", 3 * j + 2)][1] for j in range(n_sb)])
    g_pool_w = g_pool.reshape(pool_w.shape)

    grads = [g_norm_mix, g_norm_ffn, g_w_mod, g_b_mod, g_pool_w, g_pool_scale, g_conv_in, g_conv_w, g_conv_out,
             g_sb_qkv, g_sb_o, g_ffn_gate, g_ffn_up, g_ffn_down, g_final]
    weights = [norm_mix_g, norm_ffn_g, w_mod, b_mod, pool_w, pool_scale, conv_w_in, conv_w, conv_w_out,
               sb_w_qkv, sb_w_o, ffn_w_gate, ffn_w_up, ffn_w_down, final_g]
    ms = [m_norm_mix_g, m_norm_ffn_g, m_w_mod, m_b_mod, m_pool_w, m_pool_scale, m_conv_w_in, m_conv_w, m_conv_w_out,
          m_sb_w_qkv, m_sb_w_o, m_ffn_w_gate, m_ffn_w_up, m_ffn_w_down, m_final_g]
    vs = [v_norm_mix_g, v_norm_ffn_g, v_w_mod, v_b_mod, v_pool_w, v_pool_scale, v_conv_w_in, v_conv_w, v_conv_w_out,
          v_sb_w_qkv, v_sb_w_o, v_ffn_w_gate, v_ffn_w_up, v_ffn_w_down, v_final_g]
    deltas, new_ms, new_vs = [], [], []
    for n, (w, g, m, v) in enumerate(zip(weights, grads, ms, vs)):
        if w.ndim == 1:
            w, g, m, v = (a.reshape(1, -1) for a in (w, g, m, v))
        g = g.reshape(w.shape)
        grads[n] = g.reshape(weights[n].shape)
        d, nm, nv = _adamw(w, g, m, v, f"adamw{n}")
        deltas.append(d.reshape(weights[n].shape))
        new_ms.append(nm.reshape(weights[n].shape))
        new_vs.append(nv.reshape(weights[n].shape))
    return (loss, grad_x, *grads, *deltas, *new_ms, *new_vs)
```

```python
import functools

import jax
import jax.numpy as jnp
from jax import lax
from jax.experimental import pallas as pl
from jax.experimental.pallas import tpu as pltpu

F32 = jnp.float32
BF16 = jnp.bfloat16
SDS = jax.ShapeDtypeStruct
MESH = pl.DeviceIdType.MESH

RMS_EPS = 1e-6
POOL_WINDOWS = (2, 4, 8, 16)
POOL_HALO = 16
CONV_HALO = 16
HEAD_DIM = 128
N_MOD = 6
N_CHIPS = 4
N_DEV = 8
ADAM_LR = 0.001
ADAM_B1 = 0.9
ADAM_B2 = 0.999
ADAM_EPS = 1e-08
ADAM_WD = 0.01
ADAM_STEP = 10
VMEM_LIMIT_V7X = 52 * 1024 * 1024
ANY = pl.BlockSpec(memory_space=pl.ANY)
VMEM_WHOLE = pl.BlockSpec(memory_space=pltpu.VMEM)


def _cp(*sem):
    return pltpu.CompilerParams(dimension_semantics=sem, vmem_limit_bytes=VMEM_LIMIT_V7X)


def _tile(n, pref, unit):
    if n <= pref:
        return n
    t = (pref // unit) * unit
    while t >= unit:
        if n % t == 0:
            return t
        t -= unit
    return n


def _dot(a, b):
    return jnp.dot(a, b, preferred_element_type=F32)


def _dot_nt(a, b):
    return lax.dot_general(a, b, (((1,), (1,)), ((), ())), preferred_element_type=F32)


def _dot_tn(a, b):
    return lax.dot_general(a, b, (((0,), (0,)), ((), ())), preferred_element_type=F32)


def _split_bf16(x):
    hi = x.astype(BF16)
    lo = (x - hi.astype(F32)).astype(BF16)
    return hi, lo


def _sigmoid(x):
    return 1.0 / (1.0 + jnp.exp(-x))


def _my_place():
    return lax.axis_index("x"), lax.axis_index("y"), lax.axis_index("c")


def _allgather8(blk, name):
    m, n = blk.shape

    def body(x_ref, out_ref, send_sems, recv_sems, local_sem):
        x, y, c = _my_place()
        me, sibling = (x, y, c), (x, y, 1 - c)
        chips = [(1 - x, y), (x, 1 - y), (1 - x, 1 - y)]

        def rows(px, py, pc):
            return out_ref.at[pl.ds((4 * px + 2 * py + pc) * m, m), :]

        def copy(k, block, to, src=None):
            return pltpu.make_async_remote_copy(
                src_ref=rows(*block) if src is None else src, dst_ref=rows(*block),
                send_sem=send_sems.at[k], recv_sem=recv_sems.at[k], device_id=to, device_id_type=MESH)

        mine = pltpu.make_async_copy(x_ref, rows(*me), local_sem)
        mine.start()
        first = [copy(0, me, sibling, src=x_ref)]
        first += [copy(1 + j, me, (*chip, c), src=x_ref) for j, chip in enumerate(chips)]
        for cp in first:
            cp.start()
        passed = [copy(4 + j, (*chip, c), sibling) for j, chip in enumerate(chips)]
        for j, chip in enumerate(chips):
            copy(1 + j, (*chip, c), me).wait_recv()
            passed[j].start()
        copy(0, sibling, me).wait_recv()
        for j, chip in enumerate(chips):
            copy(4 + j, (*chip, 1 - c), me).wait_recv()
        for cp in first + passed:
            cp.wait_send()
        mine.wait()

    return pl.pallas_call(
        body, name=name, out_shape=SDS((N_DEV * m, n), blk.dtype),
        in_specs=[VMEM_WHOLE], out_specs=VMEM_WHOLE,
        scratch_shapes=[pltpu.SemaphoreType.DMA((7,)), pltpu.SemaphoreType.DMA((7,)), pltpu.SemaphoreType.DMA],
    )(blk)


class _GatherShards:
    def __init__(self, ws):
        nt = len(ws)
        self.ws = ws
        self.inputs = list(ws)
        self.out_shapes = [SDS((N_CHIPS,) + w.shape, w.dtype) for w in ws]
        self.sem_shapes = [pltpu.SemaphoreType.DMA((6 * nt,)), pltpu.SemaphoreType.DMA((6 * nt,)),
                           pltpu.SemaphoreType.DMA((nt,))]

    def _copies(self, w_refs, out_refs, sems):
        send_sems, recv_sems, local_sems = sems
        x, y, c = _my_place()
        chips = [(1 - x, y), (x, 1 - y), (1 - x, 1 - y)]
        per_tensor = []
        for t, w in enumerate(self.ws):
            half = w.shape[0] // 2
            w_ref, out_ref = w_refs[t], out_refs[t]

            def dst(k, hc, out_ref=out_ref, half=half):
                return out_ref.at[k, pl.ds(hc * half, half), :]

            def copy(s, src, to_dst, to, t=t):
                return pltpu.make_async_remote_copy(
                    src_ref=src, dst_ref=to_dst, send_sem=send_sems.at[6 * t + s], recv_sem=recv_sems.at[6 * t + s],
                    device_id=to, device_id_type=MESH)

            mine = pltpu.make_async_copy(w_ref, out_ref.at[2 * x + y], local_sems.at[t])
            first = [copy(j, w_ref.at[pl.ds(c * half, half), :], dst(2 * x + y, c), (*chip, c))
                     for j, chip in enumerate(chips)]
            landed = [dst(2 * px + py, c) for px, py in chips]
            arrive = [copy(j, landed[j], landed[j], (*chips[j], c)) for j in range(3)]
            passed = [copy(3 + j, landed[j], landed[j], (x, y, 1 - c)) for j in range(3)]
            other = [dst(2 * px + py, 1 - c) for px, py in chips]
            from_sibling = [copy(3 + j, other[j], other[j], (x, y, 1 - c)) for j in range(3)]
            per_tensor.append((mine, first, arrive, passed, from_sibling))
        return per_tensor

    def start(self, w_refs, out_refs, sems):
        for mine, first, _, _, _ in self._copies(w_refs, out_refs, sems):
            mine.start()
            for cp in first:
                cp.start()

    def finish(self, w_refs, out_refs, sems):
        per_tensor = self._copies(w_refs, out_refs, sems)
        for _, _, arrive, passed, _ in per_tensor:
            for j in range(3):
                arrive[j].wait_recv()
                passed[j].start()
        for _, _, _, _, from_sibling in per_tensor:
            for cp in from_sibling:
                cp.wait_recv()
        for mine, first, _, passed, _ in per_tensor:
            for cp in first + passed:
                cp.wait_send()
            mine.wait()


class _ScatterToChips:
    def __init__(self, ps):
        nt = len(ps)
        self.ps = ps
        self.inputs = list(ps)
        self.out_shapes = [SDS((3,) + p.shape[1:], p.dtype) for p in ps]
        self.sem_shapes = [pltpu.SemaphoreType.DMA((3 * nt,)), pltpu.SemaphoreType.DMA((3 * nt,))]

    def _copies(self, p_refs, out_refs, sems):
        send_sems, recv_sems = sems
        x, y, c = _my_place()
        chips = [(1 - x, y), (x, 1 - y), (1 - x, 1 - y)]
        return [pltpu.make_async_remote_copy(
            src_ref=p_refs[t].at[2 * px + py], dst_ref=out_refs[t].at[j], send_sem=send_sems.at[3 * t + j],
            recv_sem=recv_sems.at[3 * t + j], device_id=(px, py, c), device_id_type=MESH)
            for t in range(len(self.ps)) for j, (px, py) in enumerate(chips)]

    def start(self, p_refs, out_refs, sems):
        for cp in self._copies(p_refs, out_refs, sems):
            cp.start()

    def finish(self, p_refs, out_refs, sems):
        for cp in self._copies(p_refs, out_refs, sems):
            cp.wait()


def _run_comm(comm, name):
    ni, no = len(comm.inputs), len(comm.out_shapes)

    def body(*refs):
        comm.start(refs[:ni], refs[ni:ni + no], refs[ni + no:])
        comm.finish(refs[:ni], refs[ni:ni + no], refs[ni + no:])

    return pl.pallas_call(body, name=name, out_shape=comm.out_shapes, in_specs=[ANY] * ni, out_specs=[ANY] * no,
                          scratch_shapes=comm.sem_shapes)(*comm.inputs)


def _launch(body, name, grid, in_specs, out_specs, out_shape, args, sem, scratch_shapes=(), comm=None):
    if comm is None:
        return pl.pallas_call(body, name=name, grid=grid, in_specs=in_specs, out_specs=out_specs,
                              out_shape=out_shape, scratch_shapes=list(scratch_shapes),
                              compiler_params=_cp(*sem))(*args), None
    single = not isinstance(out_shape, (list, tuple))
    out_specs_l = [out_specs] if single else list(out_specs)
    out_shape_l = [out_shape] if single else list(out_shape)
    n_in, n_out, n_scr = len(in_specs), len(out_shape_l), len(scratch_shapes)
    nci, nco = len(comm.inputs), len(comm.out_shapes)

    def carried(*refs):
        ins, refs = refs[:n_in], refs[n_in:]
        cins, refs = refs[:nci], refs[nci:]
        outs, refs = refs[:n_out], refs[n_out:]
        couts, refs = refs[:nco], refs[nco:]
        scr, sems = refs[:n_scr], refs[n_scr:]
        ids = [pl.program_id(ax) for ax in range(len(grid))]
        first = functools.reduce(jnp.logical_and, [i == 0 for i in ids])
        last = functools.reduce(jnp.logical_and, [i == g - 1 for i, g in zip(ids, grid)])

        @pl.when(first)
        def _():
            comm.start(cins, couts, sems)

        body(*ins, *outs, *scr)

        @pl.when(last)
        def _():
            comm.finish(cins, couts, sems)

    res = pl.pallas_call(
        carried, name=name, grid=grid, in_specs=list(in_specs) + [ANY] * nci, out_specs=out_specs_l + [ANY] * nco,
        out_shape=out_shape_l + list(comm.out_shapes), scratch_shapes=list(scratch_shapes) + list(comm.sem_shapes),
        compiler_params=_cp(*["arbitrary"] * len(grid)))(*args, *comm.inputs)
    main = res[:n_out]
    return (main[0] if single else main), res[n_out:]


class _SwapHalves:
    def __init__(self, gs):
        nt = len(gs)
        self.gs = gs
        self.inputs = list(gs)
        self.out_shapes = [SDS((g.shape[0], g.shape[1] // 2, g.shape[2]), g.dtype) for g in gs]
        self.sem_shapes = [pltpu.SemaphoreType.DMA((nt,)), pltpu.SemaphoreType.DMA((nt,))]

    def _copies(self, g_refs, out_refs, sems):
        send_sems, recv_sems = sems
        x, y, c = _my_place()
        cps = []
        for t, g in enumerate(self.gs):
            half = g.shape[1] // 2
            cps.append(pltpu.make_async_remote_copy(
                src_ref=g_refs[t].at[:, pl.ds((1 - c) * half, half), :], dst_ref=out_refs[t],
                send_sem=send_sems.at[t], recv_sem=recv_sems.at[t], device_id=(x, y, 1 - c), device_id_type=MESH))
        return cps

    def start(self, g_refs, out_refs, sems):
        for cp in self._copies(g_refs, out_refs, sems):
            cp.start()

    def finish(self, g_refs, out_refs, sems):
        for cp in self._copies(g_refs, out_refs, sems):
            cp.wait()


def _join_sibling_halves(fs, name):
    nt = len(fs)

    def body(*refs):
        out_refs = refs[nt:2 * nt]
        send_sems, recv_sems = refs[2 * nt:]
        x, y, c = _my_place()
        cps = []
        for t in range(nt):
            r = fs[t].shape[0] // 2
            mine = out_refs[t].at[pl.ds(c * r, r), :]
            cp = pltpu.make_async_remote_copy(
                src_ref=mine, dst_ref=mine, send_sem=send_sems.at[t], recv_sem=recv_sems.at[t],
                device_id=(x, y, 1 - c), device_id_type=MESH)
            cp.start()
            cps.append((cp, r))
        for t, (cp, r) in enumerate(cps):
            cp.wait_send()
            other = out_refs[t].at[pl.ds((1 - c) * r, r), :]
            pltpu.make_async_remote_copy(
                src_ref=other, dst_ref=other, send_sem=send_sems.at[t], recv_sem=recv_sems.at[t],
                device_id=(x, y, 1 - c), device_id_type=MESH).wait_recv()

    return pl.pallas_call(
        body, name=name, out_shape=[SDS(f.shape, f.dtype) for f in fs],
        in_specs=[ANY] * nt, out_specs=[ANY] * nt, input_output_aliases={t: t for t in range(nt)},
        scratch_shapes=[pltpu.SemaphoreType.DMA((nt,)), pltpu.SemaphoreType.DMA((nt,))],
    )(*fs)


def _add_sibling(g, recv, name):
    _, R, C = g.shape
    half = R // 2
    br = _tile(half, max(16, (1 << 19) // C), 16)
    nrb = half // br

    def body(g_ref, r_ref, bf_ref, own_ref):
        s = g_ref[...] + r_ref[...]
        bf_ref[...] = s.astype(BF16)

        @pl.when(pl.program_id(1) == 2 * lax.axis_index("x") + lax.axis_index("y"))
        def _():
            own_ref[...] = s

    return pl.pallas_call(
        body, name=name, grid=(nrb, N_CHIPS),
        in_specs=[pl.BlockSpec((None, br, C), lambda i, k: (k, lax.axis_index("c") * nrb + i, 0)),
                  pl.BlockSpec((None, br, C), lambda i, k: (k, i, 0))],
        out_specs=[pl.BlockSpec((None, br, C), lambda i, k: (k, i, 0)),
                   pl.BlockSpec((br, C), lambda i, k: (i, 0))],
        out_shape=[SDS((N_CHIPS, half, C), BF16), SDS((half, C), F32)],
        compiler_params=_cp("arbitrary", "arbitrary"),
    )(g, recv)


def _add_chips(own, recv, name):
    r, C = own.shape
    br = _tile(r, max(16, (1 << 19) // C), 16)
    nrb = r // br

    def body(own_ref, r_ref, o_ref):
        s = own_ref[...]
        for j in range(3):
            s = s + r_ref[j].astype(F32)
        o_ref[...] = s

    return pl.pallas_call(
        body, name=name, grid=(nrb,),
        in_specs=[pl.BlockSpec((br, C), lambda i: (i, 0)), pl.BlockSpec((3, br, C), lambda i: (0, i, 0))],
        out_specs=pl.BlockSpec((br, C), lambda i: (lax.axis_index("c") * nrb + i, 0)),
        out_shape=SDS((2 * r, C), F32), compiler_params=_cp("arbitrary"),
    )(own, recv)


class _Reduce:
    def __init__(self, gs, name):
        self.gs, self.name = gs, name
        self.swap = _SwapHalves(gs)

    def swapped(self, brought=None):
        if brought is None:
            brought = _run_comm(self.swap, self.name + "_swap")
        self.parts = [_add_sibling(g, r, self.name + "_add1") for g, r in zip(self.gs, brought)]
        return self

    def scatter(self, which=None):
        which = range(len(self.parts)) if which is None else which
        return _ScatterToChips([self.parts[t][0] for t in which])

    def finish(self, brought=None):
        if brought is None:
            brought = _run_comm(self.scatter(), self.name + "_scatter")
        fins = [_add_chips(p[1], r, self.name + "_add2") for p, r in zip(self.parts, brought)]
        return _join_sibling_halves(fins, self.name + "_join")


def _sum_devices(allv, name):
    _, r, n = allv.shape

    def body(a_ref, o_ref):
        s = a_ref[0]
        for d in range(1, N_DEV):
            s = s + a_ref[d]
        o_ref[...] = s

    return pl.pallas_call(body, name=name, out_shape=SDS((r, n), F32), in_specs=[VMEM_WHOLE],
                          out_specs=VMEM_WHOLE)(allv)


def _adamw(w, g, m, v, name):
    shape = w.shape
    C = shape[-1]
    R = w.size // C
    args = [a.reshape(R, C) for a in (w, g, m, v)]
    br = _tile(R, max(8, (1 << 18) // C), 8)

    def body(w_ref, g_ref, m_ref, v_ref, d_ref, nm_ref, nv_ref):
        g_ = g_ref[...]
        m_ = ADAM_B1 * m_ref[...] + (1.0 - ADAM_B1) * g_
        v_ = ADAM_B2 * v_ref[...] + (1.0 - ADAM_B2) * (g_ * g_)
        m_hat = m_ / (1.0 - ADAM_B1 ** ADAM_STEP)
        v_hat = v_ / (1.0 - ADAM_B2 ** ADAM_STEP)
        d_ref[...] = -ADAM_LR * (m_hat / (jnp.sqrt(v_hat) + ADAM_EPS) + ADAM_WD * w_ref[...])
        nm_ref[...] = m_
        nv_ref[...] = v_

    spec = pl.BlockSpec((br, C), lambda i: (i, 0))
    outs = pl.pallas_call(
        body, name=name, grid=(R // br,), in_specs=[spec] * 4, out_specs=[spec] * 3,
        out_shape=[SDS((R, C), F32)] * 3, compiler_params=_cp("parallel"),
    )(*args)
    return [o.reshape(shape) for o in outs]


def _mod_fwd(c_rows, w_mod, b_cols, name):
    L, D, n = w_mod.shape
    bn = _tile(n, 512, 128)

    def body(c_ref, w_ref, b_ref, o_ref):
        cc = c_ref[...]
        sc = (cc * _sigmoid(cc)).astype(BF16)
        o_ref[...] = _dot(sc, w_ref[...].astype(BF16)) + b_ref[...]

    return pl.pallas_call(
        body, name=name, grid=(L, n // bn),
        in_specs=[pl.BlockSpec((16, D), lambda l, j: (0, 0)),
                  pl.BlockSpec((None, D, bn), lambda l, j: (l, 0, j)),
                  pl.BlockSpec((None, 1, bn), lambda l, j: (l, 0, j))],
        out_specs=pl.BlockSpec((None, 16, bn), lambda l, j: (l, 0, j)),
        out_shape=SDS((L, 16, n), F32), compiler_params=_cp("parallel", "parallel"),
    )(c_rows, w_mod, b_cols)


def _mod_wgrad(c_cols, dmod, name):
    D = c_cols.shape[0]
    L, _, n = dmod.shape
    bd = _tile(D, 512, 8)
    bn = _tile(n, 512, 128)

    def body(c_ref, d_ref, o_ref):
        cc = c_ref[...]
        sc = cc * _sigmoid(cc)
        dm = d_ref[...]
        acc = sc[:, 0:1] * dm[0:1, :]
        for b in range(1, N_DEV):
            acc = acc + sc[:, b:b + 1] * dm[b:b + 1, :]
        o_ref[...] = acc

    return pl.pallas_call(
        body, name=name, grid=(L, D // bd, n // bn),
        in_specs=[pl.BlockSpec((bd, N_DEV), lambda l, i, j: (i, 0)),
                  pl.BlockSpec((None, N_DEV, bn), lambda l, i, j: (l, 0, j))],
        out_specs=pl.BlockSpec((None, bd, bn), lambda l, i, j: (l, i, j)),
        out_shape=SDS((L, D, n), F32), compiler_params=_cp("parallel", "parallel", "parallel"),
    )(c_cols, dmod)


def _mm_in(x, w4, bn, name, gate=None, comm=None):
    M, K = x.shape
    nsh, _, n = w4.shape
    N = nsh * n
    nb = n // bn
    bm = _tile(M, 512, 16)
    x_spec = pl.BlockSpec((bm, K), lambda j, i: (i, 0))
    w_spec = pl.BlockSpec((None, K, bn), lambda j, i: (j // nb, 0, j % nb))
    o_spec = pl.BlockSpec((bm, bn), lambda j, i: (i, j))
    if gate is None:
        def body(x_ref, w_ref, o_ref):
            o_ref[...] = _dot(x_ref[...], w_ref[...]).astype(BF16)

        return _launch(body, name, (N // bn, M // bm), [x_spec, w_spec], o_spec, SDS((M, N), BF16), (x, w4),
                       ("parallel", "parallel"), comm=comm)

    def body_gated(x_ref, w_ref, g_ref, up_ref, a_ref):
        up = _dot(x_ref[...], w_ref[...])
        g = g_ref[...].astype(F32)
        up_ref[...] = up.astype(BF16)
        a_ref[...] = (g * _sigmoid(g) * up).astype(BF16)

    return _launch(body_gated, name, (N // bn, M // bm), [x_spec, w_spec, o_spec], [o_spec, o_spec],
                   [SDS((M, N), BF16)] * 2, (x, w4, gate), ("parallel", "parallel"), comm=comm)


def _mm_out_res(a, w, h, cvec, name, groups=False, comm=None):
    M = a.shape[0]
    N = h.shape[1]
    if groups:
        bn = w.shape[2]
        a_spec = pl.BlockSpec((_tile(M, 512, 16), w.shape[1]), lambda j, i: (i, j))
        w_spec = pl.BlockSpec((None, w.shape[1], bn), lambda j, i: (j, 0, 0))
    else:
        bn = _tile(N, 512, 128)
        a_spec = pl.BlockSpec((_tile(M, 512, 16), a.shape[1]), lambda j, i: (i, 0))
        w_spec = pl.BlockSpec((a.shape[1], bn), lambda j, i: (0, j))
    bm = _tile(M, 512, 16)
    o_spec = pl.BlockSpec((bm, bn), lambda j, i: (i, j))

    def body(a_ref, w_ref, h_ref, c_ref, hn_ref, y_ref):
        y = _dot(a_ref[...], w_ref[...])
        hn_ref[...] = h_ref[...] + c_ref[...] * y
        y_ref[...] = y.astype(BF16)

    return _launch(body, name, (N // bn, M // bm),
                   [a_spec, w_spec, o_spec, pl.BlockSpec((1, bn), lambda j, i: (0, j))], [o_spec, o_spec],
                   [SDS((M, N), F32), SDS((M, N), BF16)], (a, w, h, cvec), ("parallel", "parallel"), comm=comm)


def _mm_nt(dy, w, name, groups=False, swiglu=None, comm=None):
    M = dy.shape[0]
    bm = _tile(M, 1024, 16)
    if groups:
        N = dy.shape[1]
        bn = w.shape[1]
        dy_spec = pl.BlockSpec((bm, w.shape[2]), lambda j, i: (i, j))
        w_spec = pl.BlockSpec((None, bn, w.shape[2]), lambda j, i: (j, 0, 0))
    else:
        N = w.shape[0]
        bn = _tile(N, 512, 128)
        dy_spec = pl.BlockSpec((bm, dy.shape[1]), lambda j, i: (i, 0))
        w_spec = pl.BlockSpec((bn, w.shape[1]), lambda j, i: (j, 0))
    o_spec = pl.BlockSpec((bm, bn), lambda j, i: (i, j))
    if swiglu is None:
        def body(dy_ref, w_ref, o_ref):
            o_ref[...] = _dot_nt(dy_ref[...], w_ref[...]).astype(BF16)

        return _launch(body, name, (N // bn, M // bm), [dy_spec, w_spec], o_spec, SDS((M, N), BF16), (dy, w),
                       ("parallel", "parallel"), comm=comm)

    def body_swiglu(dy_ref, w_ref, g_ref, u_ref, dg_ref, du_ref):
        da = _dot_nt(dy_ref[...], w_ref[...])
        g = g_ref[...].astype(F32)
        sg = _sigmoid(g)
        silu = g * sg
        dg_ref[...] = (da * u_ref[...].astype(F32) * (sg + silu * (1.0 - sg))).astype(BF16)
        du_ref[...] = (da * silu).astype(BF16)

    return _launch(body_swiglu, name, (N // bn, M // bm), [dy_spec, w_spec, o_spec, o_spec], [o_spec, o_spec],
                   [SDS((M, N), BF16)] * 2, (dy, w, *swiglu), ("parallel", "parallel"), comm=comm)


def _mm_nt_acc(dx, w4, name, add=None, comm=None):
    M = dx.shape[0]
    nc, K, n = w4.shape
    bm = _tile(M, 512, 16)
    with_add = add is not None

    def body(*refs):
        dx_ref, w_ref = refs[:2]
        o_ref, acc_ref = refs[-2:]
        c = pl.program_id(1)
        s = _dot_nt(dx_ref[...], w_ref[...])

        @pl.when(c == 0)
        def _():
            acc_ref[...] = s + refs[2][...].astype(F32) if with_add else s

        @pl.when(c > 0)
        def _():
            acc_ref[...] += s

        @pl.when(c == nc - 1)
        def _():
            o_ref[...] = acc_ref[...].astype(BF16)

    o_spec = pl.BlockSpec((bm, K), lambda i, c: (i, 0))
    in_specs = [pl.BlockSpec((bm, n), lambda i, c: (i, c)), pl.BlockSpec((None, K, n), lambda i, c: (c, 0, 0))]
    args = [dx, w4]
    if with_add:
        in_specs.append(o_spec)
        args.append(add)
    return _launch(body, name, (M // bm, nc), in_specs, o_spec, SDS((M, K), BF16), args, ("parallel", "arbitrary"),
                   scratch_shapes=[pltpu.VMEM((bm, K), F32)], comm=comm)


def _mm_tn(x, dy, name, shard_cols=None, groups=None, comm=None):
    M, K = x.shape
    N = dy.shape[1]
    bm = _tile(M, 1024, 16)
    if groups is not None:
        kg, ng = K // groups, N // groups
        grid = (groups, 1, M // bm)
        x_spec = pl.BlockSpec((bm, kg), lambda i, j, s: (s, i))
        dy_spec = pl.BlockSpec((bm, ng), lambda i, j, s: (s, i))
        o_spec = pl.BlockSpec((None, kg, ng), lambda i, j, s: (i, 0, 0))
        out_shape = SDS((groups, kg, ng), F32)
    else:
        bko = _tile(K, 1408, 128)
        if shard_cols is not None:
            bn = _tile(shard_cols, 1536, 128)
            nb = shard_cols // bn
            o_spec = pl.BlockSpec((None, bko, bn), lambda i, j, s: (j // nb, i, j % nb))
            out_shape = SDS((N_CHIPS, K, shard_cols), F32)
        else:
            bn = _tile(N, 1024, 128)
            o_spec = pl.BlockSpec((bko, bn), lambda i, j, s: (i, j))
            out_shape = SDS((K, N), F32)
        grid = (K // bko, N // bn, M // bm)
        x_spec = pl.BlockSpec((bm, bko), lambda i, j, s: (s, i))
        dy_spec = pl.BlockSpec((bm, bn), lambda i, j, s: (s, j))

    def body(x_ref, dy_ref, o_ref):
        p = _dot_tn(x_ref[...], dy_ref[...])

        @pl.when(pl.program_id(2) == 0)
        def _():
            o_ref[...] = p

        @pl.when(pl.program_id(2) > 0)
        def _():
            o_ref[...] += p

    return _launch(body, name, grid, [x_spec, dy_spec], o_spec, out_shape, (x, dy),
                   ("parallel", "parallel", "arbitrary"), comm=comm)


def _norm_mod_rows(h, g, scale, shift):
    r = lax.rsqrt(jnp.mean(h * h, axis=-1, keepdims=True) + RMS_EPS)
    return (h * r) * g * (1.0 + scale) + shift


def _vec_spec(D):
    return pl.BlockSpec((1, D), lambda i: (0, 0))


def _norm_mod(h, g, scale, shift, name):
    S, D = h.shape
    bs = _tile(S, 512, 16)

    def body(h_ref, g_ref, sc_ref, sh_ref, u_ref):
        u_ref[...] = _norm_mod_rows(h_ref[...], g_ref[...], sc_ref[...], sh_ref[...]).astype(BF16)

    row = pl.BlockSpec((bs, D), lambda i: (i, 0))
    return pl.pallas_call(
        body, name=name, grid=(S // bs,), in_specs=[row, _vec_spec(D), _vec_spec(D), _vec_spec(D)],
        out_specs=row, out_shape=SDS((S, D), BF16), compiler_params=_cp("parallel"))(h, g, scale, shift)


def _band(rows, cols, lo, hi):
    d = lax.broadcasted_iota(jnp.int32, (rows, cols), 1) - lax.broadcasted_iota(jnp.int32, (rows, cols), 0)
    return jnp.where((d >= lo) & (d < hi), 1.0, 0.0).astype(BF16)


def _band_apply(band, x):
    hi, lo = _split_bf16(x)
    return _dot(band, hi) + _dot(band, lo)


def _pool_pre(h, g, scale, shift, name):
    S, D = h.shape
    ng = len(POOL_WINDOWS)
    pg = D // ng
    bs = _tile(S, 256, POOL_HALO)
    hb = bs // POOL_HALO

    def body(h_ref, hh_ref, g_ref, sc_ref, sh_ref, o_ref):
        i = pl.program_id(0)
        u = _norm_mod_rows(h_ref[...], g_ref[...], sc_ref[...], sh_ref[...])
        uh = _norm_mod_rows(hh_ref[...], g_ref[...], sc_ref[...], sh_ref[...])
        uh = jnp.where(i == 0, 0.0, uh)
        ue = jnp.concatenate([uh, u], axis=0)
        t = i * bs + lax.broadcasted_iota(jnp.int32, (bs, 1), 0)
        for gi, w in enumerate(POOL_WINDOWS):
            cols = slice(gi * pg, (gi + 1) * pg)
            band = _band(bs, bs + POOL_HALO, POOL_HALO - w + 1, POOL_HALO + 1)
            inv = 1.0 / jnp.minimum(t + 1, w).astype(F32)
            o_ref[:, cols] = (_band_apply(band, ue[:, cols]) * inv - u[:, cols]).astype(BF16)

    row = pl.BlockSpec((bs, D), lambda i: (i, 0))
    halo = pl.BlockSpec((POOL_HALO, D), lambda i: (jnp.maximum(i * hb - 1, 0), 0))
    return pl.pallas_call(
        body, name=name, grid=(S // bs,),
        in_specs=[row, halo, _vec_spec(D), _vec_spec(D), _vec_spec(D)],
        out_specs=row, out_shape=SDS((S, D), BF16), compiler_params=_cp("parallel"))(h, h, g, scale, shift)


def _pool_post(dd, name):
    S, D = dd.shape
    ng = len(POOL_WINDOWS)
    pg = D // ng
    bs = _tile(S, 256, POOL_HALO)
    hb = bs // POOL_HALO
    nblk = S // bs

    def body(d_ref, dn_ref, o_ref):
        i = pl.program_id(0)
        d = d_ref[...].astype(F32)
        dn = jnp.where(i == nblk - 1, 0.0, dn_ref[...].astype(F32))
        de = jnp.concatenate([d, dn], axis=0)
        t = i * bs + lax.broadcasted_iota(jnp.int32, (bs + POOL_HALO, 1), 0)
        for gi, w in enumerate(POOL_WINDOWS):
            cols = slice(gi * pg, (gi + 1) * pg)
            inv = 1.0 / jnp.minimum(t + 1, w).astype(F32)
            band = _band(bs, bs + POOL_HALO, 0, w)
            o_ref[:, cols] = (_band_apply(band, de[:, cols] * inv) - d[:, cols]).astype(BF16)

    row = pl.BlockSpec((bs, D), lambda i: (i, 0))
    nxt = pl.BlockSpec((POOL_HALO, D), lambda i: (jnp.minimum((i + 1) * hb, S // POOL_HALO - 1), 0))
    return pl.pallas_call(
        body, name=name, grid=(nblk,), in_specs=[row, nxt], out_specs=row, out_shape=SDS((S, D), BF16),
        compiler_params=_cp("parallel"))(dd, dd)


def _colsum(x):
    return jnp.sum(x, axis=0, keepdims=True)


def _accumulate_rows(st_ref, rows, first):
    @pl.when(first)
    def _():
        st_ref[...] = jnp.zeros_like(st_ref)

    for r, row in enumerate(rows):
        st_ref[r:r + 1, :] += row


def _norm_bwd(h, g, scale, du, dh_out, name, prev=None):
    S, D = h.shape
    bs = _tile(S, 256, 16)
    with_prev = prev is not None

    def body(*refs):
        h_ref, g_ref, sc_ref, du_ref, dho_ref = refs[:5]
        if with_prev:
            y_ref, cv_ref, dh_ref, dy_ref, st_ref = refs[5:]
        else:
            dh_ref, st_ref = refs[5:]
        hh = h_ref[...]
        du_ = du_ref[...].astype(F32)
        r = lax.rsqrt(jnp.mean(hh * hh, axis=-1, keepdims=True) + RMS_EPS)
        xhat = hh * r
        dn = du_ * (1.0 + sc_ref[...])
        dxhat = dn * g_ref[...]
        dh = dho_ref[...] + r * (dxhat - xhat * jnp.mean(dxhat * xhat, axis=-1, keepdims=True))
        dh_ref[...] = dh
        rows = [_colsum(du_), _colsum(du_ * (xhat * g_ref[...])), _colsum(dn * xhat)]
        if with_prev:
            dy_ref[...] = (dh * cv_ref[...]).astype(BF16)
            rows.append(_colsum(dh * y_ref[...].astype(F32)))
        _accumulate_rows(st_ref, rows, pl.program_id(0) == 0)

    row = pl.BlockSpec((bs, D), lambda i: (i, 0))
    st_spec = pl.BlockSpec((8, D), lambda i: (0, 0))
    in_specs = [row, _vec_spec(D), _vec_spec(D), row, row]
    args = [h, g, scale, du, dh_out]
    out_specs, out_shape = [row], [SDS((S, D), F32)]
    if with_prev:
        in_specs += [row, _vec_spec(D)]
        args += list(prev)
        out_specs.append(row)
        out_shape.append(SDS((S, D), BF16))
    out_specs.append(st_spec)
    out_shape.append(SDS((8, D), F32))
    return pl.pallas_call(
        body, name=name, grid=(S // bs,), in_specs=in_specs, out_specs=out_specs, out_shape=out_shape,
        compiler_params=_cp("arbitrary"))(*args)


def _loss_head(h, g, target, y, cvec, name):
    S, D = h.shape
    bs = _tile(S, 256, 16)

    def body(h_ref, g_ref, t_ref, y_ref, cv_ref, dh_ref, dy_ref, st_ref):
        hh = h_ref[...]
        r = lax.rsqrt(jnp.mean(hh * hh, axis=-1, keepdims=True) + RMS_EPS)
        xhat = hh * r
        err = xhat * g_ref[...] - t_ref[...]
        dout = err * (1.0 / D)
        dxhat = dout * g_ref[...]
        dh = r * (dxhat - xhat * jnp.mean(dxhat * xhat, axis=-1, keepdims=True))
        dh_ref[...] = dh
        dy_ref[...] = (dh * cv_ref[...]).astype(BF16)
        rows = [_colsum(dout * xhat), _colsum(dh * y_ref[...].astype(F32)), _colsum(err * err) * (0.5 / D)]
        _accumulate_rows(st_ref, rows, pl.program_id(0) == 0)

    row = pl.BlockSpec((bs, D), lambda i: (i, 0))
    return pl.pallas_call(
        body, name=name, grid=(S // bs,), in_specs=[row, _vec_spec(D), row, row, _vec_spec(D)],
        out_specs=[row, row, pl.BlockSpec((8, D), lambda i: (0, 0))],
        out_shape=[SDS((S, D), F32), SDS((S, D), BF16), SDS((8, D), F32)],
        compiler_params=_cp("arbitrary"))(h, g, target, y, cvec)


def _sum_all(x, name):
    def body(x_ref, o_ref):
        o_ref[...] = jnp.sum(jnp.sum(x_ref[...], axis=1, keepdims=True), axis=0, keepdims=True)

    return pl.pallas_call(body, name=name, out_shape=SDS((1, 1), F32), in_specs=[VMEM_WHOLE],
                          out_specs=VMEM_WHOLE)(x)


def _conv_mid(u3, cw, name):
    S, D3 = u3.shape
    D = D3 // 3
    cb = _tile(D, 512, 128)
    nj = D // cb
    bs = _tile(S, 256, CONV_HALO)
    hb = bs // CONV_HALO

    def body(b_ref, c_ref, v_ref, ch_ref, vh_ref, w_ref, o_ref):
        i = pl.program_id(0)
        z = c_ref[...].astype(F32) * v_ref[...].astype(F32)
        zh = jnp.where(i == 0, 0.0, ch_ref[...].astype(F32) * vh_ref[...].astype(F32))
        ze = jnp.concatenate([zh, z], axis=0)
        w = w_ref[...]
        zc = w[2:3] * z
        zc = zc + w[1:2] * _band_apply(_band(bs, bs + CONV_HALO, CONV_HALO - 1, CONV_HALO), ze)
        zc = zc + w[0:1] * _band_apply(_band(bs, bs + CONV_HALO, CONV_HALO - 2, CONV_HALO - 1), ze)
        o_ref[...] = (b_ref[...].astype(F32) * zc).astype(BF16)

    def blk(off):
        return pl.BlockSpec((bs, cb), lambda i, j: (i, off + j))

    def halo(off):
        return pl.BlockSpec((CONV_HALO, cb), lambda i, j: (jnp.maximum(i * hb - 1, 0), off + j))

    return pl.pallas_call(
        body, name=name, grid=(S // bs, nj),
        in_specs=[blk(0), blk(nj), blk(2 * nj), halo(nj), halo(2 * nj), pl.BlockSpec((3, cb), lambda i, j: (0, j))],
        out_specs=pl.BlockSpec((bs, cb), lambda i, j: (i, j)), out_shape=SDS((S, D), BF16),
        compiler_params=_cp("parallel", "parallel"))(u3, u3, u3, u3, u3, cw)


def _conv_mid_bwd(u3, da, cw, name):
    S, D3 = u3.shape
    D = D3 // 3
    cb = _tile(D, 512, 128)
    nj = D // cb
    bs = _tile(S, 256, CONV_HALO)
    hb = bs // CONV_HALO
    nblk = S // bs
    last_halo = S // CONV_HALO - 1

    def body(b_ref, c_ref, v_ref, ch_ref, vh_ref, bn_ref, da_ref, dan_ref, w_ref, db_ref, dc_ref, dv_ref, dw_ref):
        i = pl.program_id(0)
        c = c_ref[...].astype(F32)
        v = v_ref[...].astype(F32)
        b = b_ref[...].astype(F32)
        da_ = da_ref[...].astype(F32)
        z = c * v
        zh = jnp.where(i == 0, 0.0, ch_ref[...].astype(F32) * vh_ref[...].astype(F32))
        ze = jnp.concatenate([zh, z], axis=0)
        z1 = _band_apply(_band(bs, bs + CONV_HALO, CONV_HALO - 1, CONV_HALO), ze)
        z2 = _band_apply(_band(bs, bs + CONV_HALO, CONV_HALO - 2, CONV_HALO - 1), ze)
        w = w_ref[...]
        zc = w[2:3] * z + w[1:2] * z1 + w[0:1] * z2
        db_ref[...] = (da_ * zc).astype(BF16)
        dzc = da_ * b
        dzn = jnp.where(i == nblk - 1, 0.0, dan_ref[...].astype(F32) * bn_ref[...].astype(F32))
        dze = jnp.concatenate([dzc, dzn], axis=0)
        dz = w[2:3] * dzc
        dz = dz + w[1:2] * _band_apply(_band(bs, bs + CONV_HALO, 1, 2), dze)
        dz = dz + w[0:1] * _band_apply(_band(bs, bs + CONV_HALO, 2, 3), dze)
        dc_ref[...] = (dz * v).astype(BF16)
        dv_ref[...] = (dz * c).astype(BF16)
        dw_ref[...] = jnp.zeros_like(dw_ref)
        dw_ref[0:1, :] = _colsum(dzc * z2)
        dw_ref[1:2, :] = _colsum(dzc * z1)
        dw_ref[2:3, :] = _colsum(dzc * z)

    def blk(off):
        return pl.BlockSpec((bs, cb), lambda i, j: (i, off + j))

    def halo(off):
        return pl.BlockSpec((CONV_HALO, cb), lambda i, j: (jnp.maximum(i * hb - 1, 0), off + j))

    def nxt(off):
        return pl.BlockSpec((CONV_HALO, cb), lambda i, j: (jnp.minimum((i + 1) * hb, last_halo), off + j))

    o_spec = pl.BlockSpec((bs, cb), lambda i, j: (i, j))
    return pl.pallas_call(
        body, name=name, grid=(nblk, nj),
        in_specs=[blk(0), blk(nj), blk(2 * nj), halo(nj), halo(2 * nj), nxt(0), o_spec, nxt(0),
                  pl.BlockSpec((3, cb), lambda i, j: (0, j))],
        out_specs=[o_spec, o_spec, o_spec, pl.BlockSpec((None, 8, cb), lambda i, j: (i, 0, j))],
        out_shape=[SDS((S, D), BF16)] * 3 + [SDS((nblk, 8, D), F32)],
        compiler_params=_cp("parallel", "parallel"))(u3, u3, u3, u3, u3, u3, da, da, cw)


def _sum_lead(x, name):
    n, r, C = x.shape

    def body(x_ref, o_ref):
        @pl.when(pl.program_id(0) == 0)
        def _():
            o_ref[...] = x_ref[...]

        @pl.when(pl.program_id(0) > 0)
        def _():
            o_ref[...] += x_ref[...]

    return pl.pallas_call(
        body, name=name, grid=(n,), in_specs=[pl.BlockSpec((None, r, C), lambda i: (i, 0, 0))],
        out_specs=pl.BlockSpec((r, C), lambda i: (0, 0)), out_shape=SDS((r, C), F32),
        compiler_params=_cp("arbitrary"))(x)


def _log_sigmoids(z):
    lb = jnp.minimum(z, 0.0) - jnp.log(1.0 + jnp.exp(-jnp.abs(z)))
    return lb, lb - z


def _attn_blocks(S):
    bk = _tile(S, 256, 128)
    bq = 4 * bk if S % (4 * bk) == 0 else bk
    return bq, bk


def _tri(n, pred):
    rowi = lax.broadcasted_iota(jnp.int32, (n, n), 0)
    coli = lax.broadcasted_iota(jnp.int32, (n, n), 1)
    return jnp.where(pred(rowi, coli), 1.0, 0.0).astype(BF16)


def _causal_mask(bq, bk, m):
    rowi = lax.broadcasted_iota(jnp.int32, (bq, bk), 0)
    coli = lax.broadcasted_iota(jnp.int32, (bq, bk), 1)
    return m * bk + coli < rowi


def _sb_attention(qkv, name, comm=None):
    S, D3 = qkv.shape
    D = D3 // 3
    H = D // HEAD_DIM
    bq, bk = _attn_blocks(S)
    nq, r = S // bq, bq // bk
    unroll = r
    scale = HEAD_DIM ** -0.5

    def body(q_ref, k_ref, v_ref, o_ref, lt_ref):
        i = pl.program_id(1)
        q = q_ref[...]
        after = _tri(bk, lambda j, s: j > s)

        def block(kb, carry, acc, causal):
            rows = pl.ds(pl.multiple_of(kb * bk, bk), bk)
            z = _dot_nt(q, k_ref[rows, :]) * scale
            lb, l1 = _log_sigmoids(z)
            if causal is not None:
                l1 = jnp.where(causal, l1, 0.0)
            a = jnp.exp(lb + (_dot(l1.astype(BF16), after) + carry))
            if causal is not None:
                a = jnp.where(causal, a, 0.0)
            acc = acc + _dot(a.astype(BF16), v_ref[rows, :])
            return carry + jnp.sum(l1, axis=1, keepdims=True), acc

        carry, acc = jnp.zeros((bq, 1), F32), jnp.zeros((bq, HEAD_DIM), F32)
        for m in reversed(range(r)):
            carry, acc = block(i * r + m, carry, acc, _causal_mask(bq, bk, m))

        def step(j, ca):
            for n in range(unroll):
                ca = block(i * r - 1 - unroll * j - n, ca[0], ca[1], None)
            return ca

        carry, acc = lax.fori_loop(0, i * (r // unroll), step, (carry, acc))
        o_ref[...] = acc.astype(BF16)
        lt_ref[...] = jnp.broadcast_to(carry, (bq, HEAD_DIM))

    head_rows = lambda off: pl.BlockSpec((S, HEAD_DIM), lambda hd, i: (0, off + hd))
    blk = pl.BlockSpec((bq, HEAD_DIM), lambda hd, i: (i, hd))
    return _launch(body, name, (H, nq), [blk, head_rows(H), head_rows(2 * H)], [blk, blk],
                   [SDS((S, D), BF16), SDS((S, D), F32)], (qkv, qkv, qkv), ("parallel", "arbitrary"), comm=comm)


def _sb_attention_bwd(qkv, ltot, do, name):
    S, D3 = qkv.shape
    D = D3 // 3
    H = D // HEAD_DIM
    bq, bk = _attn_blocks(S)
    nq, r = S // bq, bq // bk
    unroll = r
    scale = HEAD_DIM ** -0.5

    def body(q_ref, k_ref, v_ref, lt_ref, do_ref, dq_ref, dk_ref, dv_ref, dkt_acc, dvt_acc):
        i = pl.program_id(1)
        q = q_ref[...]
        do_ = do_ref[...]
        qt = jnp.transpose(q.astype(F32)).astype(BF16)
        dot = jnp.transpose(do_.astype(F32)).astype(BF16)
        lt = lt_ref[:, 0:1]
        after = _tri(bk, lambda j, s: j > s)
        before = _tri(bk, lambda j, s: j < s)

        @pl.when(i == 0)
        def _():
            dkt_acc[...] = jnp.zeros_like(dkt_acc)
            dvt_acc[...] = jnp.zeros_like(dvt_acc)

        def block(kb, c1, ce, dq, causal):
            rows = pl.ds(pl.multiple_of(kb * bk, bk), bk)
            k = k_ref[rows, :]
            v = v_ref[rows, :]
            z = _dot_nt(q, k) * scale
            lb, l1 = _log_sigmoids(z)
            sig = jnp.exp(lb)
            if causal is not None:
                l1 = jnp.where(causal, l1, 0.0)
            c1 = c1 + jnp.sum(l1, axis=1, keepdims=True)
            a = jnp.exp(lb + (_dot(l1.astype(BF16), after) + (lt - c1)))
            if causal is not None:
                a = jnp.where(causal, a, 0.0)
            e = a * _dot_nt(do_, v)
            p = _dot(e.astype(BF16), before) + ce
            dz = e - sig * (e + p)
            if causal is not None:
                dz = jnp.where(causal, dz, 0.0)
            dzb = dz.astype(BF16)
            dkt_acc[kb] += _dot(qt, dzb)
            dvt_acc[kb] += _dot(dot, a.astype(BF16))
            dq = dq + _dot(dzb, k)
            return c1, ce + jnp.sum(e, axis=1, keepdims=True), dq

        def step(j, st):
            for n in range(unroll):
                st = block(unroll * j + n, st[0], st[1], st[2], None)
            return st

        zero = jnp.zeros((bq, 1), F32)
        c1, ce, dq = lax.fori_loop(0, i * (r // unroll), step, (zero, zero, jnp.zeros((bq, HEAD_DIM), F32)))
        for m in range(r):
            c1, ce, dq = block(i * r + m, c1, ce, dq, _causal_mask(bq, bk, m))
        dq_ref[...] = (dq * scale).astype(BF16)

        @pl.when(i == nq - 1)
        def _():
            def flush(kb, _):
                rows = pl.ds(pl.multiple_of(kb * bk, bk), bk)
                dk_ref[rows, :] = (jnp.transpose(dkt_acc[kb]) * scale).astype(BF16)
                dv_ref[rows, :] = jnp.transpose(dvt_acc[kb]).astype(BF16)
                return 0

            lax.fori_loop(0, S // bk, flush, 0)

    head_rows = lambda off: pl.BlockSpec((S, HEAD_DIM), lambda hd, i: (0, off + hd))
    blk = pl.BlockSpec((bq, HEAD_DIM), lambda hd, i: (i, hd))
    return pl.pallas_call(
        body, name=name, grid=(H, nq), in_specs=[blk, head_rows(H), head_rows(2 * H), blk, blk],
        out_specs=[blk, head_rows(0), head_rows(0)], out_shape=[SDS((S, D), BF16)] * 3,
        scratch_shapes=[pltpu.VMEM((S // bk, HEAD_DIM, bk), F32), pltpu.VMEM((S // bk, HEAD_DIM, bk), F32)],
        compiler_params=_cp("arbitrary", "arbitrary"))(qkv, qkv, qkv, ltot, do)


def kernel(x, c, norm_mix_g, norm_ffn_g, w_mod, b_mod, pool_w, pool_scale, conv_w_in, conv_w, conv_w_out, sb_w_qkv, sb_w_o, ffn_w_gate, ffn_w_up, ffn_w_down, final_g, loss_target, m_norm_mix_g, m_norm_ffn_g, m_w_mod, m_b_mod, m_pool_w, m_pool_scale, m_conv_w_in, m_conv_w, m_conv_w_out, m_sb_w_qkv, m_sb_w_o, m_ffn_w_gate, m_ffn_w_up, m_ffn_w_down, m_final_g, v_norm_mix_g, v_norm_ffn_g, v_w_mod, v_b_mod, v_pool_w, v_pool_scale, v_conv_w_in, v_conv_w, v_conv_w_out, v_sb_w_qkv, v_sb_w_o, v_ffn_w_gate, v_ffn_w_up, v_ffn_w_down, v_final_g):
    S, D = x.shape[1], x.shape[2]
    L = norm_mix_g.shape[0]
    nmod = w_mod.shape[2]
    nf = ffn_w_gate.shape[2]
    n3 = conv_w_in.shape[2]
    nd = conv_w_out.shape[1]
    cb = n3 // 3
    ng = pool_w.shape[1]
    pg = pool_w.shape[3]
    n_pool = pool_w.shape[0]
    assert D % HEAD_DIM == 0 and S % 256 == 0 and nd == cb and N_CHIPS * nd == D and pg * ng == D

    mx, my, mc = lax.axis_index("x"), lax.axis_index("y"), lax.axis_index("c")
    chip = 2 * mx + my
    dev = 2 * chip + mc
    hx, ht = x[0], loss_target[0]

    c_all = _allgather8(jnp.broadcast_to(c, (8, D)), "gather_c").reshape(N_DEV, 8, D)[:, 0]
    c_rows = jnp.concatenate([c_all, jnp.zeros((8, D), F32)], axis=0)
    b_cols = lax.dynamic_slice_in_dim(b_mod, chip * nmod, nmod, axis=1).reshape(L, 1, nmod)
    mod_cols = _mod_fwd(c_rows, w_mod, b_cols, "mod_fwd")
    mod_all = _allgather8(mod_cols.reshape(L * 16, nmod), "gather_mod").reshape(N_CHIPS, 2, L, 16, nmod)
    mod = lax.dynamic_index_in_dim(mod_all[:, 0], dev, axis=2, keepdims=False)
    mod = jnp.transpose(mod, (1, 0, 2)).reshape(L, N_MOD, 1, D)

    bf = lambda w: w.astype(BF16)
    n_conv, n_sb = conv_w_in.shape[0], sb_w_qkv.shape[0]

    def mixer_shards(l):
        kind, j = l % 3, l // 3
        if kind == 0:
            return [bf(pool_w[j]).reshape(ng * (pg // N_CHIPS), pg)]
        return [bf(conv_w_in[j]), bf(conv_w_out[j])] if kind == 1 else [bf(sb_w_qkv[j]), bf(sb_w_o[j])]

    def mixer_weights(l, got):
        if l % 3 == 0:
            return [jnp.transpose(got[0].reshape(N_CHIPS, ng, pg // N_CHIPS, pg), (1, 0, 2, 3)).reshape(ng, pg, pg)]
        return [got[0], got[1].reshape(D, D)]

    first = mixer_shards(0)
    got = _run_comm(_GatherShards(first + [bf(ffn_w_gate[0]), bf(ffn_w_up[0]), bf(ffn_w_down[0])]), "gather_layer0")
    w_mix = {0: mixer_weights(0, got[:len(first)])}
    w_gate, w_up = {0: got[-3]}, {0: got[-2]}
    w_down = {0: got[-1].reshape(N_CHIPS * nf, D)}
    taps_cols = jnp.concatenate([pool_scale, conv_w.reshape(-1, nd)], axis=0)
    n_small = taps_cols.shape[0]
    small_rows = jnp.concatenate([taps_cols, jnp.zeros((16 - n_small, nd), F32)], axis=0)
    small_all = _allgather8(small_rows, "gather_small").reshape(N_CHIPS, 2, 16, nd)[:, 0]
    small_full = jnp.transpose(small_all, (1, 0, 2)).reshape(16, D)
    pool_scale_full = small_full[:n_pool]
    conv_taps_full = small_full[n_pool:n_small].reshape(n_conv, 3, D)

    saved = []
    h = hx
    for l in range(L):
        kind, j = l % 3, l // 3
        sh_m, sc_m, gt_m, sh_f, sc_f, gt_f = (mod[l, r] for r in range(N_MOD))
        gm = norm_mix_g[l].reshape(1, D)
        gf = norm_ffn_g[l].reshape(1, D)
        s = {"h_in": h}
        if kind == 0:
            s["diff"] = _pool_pre(h, gm, sc_m, sh_m, f"pool_pre{l}")
            s["cvec_m"] = gt_m * pool_scale_full[j].reshape(1, D)
            (h, s["y_m"]), _ = _mm_out_res(s["diff"], w_mix[l][0], h, s["cvec_m"], f"pool_mm{l}", groups=True)
        else:
            down_comm = None if l in w_down else _GatherShards([bf(ffn_w_down[l])])
            s["u"] = _norm_mod(h, gm, sc_m, sh_m, f"norm_mix{l}")
            s["cvec_m"] = gt_m
            if kind == 1:
                s["u3"], got = _mm_in(s["u"], w_mix[l][0], n3, f"conv_in{l}", comm=down_comm)
                s["a_m"] = _conv_mid(s["u3"], conv_taps_full[j], f"conv_mid{l}")
                (h, s["y_m"]), _ = _mm_out_res(s["a_m"], w_mix[l][1], h, gt_m, f"conv_out{l}")
            else:
                s["qkv"], got = _mm_in(s["u"], w_mix[l][0], n3, f"sb_qkv{l}", comm=down_comm)
                ahead = l + 1 < L and (l + 1) % 3 == 0
                (s["o"], s["ltot"]), got_ahead = _sb_attention(
                    s["qkv"], f"sb_attn{l}", comm=_GatherShards([bf(ffn_w_down[l + 1])]) if ahead else None)
                if ahead:
                    w_down[l + 1] = got_ahead[0].reshape(N_CHIPS * nf, D)
                (h, s["y_m"]), _ = _mm_out_res(s["o"], w_mix[l][1], h, gt_m, f"sb_out{l}")
            if down_comm:
                w_down[l] = got[0].reshape(N_CHIPS * nf, D)
        s["h_mid"] = h
        s["u2"] = _norm_mod(h, gf, sc_f, sh_f, f"norm_ffn{l}")
        more = l + 1 < L
        s["gate"], got = _mm_in(s["u2"], w_gate[l], nf, f"ffn_gate{l}",
                                comm=_GatherShards(mixer_shards(l + 1)) if more else None)
        if more:
            w_mix[l + 1] = mixer_weights(l + 1, got)
        (s["up"], s["a_f"]), got = _mm_in(s["u2"], w_up[l], nf, f"ffn_up{l}", gate=s["gate"],
                                          comm=_GatherShards([bf(ffn_w_gate[l + 1])]) if more else None)
        if more:
            w_gate[l + 1] = got[0]
        late = [bf(ffn_w_up[l + 1])] if more else []
        if more and (l + 1) % 3 == 0 and l + 1 not in w_down:
            late.append(bf(ffn_w_down[l + 1]))
        (h, s["y_f"]), got = _mm_out_res(s["a_f"], w_down[l], h, gt_f, f"ffn_down{l}",
                                         comm=_GatherShards(late) if more else None)
        if more:
            w_up[l + 1] = got[0]
            if len(late) > 1:
                w_down[l + 1] = got[1].reshape(N_CHIPS * nf, D)
        saved.append(s)

    gt_f_last = mod[L - 1, 5]
    dh, dy, st = _loss_head(h, final_g.reshape(1, D), ht, saved[-1]["y_f"], gt_f_last, "loss_head")
    loss = lax.psum(_sum_all(st[2:3], "loss_sum")[0, 0], ("x", "y", "c"))
    d_final_g = st[0:1]
    p_gate_f = st[1:2]
    d_norm_mix, d_norm_ffn = [None] * L, [None] * L
    d_mod = [[None] * N_MOD for _ in range(L)]
    d_pool_scale, d_taps = [None] * n_pool, [None] * n_conv
    big = {}
    ffn_reduce = mix_reduce = None
    for l in reversed(range(L)):
        kind, j = l % 3, l // 3
        s = saved[l]
        sh_m, sc_m, gt_m, sh_f, sc_f, gt_f = (mod[l, r] for r in range(N_MOD))
        gm = norm_mix_g[l].reshape(1, D)
        gf = norm_ffn_g[l].reshape(1, D)
        d_mod[l][5] = p_gate_f
        (dgate, dup), brought = _mm_nt(dy, w_down[l], f"ffn_down_bwd{l}", swiglu=(s["gate"], s["up"]),
                                       comm=ffn_reduce.scatter([0, 1]) if ffn_reduce else None)
        gw_down, brought_down = _mm_tn(s["a_f"], dy, f"ffn_down_wgrad{l}",
                                       comm=ffn_reduce.scatter([2]) if ffn_reduce else None)
        if ffn_reduce:
            big[("ffn", l + 1)] = ffn_reduce.finish(list(brought) + list(brought_down))
        gw_gate, brought = _mm_tn(s["u2"], dgate, f"ffn_gate_wgrad{l}", shard_cols=nf,
                                  comm=mix_reduce.scatter() if mix_reduce else None)
        if mix_reduce:
            big[("mix", l + 1)] = mix_reduce.finish(brought)
        gw_up, _ = _mm_tn(s["u2"], dup, f"ffn_up_wgrad{l}", shard_cols=nf)
        ffn_reduce = _Reduce([gw_gate, gw_up, gw_down.reshape(N_CHIPS, nf, D)], f"reduce_ffn{l}")
        du2, brought = _mm_nt_acc(dgate, w_gate[l], f"ffn_gate_bwd{l}", comm=ffn_reduce.swap)
        ffn_reduce.swapped(brought)
        du2, _ = _mm_nt_acc(dup, w_up[l], f"ffn_up_bwd{l}", add=du2)
        mix_reduce = None
        dh, dy, st = _norm_bwd(s["h_mid"], gf, sc_f, du2, dh, f"norm_ffn_bwd{l}", prev=(s["y_m"], s["cvec_m"]))
        d_mod[l][3], d_mod[l][4], d_norm_ffn[l] = st[0:1], st[1:2], st[2:3]
        p_mix = st[3:4]
        if kind == 0:
            d_mod[l][2] = p_mix * pool_scale_full[j].reshape(1, D)
            d_pool_scale[j] = p_mix * gt_m
            dd, _ = _mm_nt(dy, w_mix[l][0], f"pool_mm_bwd{l}", groups=True)
            big[("pool", j)], _ = _mm_tn(s["diff"], dy, f"pool_wgrad{l}", groups=ng)
            du = _pool_post(dd, f"pool_post{l}")
        elif kind == 1:
            d_mod[l][2] = p_mix
            da, _ = _mm_nt(dy, w_mix[l][1], f"conv_out_bwd{l}")
            gw_out, _ = _mm_tn(s["a_m"], dy, f"conv_out_wgrad{l}")
            db, dc, dv, dtap = _conv_mid_bwd(s["u3"], da, conv_taps_full[j], f"conv_mid_bwd{l}")
            d_taps[j] = _sum_lead(dtap, f"conv_tap_sum{l}")[0:3]
            du3 = jnp.concatenate([db, dc, dv], axis=1)
            gw_in, _ = _mm_tn(s["u"], du3, f"conv_in_wgrad{l}", shard_cols=n3)
            mix_reduce = _Reduce([gw_in, gw_out.reshape(N_CHIPS, nd, D)], f"reduce_conv{l}")
            du, brought = _mm_nt_acc(du3, w_mix[l][0], f"conv_in_bwd{l}", comm=mix_reduce.swap)
            mix_reduce.swapped(brought)
        else:
            d_mod[l][2] = p_mix
            do, _ = _mm_nt(dy, w_mix[l][1], f"sb_out_bwd{l}")
            gw_o, _ = _mm_tn(s["o"], dy, f"sb_out_wgrad{l}")
            dq, dk, dv = _sb_attention_bwd(s["qkv"], s["ltot"], do, f"sb_attn_bwd{l}")
            dqkv = jnp.concatenate([dq, dk, dv], axis=1)
            gw_qkv, _ = _mm_tn(s["u"], dqkv, f"sb_qkv_wgrad{l}", shard_cols=n3)
            mix_reduce = _Reduce([gw_qkv, gw_o.reshape(N_CHIPS, nd, D)], f"reduce_sb{l}")
            du, brought = _mm_nt_acc(dqkv, w_mix[l][0], f"sb_qkv_bwd{l}", comm=mix_reduce.swap)
            mix_reduce.swapped(brought)
        if l > 0:
            prev = (saved[l - 1]["y_f"], mod[l - 1, 5])
            dh, dy, st = _norm_bwd(s["h_in"], gm, sc_m, du, dh, f"norm_mix_bwd{l}", prev=prev)
            p_gate_f = st[3:4]
        else:
            dh, st = _norm_bwd(s["h_in"], gm, sc_m, du, dh, f"norm_mix_bwd{l}")
        d_mod[l][0], d_mod[l][1], d_norm_mix[l] = st[0:1], st[1:2], st[2:3]
    grad_x = dh.reshape(1, S, D)
    big[("ffn", 0)] = ffn_reduce.finish()
    if mix_reduce:
        big[("mix", 0)] = mix_reduce.finish()

    gw_pool = jnp.stack([big[("pool", j)] for j in range(n_pool)])
    gw_pool = jnp.transpose(gw_pool.reshape(n_pool, ng, N_CHIPS, pg // N_CHIPS, pg), (2, 0, 1, 3, 4))
    pool_reduce = _Reduce([gw_pool.reshape(N_CHIPS, n_pool * ng * (pg // N_CHIPS), pg)], "reduce_pool")
    (g_pool,) = pool_reduce.swapped().finish()

    rows = [d_final_g] + d_norm_mix + d_norm_ffn + [r for l in range(L) for r in d_mod[l]] + d_pool_scale
    rows += [d_taps[j] for j in range(n_conv)]
    vec = jnp.concatenate(rows, axis=0)
    n_rows = vec.shape[0]
    pad = -n_rows % 8
    vec = jnp.concatenate([vec, jnp.zeros((pad, D), F32)], axis=0) if pad else vec
    vec_all = _allgather8(vec, "gather_small_grads").reshape(N_DEV, n_rows + pad, D)
    tot = _sum_devices(vec_all, "sum_small_grads")
    r0 = 1 + 2 * L
    g_final = tot[0]
    g_norm_mix = tot[1:1 + L]
    g_norm_ffn = tot[1 + L:r0]
    g_b_mod = tot[r0:r0 + N_MOD * L].reshape(L, N_MOD * D)
    r1 = r0 + N_MOD * L
    g_pool_scale = lax.dynamic_slice_in_dim(tot[r1:r1 + n_pool], chip * nd, nd, axis=1)
    g_taps = lax.dynamic_slice_in_dim(tot[r1 + n_pool:r1 + n_pool + 3 * n_conv], chip * nd, nd, axis=1)
    g_conv_w = g_taps.reshape(conv_w.shape)
    dmod_all = vec_all[:, r0:r1].reshape(N_DEV, L, N_MOD * D)
    dmod_cols = jnp.transpose(lax.dynamic_slice_in_dim(dmod_all, chip * nmod, nmod, axis=2), (1, 0, 2))
    g_w_mod = _mod_wgrad(jnp.transpose(c_all), dmod_cols, "mod_wgrad")

    g_ffn_gate = jnp.stack([big[("ffn", l)][0] for l in range(L)])
    g_ffn_up = jnp.stack([big[("ffn", l)][1] for l in range(L)])
    g_ffn_down = jnp.stack([big[("ffn", l)][2] for l in range(L)])
    g_conv_in = jnp.stack([big[("mix", 3 * j + 1)][0] for j in range(n_conv)])
    g_conv_out = jnp.stack([big[("mix", 3 * j + 1)][1] for j in range(n_conv)])
    g_sb_qkv = jnp.stack([big[("mix", 3 * j + 2)][0] for j in range(n_sb)])
    g_sb_o = jnp.stack([big[("mix", 3 * j + 2)][1] for j in range(n_sb)])
    g_pool_w = g_pool.reshape(pool_w.shape)

    grads = [g_norm_mix, g_norm_ffn, g_w_mod, g_b_mod, g_pool_w, g_pool_scale, g_conv_in, g_conv_w, g_conv_out,
             g_sb_qkv, g_sb_o, g_ffn_gate, g_ffn_up, g_ffn_down, g_final]
    weights = [norm_mix_g, norm_ffn_g, w_mod, b_mod, pool_w, pool_scale, conv_w_in, conv_w, conv_w_out,
               sb_w_qkv, sb_w_o, ffn_w_gate, ffn_w_up, ffn_w_down, final_g]
    ms = [m_norm_mix_g, m_norm_ffn_g, m_w_mod, m_b_mod, m_pool_w, m_pool_scale, m_conv_w_in, m_conv_w, m_conv_w_out,
          m_sb_w_qkv, m_sb_w_o, m_ffn_w_gate, m_ffn_w_up, m_ffn_w_down, m_final_g]
    vs = [v_norm_mix_g, v_norm_ffn_g, v_w_mod, v_b_mod, v_pool_w, v_pool_scale, v_conv_w_in, v_conv_w, v_conv_w_out,
          v_sb_w_qkv, v_sb_w_o, v_ffn_w_gate, v_ffn_w_up, v_ffn_w_down, v_final_g]
    deltas, new_ms, new_vs = [], [], []
    for n, (w, g, m, v) in enumerate(zip(weights, grads, ms, vs)):
        if w.ndim == 1:
            w, g, m, v = (a.reshape(1, -1) for a in (w, g, m, v))
        g = g.reshape(w.shape)
        grads[n] = g.reshape(weights[n].shape)
        d, nm, nv = _adamw(w, g, m, v, f"adamw{n}")
        deltas.append(d.reshape(weights[n].shape))
        new_ms.append(nm.reshape(weights[n].shape))
        new_vs.append(nv.reshape(weights[n].shape))
    return (loss, grad_x, *grads, *deltas, *new_ms, *new_vs)
```

```python
import functools

import jax
import jax.numpy as jnp
from jax import lax
from jax.experimental import pallas as pl
from jax.experimental.pallas import tpu as pltpu

F32 = jnp.float32
BF16 = jnp.bfloat16
SDS = jax.ShapeDtypeStruct
MESH = pl.DeviceIdType.MESH

RMS_EPS = 1e-6
POOL_WINDOWS = (2, 4, 8, 16)
POOL_HALO = 16
CONV_HALO = 16
HEAD_DIM = 128
N_MOD = 6
N_CHIPS = 4
N_DEV = 8
ADAM_LR = 0.001
ADAM_B1 = 0.9
ADAM_B2 = 0.999
ADAM_EPS = 1e-08
ADAM_WD = 0.01
ADAM_STEP = 10
VMEM_LIMIT_V7X = 52 * 1024 * 1024
ANY = pl.BlockSpec(memory_space=pl.ANY)
VMEM_WHOLE = pl.BlockSpec(memory_space=pltpu.VMEM)


def _cp(*sem):
    return pltpu.CompilerParams(dimension_semantics=sem, vmem_limit_bytes=VMEM_LIMIT_V7X)


def _tile(n, pref, unit):
    if n <= pref:
        return n
    t = (pref // unit) * unit
    while t >= unit:
        if n % t == 0:
            return t
        t -= unit
    return n


def _dot(a, b):
    return jnp.dot(a, b, preferred_element_type=F32)


def _dot_nt(a, b):
    return lax.dot_general(a, b, (((1,), (1,)), ((), ())), preferred_element_type=F32)


def _dot_tn(a, b):
    return lax.dot_general(a, b, (((0,), (0,)), ((), ())), preferred_element_type=F32)


def _sigmoid(x):
    return 1.0 / (1.0 + jnp.exp(-x))


def _my_place():
    return lax.axis_index("x"), lax.axis_index("y"), lax.axis_index("c")


def _allgather8(blk, name):
    m, n = blk.shape

    def body(x_ref, out_ref, send_sems, recv_sems, local_sem):
        x, y, c = _my_place()
        me, sibling = (x, y, c), (x, y, 1 - c)
        chips = [(1 - x, y), (x, 1 - y), (1 - x, 1 - y)]

        def rows(px, py, pc):
            return out_ref.at[pl.ds((4 * px + 2 * py + pc) * m, m), :]

        def copy(k, block, to, src=None):
            return pltpu.make_async_remote_copy(
                src_ref=rows(*block) if src is None else src, dst_ref=rows(*block),
                send_sem=send_sems.at[k], recv_sem=recv_sems.at[k], device_id=to, device_id_type=MESH)

        mine = pltpu.make_async_copy(x_ref, rows(*me), local_sem)
        mine.start()
        first = [copy(0, me, sibling, src=x_ref)]
        first += [copy(1 + j, me, (*chip, c), src=x_ref) for j, chip in enumerate(chips)]
        for cp in first:
            cp.start()
        passed = [copy(4 + j, (*chip, c), sibling) for j, chip in enumerate(chips)]
        for j, chip in enumerate(chips):
            copy(1 + j, (*chip, c), me).wait_recv()
            passed[j].start()
        copy(0, sibling, me).wait_recv()
        for j, chip in enumerate(chips):
            copy(4 + j, (*chip, 1 - c), me).wait_recv()
        for cp in first + passed:
            cp.wait_send()
        mine.wait()

    return pl.pallas_call(
        body, name=name, out_shape=SDS((N_DEV * m, n), blk.dtype),
        in_specs=[VMEM_WHOLE], out_specs=VMEM_WHOLE,
        scratch_shapes=[pltpu.SemaphoreType.DMA((7,)), pltpu.SemaphoreType.DMA((7,)), pltpu.SemaphoreType.DMA],
    )(blk)


class _GatherShards:
    def __init__(self, ws):
        nt = len(ws)
        self.ws = ws
        self.inputs = list(ws)
        self.out_shapes = [SDS((N_CHIPS,) + w.shape, w.dtype) for w in ws]
        self.sem_shapes = [pltpu.SemaphoreType.DMA((6 * nt,)), pltpu.SemaphoreType.DMA((6 * nt,)),
                           pltpu.SemaphoreType.DMA((nt,))]

    def _copies(self, w_refs, out_refs, sems):
        send_sems, recv_sems, local_sems = sems
        x, y, c = _my_place()
        chips = [(1 - x, y), (x, 1 - y), (1 - x, 1 - y)]
        per_tensor = []
        for t, w in enumerate(self.ws):
            half = w.shape[0] // 2
            w_ref, out_ref = w_refs[t], out_refs[t]

            def dst(k, hc, out_ref=out_ref, half=half):
                return out_ref.at[k, pl.ds(hc * half, half), :]

            def copy(s, src, to_dst, to, t=t):
                return pltpu.make_async_remote_copy(
                    src_ref=src, dst_ref=to_dst, send_sem=send_sems.at[6 * t + s], recv_sem=recv_sems.at[6 * t + s],
                    device_id=to, device_id_type=MESH)

            mine = pltpu.make_async_copy(w_ref, out_ref.at[2 * x + y], local_sems.at[t])
            first = [copy(j, w_ref.at[pl.ds(c * half, half), :], dst(2 * x + y, c), (*chip, c))
                     for j, chip in enumerate(chips)]
            landed = [dst(2 * px + py, c) for px, py in chips]
            arrive = [copy(j, landed[j], landed[j], (*chips[j], c)) for j in range(3)]
            passed = [copy(3 + j, landed[j], landed[j], (x, y, 1 - c)) for j in range(3)]
            other = [dst(2 * px + py, 1 - c) for px, py in chips]
            from_sibling = [copy(3 + j, other[j], other[j], (x, y, 1 - c)) for j in range(3)]
            per_tensor.append((mine, first, arrive, passed, from_sibling))
        return per_tensor

    def start(self, w_refs, out_refs, sems):
        for mine, first, _, _, _ in self._copies(w_refs, out_refs, sems):
            mine.start()
            for cp in first:
                cp.start()

    def finish(self, w_refs, out_refs, sems):
        per_tensor = self._copies(w_refs, out_refs, sems)
        for _, _, arrive, passed, _ in per_tensor:
            for j in range(3):
                arrive[j].wait_recv()
                passed[j].start()
        for _, _, _, _, from_sibling in per_tensor:
            for cp in from_sibling:
                cp.wait_recv()
        for mine, first, _, passed, _ in per_tensor:
            for cp in first + passed:
                cp.wait_send()
            mine.wait()


class _ScatterToChips:
    def __init__(self, ps):
        nt = len(ps)
        self.ps = ps
        self.inputs = list(ps)
        self.out_shapes = [SDS((3,) + p.shape[1:], p.dtype) for p in ps]
        self.sem_shapes = [pltpu.SemaphoreType.DMA((3 * nt,)), pltpu.SemaphoreType.DMA((3 * nt,))]

    def _copies(self, p_refs, out_refs, sems):
        send_sems, recv_sems = sems
        x, y, c = _my_place()
        chips = [(1 - x, y), (x, 1 - y), (1 - x, 1 - y)]
        return [pltpu.make_async_remote_copy(
            src_ref=p_refs[t].at[2 * px + py], dst_ref=out_refs[t].at[j], send_sem=send_sems.at[3 * t + j],
            recv_sem=recv_sems.at[3 * t + j], device_id=(px, py, c), device_id_type=MESH)
            for t in range(len(self.ps)) for j, (px, py) in enumerate(chips)]

    def start(self, p_refs, out_refs, sems):
        for cp in self._copies(p_refs, out_refs, sems):
            cp.start()

    def finish(self, p_refs, out_refs, sems):
        for cp in self._copies(p_refs, out_refs, sems):
            cp.wait()


def _run_comm(comm, name):
    ni, no = len(comm.inputs), len(comm.out_shapes)

    def body(*refs):
        comm.start(refs[:ni], refs[ni:ni + no], refs[ni + no:])
        comm.finish(refs[:ni], refs[ni:ni + no], refs[ni + no:])

    return pl.pallas_call(body, name=name, out_shape=comm.out_shapes, in_specs=[ANY] * ni, out_specs=[ANY] * no,
                          scratch_shapes=comm.sem_shapes)(*comm.inputs)


def _launch(body, name, grid, in_specs, out_specs, out_shape, args, sem, scratch_shapes=(), comm=None):
    if comm is None:
        return pl.pallas_call(body, name=name, grid=grid, in_specs=in_specs, out_specs=out_specs,
                              out_shape=out_shape, scratch_shapes=list(scratch_shapes),
                              compiler_params=_cp(*sem))(*args), None
    single = not isinstance(out_shape, (list, tuple))
    out_specs_l = [out_specs] if single else list(out_specs)
    out_shape_l = [out_shape] if single else list(out_shape)
    n_in, n_out, n_scr = len(in_specs), len(out_shape_l), len(scratch_shapes)
    nci, nco = len(comm.inputs), len(comm.out_shapes)

    def carried(*refs):
        ins, refs = refs[:n_in], refs[n_in:]
        cins, refs = refs[:nci], refs[nci:]
        outs, refs = refs[:n_out], refs[n_out:]
        couts, refs = refs[:nco], refs[nco:]
        scr, sems = refs[:n_scr], refs[n_scr:]
        ids = [pl.program_id(ax) for ax in range(len(grid))]
        first = functools.reduce(jnp.logical_and, [i == 0 for i in ids])
        last = functools.reduce(jnp.logical_and, [i == g - 1 for i, g in zip(ids, grid)])

        @pl.when(first)
        def _():
            comm.start(cins, couts, sems)

        body(*ins, *outs, *scr)

        @pl.when(last)
        def _():
            comm.finish(cins, couts, sems)

    res = pl.pallas_call(
        carried, name=name, grid=grid, in_specs=list(in_specs) + [ANY] * nci, out_specs=out_specs_l + [ANY] * nco,
        out_shape=out_shape_l + list(comm.out_shapes), scratch_shapes=list(scratch_shapes) + list(comm.sem_shapes),
        compiler_params=_cp(*["arbitrary"] * len(grid)))(*args, *comm.inputs)
    main = res[:n_out]
    return (main[0] if single else main), res[n_out:]


class _SwapHalves:
    def __init__(self, gs):
        nt = len(gs)
        self.gs = gs
        self.inputs = list(gs)
        self.out_shapes = [SDS((g.shape[0], g.shape[1] // 2, g.shape[2]), g.dtype) for g in gs]
        self.sem_shapes = [pltpu.SemaphoreType.DMA((nt,)), pltpu.SemaphoreType.DMA((nt,))]

    def _copies(self, g_refs, out_refs, sems):
        send_sems, recv_sems = sems
        x, y, c = _my_place()
        cps = []
        for t, g in enumerate(self.gs):
            half = g.shape[1] // 2
            cps.append(pltpu.make_async_remote_copy(
                src_ref=g_refs[t].at[:, pl.ds((1 - c) * half, half), :], dst_ref=out_refs[t],
                send_sem=send_sems.at[t], recv_sem=recv_sems.at[t], device_id=(x, y, 1 - c), device_id_type=MESH))
        return cps

    def start(self, g_refs, out_refs, sems):
        for cp in self._copies(g_refs, out_refs, sems):
            cp.start()

    def finish(self, g_refs, out_refs, sems):
        for cp in self._copies(g_refs, out_refs, sems):
            cp.wait()


def _join_sibling_halves(fs, name):
    nt = len(fs)

    def body(*refs):
        out_refs = refs[nt:2 * nt]
        send_sems, recv_sems = refs[2 * nt:]
        x, y, c = _my_place()
        cps = []
        for t in range(nt):
            r = fs[t].shape[0] // 2
            mine = out_refs[t].at[pl.ds(c * r, r), :]
            cp = pltpu.make_async_remote_copy(
                src_ref=mine, dst_ref=mine, send_sem=send_sems.at[t], recv_sem=recv_sems.at[t],
                device_id=(x, y, 1 - c), device_id_type=MESH)
            cp.start()
            cps.append((cp, r))
        for t, (cp, r) in enumerate(cps):
            cp.wait_send()
            other = out_refs[t].at[pl.ds((1 - c) * r, r), :]
            pltpu.make_async_remote_copy(
                src_ref=other, dst_ref=other, send_sem=send_sems.at[t], recv_sem=recv_sems.at[t],
                device_id=(x, y, 1 - c), device_id_type=MESH).wait_recv()

    return pl.pallas_call(
        body, name=name, out_shape=[SDS(f.shape, f.dtype) for f in fs],
        in_specs=[ANY] * nt, out_specs=[ANY] * nt, input_output_aliases={t: t for t in range(nt)},
        scratch_shapes=[pltpu.SemaphoreType.DMA((nt,)), pltpu.SemaphoreType.DMA((nt,))],
    )(*fs)


def _add_sibling(g, recv, name):
    _, R, C = g.shape
    half = R // 2
    br = _tile(half, max(16, (1 << 19) // C), 16)
    nrb = half // br

    def body(g_ref, r_ref, bf_ref, own_ref):
        s = g_ref[...] + r_ref[...]
        bf_ref[...] = s.astype(BF16)

        @pl.when(pl.program_id(1) == 2 * lax.axis_index("x") + lax.axis_index("y"))
        def _():
            own_ref[...] = s

    return pl.pallas_call(
        body, name=name, grid=(nrb, N_CHIPS),
        in_specs=[pl.BlockSpec((None, br, C), lambda i, k: (k, lax.axis_index("c") * nrb + i, 0)),
                  pl.BlockSpec((None, br, C), lambda i, k: (k, i, 0))],
        out_specs=[pl.BlockSpec((None, br, C), lambda i, k: (k, i, 0)),
                   pl.BlockSpec((br, C), lambda i, k: (i, 0))],
        out_shape=[SDS((N_CHIPS, half, C), BF16), SDS((half, C), F32)],
        compiler_params=_cp("arbitrary", "arbitrary"),
    )(g, recv)


def _add_chips(own, recv, name):
    r, C = own.shape
    br = _tile(r, max(16, (1 << 19) // C), 16)
    nrb = r // br

    def body(own_ref, r_ref, o_ref):
        s = own_ref[...]
        for j in range(3):
            s = s + r_ref[j].astype(F32)
        o_ref[...] = s

    return pl.pallas_call(
        body, name=name, grid=(nrb,),
        in_specs=[pl.BlockSpec((br, C), lambda i: (i, 0)), pl.BlockSpec((3, br, C), lambda i: (0, i, 0))],
        out_specs=pl.BlockSpec((br, C), lambda i: (lax.axis_index("c") * nrb + i, 0)),
        out_shape=SDS((2 * r, C), F32), compiler_params=_cp("arbitrary"),
    )(own, recv)


class _Reduce:
    def __init__(self, gs, name):
        self.gs, self.name = gs, name
        self.swap = _SwapHalves(gs)

    def swapped(self, brought=None):
        if brought is None:
            brought = _run_comm(self.swap, self.name + "_swap")
        self.parts = [_add_sibling(g, r, self.name + "_add1") for g, r in zip(self.gs, brought)]
        return self

    def scatter(self, which=None):
        which = range(len(self.parts)) if which is None else which
        return _ScatterToChips([self.parts[t][0] for t in which])

    def finish(self, brought=None):
        if brought is None:
            brought = _run_comm(self.scatter(), self.name + "_scatter")
        fins = [_add_chips(p[1], r, self.name + "_add2") for p, r in zip(self.parts, brought)]
        return _join_sibling_halves(fins, self.name + "_join")


def _sum_devices(allv, name):
    _, r, n = allv.shape

    def body(a_ref, o_ref):
        s = a_ref[0]
        for d in range(1, N_DEV):
            s = s + a_ref[d]
        o_ref[...] = s

    return pl.pallas_call(body, name=name, out_shape=SDS((r, n), F32), in_specs=[VMEM_WHOLE],
                          out_specs=VMEM_WHOLE)(allv)


def _adamw(w, g, m, v, name, comm=None):
    shape = w.shape
    C = shape[-1]
    R = w.size // C
    args = [a.reshape(R, C) for a in (w, g, m, v)]
    br = _tile(R, max(8, (1 << 18) // C), 8)

    def body(w_ref, g_ref, m_ref, v_ref, d_ref, nm_ref, nv_ref):
        g_ = g_ref[...]
        m_ = ADAM_B1 * m_ref[...] + (1.0 - ADAM_B1) * g_
        v_ = ADAM_B2 * v_ref[...] + (1.0 - ADAM_B2) * (g_ * g_)
        m_hat = m_ / (1.0 - ADAM_B1 ** ADAM_STEP)
        v_hat = v_ / (1.0 - ADAM_B2 ** ADAM_STEP)
        d_ref[...] = -ADAM_LR * (m_hat / (jnp.sqrt(v_hat) + ADAM_EPS) + ADAM_WD * w_ref[...])
        nm_ref[...] = m_
        nv_ref[...] = v_

    spec = pl.BlockSpec((br, C), lambda i: (i, 0))
    outs, brought = _launch(body, name, (R // br,), [spec] * 4, [spec] * 3, [SDS((R, C), F32)] * 3, args,
                            ("parallel",), comm=comm)
    return [o.reshape(shape) for o in outs], brought


def _mod_fwd(c_rows, w_mod, b_cols, name):
    L, D, n = w_mod.shape
    bn = _tile(n, 512, 128)

    def body(c_ref, w_ref, b_ref, o_ref):
        cc = c_ref[...]
        sc = (cc * _sigmoid(cc)).astype(BF16)
        o_ref[...] = _dot(sc, w_ref[...].astype(BF16)) + b_ref[...]

    return pl.pallas_call(
        body, name=name, grid=(L, n // bn),
        in_specs=[pl.BlockSpec((16, D), lambda l, j: (0, 0)),
                  pl.BlockSpec((None, D, bn), lambda l, j: (l, 0, j)),
                  pl.BlockSpec((None, 1, bn), lambda l, j: (l, 0, j))],
        out_specs=pl.BlockSpec((None, 16, bn), lambda l, j: (l, 0, j)),
        out_shape=SDS((L, 16, n), F32), compiler_params=_cp("parallel", "parallel"),
    )(c_rows, w_mod, b_cols)


def _mod_wgrad(c_cols, dmod, name, comm=None):
    D = c_cols.shape[0]
    L, _, n = dmod.shape
    bd = _tile(D, 512, 8)
    bn = _tile(n, 512, 128)

    def body(c_ref, d_ref, o_ref):
        cc = c_ref[...]
        sc = cc * _sigmoid(cc)
        dm = d_ref[...]
        acc = sc[:, 0:1] * dm[0:1, :]
        for b in range(1, N_DEV):
            acc = acc + sc[:, b:b + 1] * dm[b:b + 1, :]
        o_ref[...] = acc

    return _launch(body, name, (L, D // bd, n // bn),
                   [pl.BlockSpec((bd, N_DEV), lambda l, i, j: (i, 0)),
                    pl.BlockSpec((None, N_DEV, bn), lambda l, i, j: (l, 0, j))],
                   pl.BlockSpec((None, bd, bn), lambda l, i, j: (l, i, j)), SDS((L, D, n), F32), (c_cols, dmod),
                   ("parallel", "parallel", "parallel"), comm=comm)


def _mm_in(x, w4, bn, name, gate=None, comm=None):
    M, K = x.shape
    nsh, _, n = w4.shape
    N = nsh * n
    nb = n // bn
    bm = _tile(M, 1024 if gate is None else 512, 16)
    x_spec = pl.BlockSpec((bm, K), lambda j, i: (i, 0))
    w_spec = pl.BlockSpec((None, K, bn), lambda j, i: (j // nb, 0, j % nb))
    o_spec = pl.BlockSpec((bm, bn), lambda j, i: (i, j))
    if gate is None:
        def body(x_ref, w_ref, o_ref):
            o_ref[...] = _dot(x_ref[...], w_ref[...]).astype(BF16)

        return _launch(body, name, (N // bn, M // bm), [x_spec, w_spec], o_spec, SDS((M, N), BF16), (x, w4),
                       ("parallel", "parallel"), comm=comm)

    def body_gated(x_ref, w_ref, g_ref, up_ref, a_ref):
        up = _dot(x_ref[...], w_ref[...])
        g = g_ref[...].astype(F32)
        up_ref[...] = up.astype(BF16)
        a_ref[...] = (g * _sigmoid(g) * up).astype(BF16)

    return _launch(body_gated, name, (N // bn, M // bm), [x_spec, w_spec, o_spec], [o_spec, o_spec],
                   [SDS((M, N), BF16)] * 2, (x, w4, gate), ("parallel", "parallel"), comm=comm)


def _mm_out_res(a, w, h, cvec, name, groups=False, comm=None):
    M = a.shape[0]
    N = h.shape[1]
    if groups:
        bn = w.shape[2]
        a_spec = pl.BlockSpec((_tile(M, 512, 16), w.shape[1]), lambda j, i: (i, j))
        w_spec = pl.BlockSpec((None, w.shape[1], bn), lambda j, i: (j, 0, 0))
    else:
        bn = _tile(N, 512, 128)
        a_spec = pl.BlockSpec((_tile(M, 512, 16), a.shape[1]), lambda j, i: (i, 0))
        w_spec = pl.BlockSpec((a.shape[1], bn), lambda j, i: (0, j))
    bm = _tile(M, 512, 16)
    o_spec = pl.BlockSpec((bm, bn), lambda j, i: (i, j))

    def body(a_ref, w_ref, h_ref, c_ref, hn_ref, y_ref):
        y = _dot(a_ref[...], w_ref[...])
        hn_ref[...] = h_ref[...] + c_ref[...] * y
        y_ref[...] = y.astype(BF16)

    return _launch(body, name, (N // bn, M // bm),
                   [a_spec, w_spec, o_spec, pl.BlockSpec((1, bn), lambda j, i: (0, j))], [o_spec, o_spec],
                   [SDS((M, N), F32), SDS((M, N), BF16)], (a, w, h, cvec), ("parallel", "parallel"), comm=comm)


def _mm_nt(dy, w, name, groups=False, swiglu=None, comm=None):
    M = dy.shape[0]
    bm = _tile(M, 1024, 16)
    if groups:
        N = dy.shape[1]
        bn = w.shape[1]
        dy_spec = pl.BlockSpec((bm, w.shape[2]), lambda j, i: (i, j))
        w_spec = pl.BlockSpec((None, bn, w.shape[2]), lambda j, i: (j, 0, 0))
    else:
        N = w.shape[0]
        bn = _tile(N, 512, 128)
        dy_spec = pl.BlockSpec((bm, dy.shape[1]), lambda j, i: (i, 0))
        w_spec = pl.BlockSpec((bn, w.shape[1]), lambda j, i: (j, 0))
    o_spec = pl.BlockSpec((bm, bn), lambda j, i: (i, j))
    if swiglu is None:
        def body(dy_ref, w_ref, o_ref):
            o_ref[...] = _dot_nt(dy_ref[...], w_ref[...]).astype(BF16)

        return _launch(body, name, (N // bn, M // bm), [dy_spec, w_spec], o_spec, SDS((M, N), BF16), (dy, w),
                       ("parallel", "parallel"), comm=comm)

    def body_swiglu(dy_ref, w_ref, g_ref, u_ref, dg_ref, du_ref):
        da = _dot_nt(dy_ref[...], w_ref[...])
        g = g_ref[...].astype(F32)
        sg = _sigmoid(g)
        silu = g * sg
        dg_ref[...] = (da * u_ref[...].astype(F32) * (sg + silu * (1.0 - sg))).astype(BF16)
        du_ref[...] = (da * silu).astype(BF16)

    return _launch(body_swiglu, name, (N // bn, M // bm), [dy_spec, w_spec, o_spec, o_spec], [o_spec, o_spec],
                   [SDS((M, N), BF16)] * 2, (dy, w, *swiglu), ("parallel", "parallel"), comm=comm)


def _mm_nt_acc(dx, w4, name, add=None, comm=None):
    M = dx.shape[0]
    nc, K, n = w4.shape
    bm = _tile(M, 512, 16)
    with_add = add is not None

    def body(*refs):
        dx_ref, w_ref = refs[:2]
        o_ref, acc_ref = refs[-2:]
        c = pl.program_id(1)
        s = _dot_nt(dx_ref[...], w_ref[...])

        @pl.when(c == 0)
        def _():
            acc_ref[...] = s + refs[2][...].astype(F32) if with_add else s

        @pl.when(c > 0)
        def _():
            acc_ref[...] += s

        @pl.when(c == nc - 1)
        def _():
            o_ref[...] = acc_ref[...].astype(BF16)

    o_spec = pl.BlockSpec((bm, K), lambda i, c: (i, 0))
    in_specs = [pl.BlockSpec((bm, n), lambda i, c: (i, c)), pl.BlockSpec((None, K, n), lambda i, c: (c, 0, 0))]
    args = [dx, w4]
    if with_add:
        in_specs.append(o_spec)
        args.append(add)
    return _launch(body, name, (M // bm, nc), in_specs, o_spec, SDS((M, K), BF16), args, ("parallel", "arbitrary"),
                   scratch_shapes=[pltpu.VMEM((bm, K), F32)], comm=comm)


def _mm_tn(x, dy, name, shard_cols=None, groups=None, comm=None):
    M, K = x.shape
    N = dy.shape[1]
    bm = _tile(M, 1024, 16)
    if groups is not None:
        kg, ng = K // groups, N // groups
        grid = (groups, 1, M // bm)
        x_spec = pl.BlockSpec((bm, kg), lambda i, j, s: (s, i))
        dy_spec = pl.BlockSpec((bm, ng), lambda i, j, s: (s, i))
        o_spec = pl.BlockSpec((None, kg, ng), lambda i, j, s: (i, 0, 0))
        out_shape = SDS((groups, kg, ng), F32)
    else:
        bko = _tile(K, 1408, 128)
        if shard_cols is not None:
            bn = _tile(shard_cols, 1536, 128)
            nb = shard_cols // bn
            o_spec = pl.BlockSpec((None, bko, bn), lambda i, j, s: (j // nb, i, j % nb))
            out_shape = SDS((N_CHIPS, K, shard_cols), F32)
        else:
            bn = _tile(N, 1024, 128)
            o_spec = pl.BlockSpec((bko, bn), lambda i, j, s: (i, j))
            out_shape = SDS((K, N), F32)
        grid = (K // bko, N // bn, M // bm)
        x_spec = pl.BlockSpec((bm, bko), lambda i, j, s: (s, i))
        dy_spec = pl.BlockSpec((bm, bn), lambda i, j, s: (s, j))

    def body(x_ref, dy_ref, o_ref):
        p = _dot_tn(x_ref[...], dy_ref[...])

        @pl.when(pl.program_id(2) == 0)
        def _():
            o_ref[...] = p

        @pl.when(pl.program_id(2) > 0)
        def _():
            o_ref[...] += p

    return _launch(body, name, grid, [x_spec, dy_spec], o_spec, out_shape, (x, dy),
                   ("parallel", "parallel", "arbitrary"), comm=comm)


def _norm_mod_rows(h, g, scale, shift):
    r = lax.rsqrt(jnp.mean(h * h, axis=-1, keepdims=True) + RMS_EPS)
    return (h * r) * g * (1.0 + scale) + shift


def _vec_spec(D):
    return pl.BlockSpec((1, D), lambda i: (0, 0))


def _norm_mod(h, g, scale, shift, name):
    S, D = h.shape
    bs = _tile(S, 512, 16)

    def body(h_ref, g_ref, sc_ref, sh_ref, u_ref):
        u_ref[...] = _norm_mod_rows(h_ref[...], g_ref[...], sc_ref[...], sh_ref[...]).astype(BF16)

    row = pl.BlockSpec((bs, D), lambda i: (i, 0))
    return pl.pallas_call(
        body, name=name, grid=(S // bs,), in_specs=[row, _vec_spec(D), _vec_spec(D), _vec_spec(D)],
        out_specs=row, out_shape=SDS((S, D), BF16), compiler_params=_cp("parallel"))(h, g, scale, shift)


def _rows_back(x, n):
    return pltpu.roll(x, n, 0)


def _rows_ahead(x, n):
    return pltpu.roll(x, x.shape[0] - n, 0)


def _window_sums(x, w, shift):
    n = 1
    while n < w:
        x = x + shift(x, n)
        n *= 2
    return x


def _pool_pre(h, g, scale, shift, name):
    S, D = h.shape
    ng = len(POOL_WINDOWS)
    pg = D // ng
    bs = _tile(S, 256, POOL_HALO)
    hb = bs // POOL_HALO

    def body(h_ref, hh_ref, g_ref, sc_ref, sh_ref, o_ref):
        i = pl.program_id(0)
        u = _norm_mod_rows(h_ref[...], g_ref[...], sc_ref[...], sh_ref[...])
        uh = _norm_mod_rows(hh_ref[...], g_ref[...], sc_ref[...], sh_ref[...])
        uh = jnp.where(i == 0, 0.0, uh)
        ue = jnp.concatenate([uh, u], axis=0)
        t = i * bs + lax.broadcasted_iota(jnp.int32, (bs, 1), 0)
        for gi, w in enumerate(POOL_WINDOWS):
            cols = slice(gi * pg, (gi + 1) * pg)
            inv = 1.0 / jnp.minimum(t + 1, w).astype(F32)
            sums = _window_sums(ue[:, cols], w, _rows_back)[POOL_HALO:]
            o_ref[:, cols] = (sums * inv - u[:, cols]).astype(BF16)

    row = pl.BlockSpec((bs, D), lambda i: (i, 0))
    halo = pl.BlockSpec((POOL_HALO, D), lambda i: (jnp.maximum(i * hb - 1, 0), 0))
    return pl.pallas_call(
        body, name=name, grid=(S // bs,),
        in_specs=[row, halo, _vec_spec(D), _vec_spec(D), _vec_spec(D)],
        out_specs=row, out_shape=SDS((S, D), BF16), compiler_params=_cp("parallel"))(h, h, g, scale, shift)


def _pool_post(dd, name):
    S, D = dd.shape
    ng = len(POOL_WINDOWS)
    pg = D // ng
    bs = _tile(S, 256, POOL_HALO)
    hb = bs // POOL_HALO
    nblk = S // bs

    def body(d_ref, dn_ref, o_ref):
        i = pl.program_id(0)
        d = d_ref[...].astype(F32)
        dn = jnp.where(i == nblk - 1, 0.0, dn_ref[...].astype(F32))
        de = jnp.concatenate([d, dn], axis=0)
        t = i * bs + lax.broadcasted_iota(jnp.int32, (bs + POOL_HALO, 1), 0)
        for gi, w in enumerate(POOL_WINDOWS):
            cols = slice(gi * pg, (gi + 1) * pg)
            inv = 1.0 / jnp.minimum(t + 1, w).astype(F32)
            sums = _window_sums(de[:, cols] * inv, w, _rows_ahead)[:bs]
            o_ref[:, cols] = (sums - d[:, cols]).astype(BF16)

    row = pl.BlockSpec((bs, D), lambda i: (i, 0))
    nxt = pl.BlockSpec((POOL_HALO, D), lambda i: (jnp.minimum((i + 1) * hb, S // POOL_HALO - 1), 0))
    return pl.pallas_call(
        body, name=name, grid=(nblk,), in_specs=[row, nxt], out_specs=row, out_shape=SDS((S, D), BF16),
        compiler_params=_cp("parallel"))(dd, dd)


def _colsum(x):
    return jnp.sum(x, axis=0, keepdims=True)


def _accumulate_rows(st_ref, rows, first):
    @pl.when(first)
    def _():
        st_ref[...] = jnp.zeros_like(st_ref)

    for r, row in enumerate(rows):
        st_ref[r:r + 1, :] += row


def _norm_bwd(h, g, scale, du, dh_out, name, prev=None):
    S, D = h.shape
    bs = _tile(S, 256, 16)
    with_prev = prev is not None

    def body(*refs):
        h_ref, g_ref, sc_ref, du_ref, dho_ref = refs[:5]
        if with_prev:
            y_ref, cv_ref, dh_ref, dy_ref, st_ref = refs[5:]
        else:
            dh_ref, st_ref = refs[5:]
        hh = h_ref[...]
        du_ = du_ref[...].astype(F32)
        r = lax.rsqrt(jnp.mean(hh * hh, axis=-1, keepdims=True) + RMS_EPS)
        xhat = hh * r
        dn = du_ * (1.0 + sc_ref[...])
        dxhat = dn * g_ref[...]
        dh = dho_ref[...] + r * (dxhat - xhat * jnp.mean(dxhat * xhat, axis=-1, keepdims=True))
        dh_ref[...] = dh
        rows = [_colsum(du_), _colsum(du_ * (xhat * g_ref[...])), _colsum(dn * xhat)]
        if with_prev:
            dy_ref[...] = (dh * cv_ref[...]).astype(BF16)
            rows.append(_colsum(dh * y_ref[...].astype(F32)))
        _accumulate_rows(st_ref, rows, pl.program_id(0) == 0)

    row = pl.BlockSpec((bs, D), lambda i: (i, 0))
    st_spec = pl.BlockSpec((8, D), lambda i: (0, 0))
    in_specs = [row, _vec_spec(D), _vec_spec(D), row, row]
    args = [h, g, scale, du, dh_out]
    out_specs, out_shape = [row], [SDS((S, D), F32)]
    if with_prev:
        in_specs += [row, _vec_spec(D)]
        args += list(prev)
        out_specs.append(row)
        out_shape.append(SDS((S, D), BF16))
    out_specs.append(st_spec)
    out_shape.append(SDS((8, D), F32))
    return pl.pallas_call(
        body, name=name, grid=(S // bs,), in_specs=in_specs, out_specs=out_specs, out_shape=out_shape,
        compiler_params=_cp("arbitrary"))(*args)


def _loss_head(h, g, target, y, cvec, name):
    S, D = h.shape
    bs = _tile(S, 256, 16)

    def body(h_ref, g_ref, t_ref, y_ref, cv_ref, dh_ref, dy_ref, st_ref):
        hh = h_ref[...]
        r = lax.rsqrt(jnp.mean(hh * hh, axis=-1, keepdims=True) + RMS_EPS)
        xhat = hh * r
        err = xhat * g_ref[...] - t_ref[...]
        dout = err * (1.0 / D)
        dxhat = dout * g_ref[...]
        dh = r * (dxhat - xhat * jnp.mean(dxhat * xhat, axis=-1, keepdims=True))
        dh_ref[...] = dh
        dy_ref[...] = (dh * cv_ref[...]).astype(BF16)
        rows = [_colsum(dout * xhat), _colsum(dh * y_ref[...].astype(F32)), _colsum(err * err) * (0.5 / D)]
        _accumulate_rows(st_ref, rows, pl.program_id(0) == 0)

    row = pl.BlockSpec((bs, D), lambda i: (i, 0))
    return pl.pallas_call(
        body, name=name, grid=(S // bs,), in_specs=[row, _vec_spec(D), row, row, _vec_spec(D)],
        out_specs=[row, row, pl.BlockSpec((8, D), lambda i: (0, 0))],
        out_shape=[SDS((S, D), F32), SDS((S, D), BF16), SDS((8, D), F32)],
        compiler_params=_cp("arbitrary"))(h, g, target, y, cvec)


def _sum_all(x, name):
    def body(x_ref, o_ref):
        o_ref[...] = jnp.sum(jnp.sum(x_ref[...], axis=1, keepdims=True), axis=0, keepdims=True)

    return pl.pallas_call(body, name=name, out_shape=SDS((1, 1), F32), in_specs=[VMEM_WHOLE],
                          out_specs=VMEM_WHOLE)(x)


def _conv_mid(u3, cw, name):
    S, D3 = u3.shape
    D = D3 // 3
    cb = _tile(D, 512, 128)
    nj = D // cb
    bs = _tile(S, 256, CONV_HALO)
    hb = bs // CONV_HALO

    def body(b_ref, c_ref, v_ref, ch_ref, vh_ref, w_ref, o_ref):
        i = pl.program_id(0)
        z = c_ref[...].astype(F32) * v_ref[...].astype(F32)
        zh = jnp.where(i == 0, 0.0, ch_ref[...].astype(F32) * vh_ref[...].astype(F32))
        ze = jnp.concatenate([zh, z], axis=0)
        w = w_ref[...]
        zc = w[2:3] * z
        zc = zc + w[1:2] * _rows_back(ze, 1)[CONV_HALO:]
        zc = zc + w[0:1] * _rows_back(ze, 2)[CONV_HALO:]
        o_ref[...] = (b_ref[...].astype(F32) * zc).astype(BF16)

    def blk(off):
        return pl.BlockSpec((bs, cb), lambda i, j: (i, off + j))

    def halo(off):
        return pl.BlockSpec((CONV_HALO, cb), lambda i, j: (jnp.maximum(i * hb - 1, 0), off + j))

    return pl.pallas_call(
        body, name=name, grid=(S // bs, nj),
        in_specs=[blk(0), blk(nj), blk(2 * nj), halo(nj), halo(2 * nj), pl.BlockSpec((3, cb), lambda i, j: (0, j))],
        out_specs=pl.BlockSpec((bs, cb), lambda i, j: (i, j)), out_shape=SDS((S, D), BF16),
        compiler_params=_cp("parallel", "parallel"))(u3, u3, u3, u3, u3, cw)


def _conv_mid_bwd(u3, da, cw, name):
    S, D3 = u3.shape
    D = D3 // 3
    cb = _tile(D, 512, 128)
    nj = D // cb
    bs = _tile(S, 256, CONV_HALO)
    hb = bs // CONV_HALO
    nblk = S // bs
    last_halo = S // CONV_HALO - 1

    def body(b_ref, c_ref, v_ref, ch_ref, vh_ref, bn_ref, da_ref, dan_ref, w_ref, db_ref, dc_ref, dv_ref, dw_ref):
        i = pl.program_id(0)
        c = c_ref[...].astype(F32)
        v = v_ref[...].astype(F32)
        b = b_ref[...].astype(F32)
        da_ = da_ref[...].astype(F32)
        z = c * v
        zh = jnp.where(i == 0, 0.0, ch_ref[...].astype(F32) * vh_ref[...].astype(F32))
        ze = jnp.concatenate([zh, z], axis=0)
        z1 = _rows_back(ze, 1)[CONV_HALO:]
        z2 = _rows_back(ze, 2)[CONV_HALO:]
        w = w_ref[...]
        zc = w[2:3] * z + w[1:2] * z1 + w[0:1] * z2
        db_ref[...] = (da_ * zc).astype(BF16)
        dzc = da_ * b
        dzn = jnp.where(i == nblk - 1, 0.0, dan_ref[...].astype(F32) * bn_ref[...].astype(F32))
        dze = jnp.concatenate([dzc, dzn], axis=0)
        dz = w[2:3] * dzc
        dz = dz + w[1:2] * _rows_ahead(dze, 1)[:bs]
        dz = dz + w[0:1] * _rows_ahead(dze, 2)[:bs]
        dc_ref[...] = (dz * v).astype(BF16)
        dv_ref[...] = (dz * c).astype(BF16)
        dw_ref[...] = jnp.zeros_like(dw_ref)
        dw_ref[0:1, :] = _colsum(dzc * z2)
        dw_ref[1:2, :] = _colsum(dzc * z1)
        dw_ref[2:3, :] = _colsum(dzc * z)

    def blk(off):
        return pl.BlockSpec((bs, cb), lambda i, j: (i, off + j))

    def halo(off):
        return pl.BlockSpec((CONV_HALO, cb), lambda i, j: (jnp.maximum(i * hb - 1, 0), off + j))

    def nxt(off):
        return pl.BlockSpec((CONV_HALO, cb), lambda i, j: (jnp.minimum((i + 1) * hb, last_halo), off + j))

    o_spec = pl.BlockSpec((bs, cb), lambda i, j: (i, j))
    return pl.pallas_call(
        body, name=name, grid=(nblk, nj),
        in_specs=[blk(0), blk(nj), blk(2 * nj), halo(nj), halo(2 * nj), nxt(0), o_spec, nxt(0),
                  pl.BlockSpec((3, cb), lambda i, j: (0, j))],
        out_specs=[o_spec, o_spec, o_spec, pl.BlockSpec((None, 8, cb), lambda i, j: (i, 0, j))],
        out_shape=[SDS((S, D), BF16)] * 3 + [SDS((nblk, 8, D), F32)],
        compiler_params=_cp("parallel", "parallel"))(u3, u3, u3, u3, u3, u3, da, da, cw)


def _sum_lead(x, name):
    n, r, C = x.shape

    def body(x_ref, o_ref):
        @pl.when(pl.program_id(0) == 0)
        def _():
            o_ref[...] = x_ref[...]

        @pl.when(pl.program_id(0) > 0)
        def _():
            o_ref[...] += x_ref[...]

    return pl.pallas_call(
        body, name=name, grid=(n,), in_specs=[pl.BlockSpec((None, r, C), lambda i: (i, 0, 0))],
        out_specs=pl.BlockSpec((r, C), lambda i: (0, 0)), out_shape=SDS((r, C), F32),
        compiler_params=_cp("arbitrary"))(x)


def _log_sigmoids(z):
    lb = jnp.minimum(z, 0.0) - jnp.log(1.0 + jnp.exp(-jnp.abs(z)))
    return lb, lb - z


def _attn_blocks(S):
    bk = _tile(S, 256, 128)
    bq = 4 * bk if S % (4 * bk) == 0 else bk
    return bq, bk


def _tri(n, pred):
    rowi = lax.broadcasted_iota(jnp.int32, (n, n), 0)
    coli = lax.broadcasted_iota(jnp.int32, (n, n), 1)
    return jnp.where(pred(rowi, coli), 1.0, 0.0).astype(BF16)


def _causal_mask(bq, bk, m):
    rowi = lax.broadcasted_iota(jnp.int32, (bq, bk), 0)
    coli = lax.broadcasted_iota(jnp.int32, (bq, bk), 1)
    return m * bk + coli < rowi


def _sb_attention(qkv, name, comm=None):
    S, D3 = qkv.shape
    D = D3 // 3
    H = D // HEAD_DIM
    bq, bk = _attn_blocks(S)
    nq, r = S // bq, bq // bk
    unroll = r
    scale = HEAD_DIM ** -0.5

    def body(q_ref, k_ref, v_ref, o_ref, lt_ref):
        i = pl.program_id(1)
        q = q_ref[...]
        after = _tri(bk, lambda j, s: j > s)

        def block(kb, carry, acc, causal):
            rows = pl.ds(pl.multiple_of(kb * bk, bk), bk)
            z = _dot_nt(q, k_ref[rows, :]) * scale
            lb, l1 = _log_sigmoids(z)
            if causal is not None:
                l1 = jnp.where(causal, l1, 0.0)
            a = jnp.exp(lb + (_dot(l1.astype(BF16), after) + carry))
            if causal is not None:
                a = jnp.where(causal, a, 0.0)
            acc = acc + _dot(a.astype(BF16), v_ref[rows, :])
            return carry + jnp.sum(l1, axis=1, keepdims=True), acc

        carry, acc = jnp.zeros((bq, 1), F32), jnp.zeros((bq, HEAD_DIM), F32)
        for m in reversed(range(r)):
            carry, acc = block(i * r + m, carry, acc, _causal_mask(bq, bk, m))

        def step(j, ca):
            for n in range(unroll):
                ca = block(i * r - 1 - unroll * j - n, ca[0], ca[1], None)
            return ca

        carry, acc = lax.fori_loop(0, i * (r // unroll), step, (carry, acc))
        o_ref[...] = acc.astype(BF16)
        lt_ref[...] = jnp.broadcast_to(carry, (bq, HEAD_DIM))

    head_rows = lambda off: pl.BlockSpec((S, HEAD_DIM), lambda hd, i: (0, off + hd))
    blk = pl.BlockSpec((bq, HEAD_DIM), lambda hd, i: (i, hd))
    return _launch(body, name, (H, nq), [blk, head_rows(H), head_rows(2 * H)], [blk, blk],
                   [SDS((S, D), BF16), SDS((S, D), F32)], (qkv, qkv, qkv), ("parallel", "arbitrary"), comm=comm)


def _sb_attention_bwd(qkv, ltot, do, name):
    S, D3 = qkv.shape
    D = D3 // 3
    H = D // HEAD_DIM
    bq, bk = _attn_blocks(S)
    nq, r = S // bq, bq // bk
    unroll = r
    scale = HEAD_DIM ** -0.5

    def body(q_ref, k_ref, v_ref, lt_ref, do_ref, dq_ref, dk_ref, dv_ref, dkt_acc, dvt_acc):
        i = pl.program_id(1)
        q = q_ref[...]
        do_ = do_ref[...]
        qt = jnp.transpose(q.astype(F32)).astype(BF16)
        dot = jnp.transpose(do_.astype(F32)).astype(BF16)
        lt = lt_ref[:, 0:1]
        after = _tri(bk, lambda j, s: j > s)
        before = _tri(bk, lambda j, s: j < s)

        @pl.when(i == 0)
        def _():
            dkt_acc[...] = jnp.zeros_like(dkt_acc)
            dvt_acc[...] = jnp.zeros_like(dvt_acc)

        def block(kb, c1, ce, dq, causal):
            rows = pl.ds(pl.multiple_of(kb * bk, bk), bk)
            k = k_ref[rows, :]
            v = v_ref[rows, :]
            z = _dot_nt(q, k) * scale
            lb, l1 = _log_sigmoids(z)
            sig = jnp.exp(lb)
            if causal is not None:
                l1 = jnp.where(causal, l1, 0.0)
            c1 = c1 + jnp.sum(l1, axis=1, keepdims=True)
            a = jnp.exp(lb + (_dot(l1.astype(BF16), after) + (lt - c1)))
            if causal is not None:
                a = jnp.where(causal, a, 0.0)
            e = a * _dot_nt(do_, v)
            p = _dot(e.astype(BF16), before) + ce
            dz = e - sig * (e + p)
            if causal is not None:
                dz = jnp.where(causal, dz, 0.0)
            dzb = dz.astype(BF16)
            dkt_acc[kb] += _dot(qt, dzb)
            dvt_acc[kb] += _dot(dot, a.astype(BF16))
            dq = dq + _dot(dzb, k)
            return c1, ce + jnp.sum(e, axis=1, keepdims=True), dq

        def step(j, st):
            for n in range(unroll):
                st = block(unroll * j + n, st[0], st[1], st[2], None)
            return st

        zero = jnp.zeros((bq, 1), F32)
        c1, ce, dq = lax.fori_loop(0, i * (r // unroll), step, (zero, zero, jnp.zeros((bq, HEAD_DIM), F32)))
        for m in range(r):
            c1, ce, dq = block(i * r + m, c1, ce, dq, _causal_mask(bq, bk, m))
        dq_ref[...] = (dq * scale).astype(BF16)

        @pl.when(i == nq - 1)
        def _():
            def flush(kb, _):
                rows = pl.ds(pl.multiple_of(kb * bk, bk), bk)
                dk_ref[rows, :] = (jnp.transpose(dkt_acc[kb]) * scale).astype(BF16)
                dv_ref[rows, :] = jnp.transpose(dvt_acc[kb]).astype(BF16)
                return 0

            lax.fori_loop(0, S // bk, flush, 0)

    head_rows = lambda off: pl.BlockSpec((S, HEAD_DIM), lambda hd, i: (0, off + hd))
    blk = pl.BlockSpec((bq, HEAD_DIM), lambda hd, i: (i, hd))
    return pl.pallas_call(
        body, name=name, grid=(H, nq), in_specs=[blk, head_rows(H), head_rows(2 * H), blk, blk],
        out_specs=[blk, head_rows(0), head_rows(0)], out_shape=[SDS((S, D), BF16)] * 3,
        scratch_shapes=[pltpu.VMEM((S // bk, HEAD_DIM, bk), F32), pltpu.VMEM((S // bk, HEAD_DIM, bk), F32)],
        compiler_params=_cp("arbitrary", "arbitrary"))(qkv, qkv, qkv, ltot, do)


def kernel(x, c, norm_mix_g, norm_ffn_g, w_mod, b_mod, pool_w, pool_scale, conv_w_in, conv_w, conv_w_out, sb_w_qkv, sb_w_o, ffn_w_gate, ffn_w_up, ffn_w_down, final_g, loss_target, m_norm_mix_g, m_norm_ffn_g, m_w_mod, m_b_mod, m_pool_w, m_pool_scale, m_conv_w_in, m_conv_w, m_conv_w_out, m_sb_w_qkv, m_sb_w_o, m_ffn_w_gate, m_ffn_w_up, m_ffn_w_down, m_final_g, v_norm_mix_g, v_norm_ffn_g, v_w_mod, v_b_mod, v_pool_w, v_pool_scale, v_conv_w_in, v_conv_w, v_conv_w_out, v_sb_w_qkv, v_sb_w_o, v_ffn_w_gate, v_ffn_w_up, v_ffn_w_down, v_final_g):
    S, D = x.shape[1], x.shape[2]
    L = norm_mix_g.shape[0]
    nmod = w_mod.shape[2]
    nf = ffn_w_gate.shape[2]
    n3 = conv_w_in.shape[2]
    nd = conv_w_out.shape[1]
    cb = n3 // 3
    ng = pool_w.shape[1]
    pg = pool_w.shape[3]
    n_pool = pool_w.shape[0]
    assert D % HEAD_DIM == 0 and S % 256 == 0 and nd == cb and N_CHIPS * nd == D and pg * ng == D

    mx, my, mc = lax.axis_index("x"), lax.axis_index("y"), lax.axis_index("c")
    chip = 2 * mx + my
    dev = 2 * chip + mc
    hx, ht = x[0], loss_target[0]

    c_all = _allgather8(jnp.broadcast_to(c, (8, D)), "gather_c").reshape(N_DEV, 8, D)[:, 0]
    c_rows = jnp.concatenate([c_all, jnp.zeros((8, D), F32)], axis=0)
    b_cols = lax.dynamic_slice_in_dim(b_mod, chip * nmod, nmod, axis=1).reshape(L, 1, nmod)
    mod_cols = _mod_fwd(c_rows, w_mod, b_cols, "mod_fwd")
    mod_all = _allgather8(mod_cols.reshape(L * 16, nmod), "gather_mod").reshape(N_CHIPS, 2, L, 16, nmod)
    mod = lax.dynamic_index_in_dim(mod_all[:, 0], dev, axis=2, keepdims=False)
    mod = jnp.transpose(mod, (1, 0, 2)).reshape(L, N_MOD, 1, D)

    bf = lambda w: w.astype(BF16)
    n_conv, n_sb = conv_w_in.shape[0], sb_w_qkv.shape[0]

    def mixer_shards(l):
        kind, j = l % 3, l // 3
        if kind == 0:
            return [bf(pool_w[j]).reshape(ng * (pg // N_CHIPS), pg)]
        return [bf(conv_w_in[j]), bf(conv_w_out[j])] if kind == 1 else [bf(sb_w_qkv[j]), bf(sb_w_o[j])]

    def mixer_weights(l, got):
        if l % 3 == 0:
            return [jnp.transpose(got[0].reshape(N_CHIPS, ng, pg // N_CHIPS, pg), (1, 0, 2, 3)).reshape(ng, pg, pg)]
        return [got[0], got[1].reshape(D, D)]

    first = mixer_shards(0)
    got = _run_comm(_GatherShards(first + [bf(ffn_w_gate[0]), bf(ffn_w_up[0]), bf(ffn_w_down[0])]), "gather_layer0")
    w_mix = {0: mixer_weights(0, got[:len(first)])}
    w_gate, w_up = {0: got[-3]}, {0: got[-2]}
    w_down = {0: got[-1].reshape(N_CHIPS * nf, D)}
    taps_cols = jnp.concatenate([pool_scale, conv_w.reshape(-1, nd)], axis=0)
    n_small = taps_cols.shape[0]
    small_rows = jnp.concatenate([taps_cols, jnp.zeros((16 - n_small, nd), F32)], axis=0)
    small_all = _allgather8(small_rows, "gather_small").reshape(N_CHIPS, 2, 16, nd)[:, 0]
    small_full = jnp.transpose(small_all, (1, 0, 2)).reshape(16, D)
    pool_scale_full = small_full[:n_pool]
    conv_taps_full = small_full[n_pool:n_small].reshape(n_conv, 3, D)

    saved = []
    h = hx
    for l in range(L):
        kind, j = l % 3, l // 3
        sh_m, sc_m, gt_m, sh_f, sc_f, gt_f = (mod[l, r] for r in range(N_MOD))
        gm = norm_mix_g[l].reshape(1, D)
        gf = norm_ffn_g[l].reshape(1, D)
        s = {"h_in": h}
        if kind == 0:
            s["diff"] = _pool_pre(h, gm, sc_m, sh_m, f"pool_pre{l}")
            s["cvec_m"] = gt_m * pool_scale_full[j].reshape(1, D)
            (h, s["y_m"]), _ = _mm_out_res(s["diff"], w_mix[l][0], h, s["cvec_m"], f"pool_mm{l}", groups=True)
        else:
            down_comm = None if l in w_down else _GatherShards([bf(ffn_w_down[l])])
            s["u"] = _norm_mod(h, gm, sc_m, sh_m, f"norm_mix{l}")
            s["cvec_m"] = gt_m
            if kind == 1:
                s["u3"], got = _mm_in(s["u"], w_mix[l][0], n3, f"conv_in{l}", comm=down_comm)
                s["a_m"] = _conv_mid(s["u3"], conv_taps_full[j], f"conv_mid{l}")
                (h, s["y_m"]), _ = _mm_out_res(s["a_m"], w_mix[l][1], h, gt_m, f"conv_out{l}")
            else:
                s["qkv"], got = _mm_in(s["u"], w_mix[l][0], n3, f"sb_qkv{l}", comm=down_comm)
                ahead = l + 1 < L and (l + 1) % 3 == 0
                (s["o"], s["ltot"]), got_ahead = _sb_attention(
                    s["qkv"], f"sb_attn{l}", comm=_GatherShards([bf(ffn_w_down[l + 1])]) if ahead else None)
                if ahead:
                    w_down[l + 1] = got_ahead[0].reshape(N_CHIPS * nf, D)
                (h, s["y_m"]), _ = _mm_out_res(s["o"], w_mix[l][1], h, gt_m, f"sb_out{l}")
            if down_comm:
                w_down[l] = got[0].reshape(N_CHIPS * nf, D)
        s["h_mid"] = h
        s["u2"] = _norm_mod(h, gf, sc_f, sh_f, f"norm_ffn{l}")
        more = l + 1 < L
        s["gate"], got = _mm_in(s["u2"], w_gate[l], nf, f"ffn_gate{l}",
                                comm=_GatherShards(mixer_shards(l + 1)) if more else None)
        if more:
            w_mix[l + 1] = mixer_weights(l + 1, got)
        (s["up"], s["a_f"]), got = _mm_in(s["u2"], w_up[l], nf, f"ffn_up{l}", gate=s["gate"],
                                          comm=_GatherShards([bf(ffn_w_gate[l + 1])]) if more else None)
        if more:
            w_gate[l + 1] = got[0]
        late = [bf(ffn_w_up[l + 1])] if more else []
        if more and (l + 1) % 3 == 0 and l + 1 not in w_down:
            late.append(bf(ffn_w_down[l + 1]))
        (h, s["y_f"]), got = _mm_out_res(s["a_f"], w_down[l], h, gt_f, f"ffn_down{l}",
                                         comm=_GatherShards(late) if more else None)
        if more:
            w_up[l + 1] = got[0]
            if len(late) > 1:
                w_down[l + 1] = got[1].reshape(N_CHIPS * nf, D)
        saved.append(s)

    gt_f_last = mod[L - 1, 5]
    dh, dy, st = _loss_head(h, final_g.reshape(1, D), ht, saved[-1]["y_f"], gt_f_last, "loss_head")
    loss = lax.psum(_sum_all(st[2:3], "loss_sum")[0, 0], ("x", "y", "c"))
    d_final_g = st[0:1]
    p_gate_f = st[1:2]
    d_norm_mix, d_norm_ffn = [None] * L, [None] * L
    d_mod = [[None] * N_MOD for _ in range(L)]
    d_pool_scale, d_taps = [None] * n_pool, [None] * n_conv
    big = {}
    ffn_reduce = mix_reduce = None
    for l in reversed(range(L)):
        kind, j = l % 3, l // 3
        s = saved[l]
        sh_m, sc_m, gt_m, sh_f, sc_f, gt_f = (mod[l, r] for r in range(N_MOD))
        gm = norm_mix_g[l].reshape(1, D)
        gf = norm_ffn_g[l].reshape(1, D)
        d_mod[l][5] = p_gate_f
        (dgate, dup), brought = _mm_nt(dy, w_down[l], f"ffn_down_bwd{l}", swiglu=(s["gate"], s["up"]),
                                       comm=ffn_reduce.scatter([0, 1]) if ffn_reduce else None)
        gw_down, brought_down = _mm_tn(s["a_f"], dy, f"ffn_down_wgrad{l}",
                                       comm=ffn_reduce.scatter([2]) if ffn_reduce else None)
        if ffn_reduce:
            big[("ffn", l + 1)] = ffn_reduce.finish(list(brought) + list(brought_down))
        gw_gate, brought = _mm_tn(s["u2"], dgate, f"ffn_gate_wgrad{l}", shard_cols=nf,
                                  comm=mix_reduce.scatter() if mix_reduce else None)
        if mix_reduce:
            big[("mix", l + 1)] = mix_reduce.finish(brought)
        gw_up, _ = _mm_tn(s["u2"], dup, f"ffn_up_wgrad{l}", shard_cols=nf)
        ffn_reduce = _Reduce([gw_gate, gw_up, gw_down.reshape(N_CHIPS, nf, D)], f"reduce_ffn{l}")
        du2, brought = _mm_nt_acc(dgate, w_gate[l], f"ffn_gate_bwd{l}", comm=ffn_reduce.swap)
        ffn_reduce.swapped(brought)
        du2, _ = _mm_nt_acc(dup, w_up[l], f"ffn_up_bwd{l}", add=du2)
        mix_reduce = None
        dh, dy, st = _norm_bwd(s["h_mid"], gf, sc_f, du2, dh, f"norm_ffn_bwd{l}", prev=(s["y_m"], s["cvec_m"]))
        d_mod[l][3], d_mod[l][4], d_norm_ffn[l] = st[0:1], st[1:2], st[2:3]
        p_mix = st[3:4]
        if kind == 0:
            d_mod[l][2] = p_mix * pool_scale_full[j].reshape(1, D)
            d_pool_scale[j] = p_mix * gt_m
            dd, _ = _mm_nt(dy, w_mix[l][0], f"pool_mm_bwd{l}", groups=True)
            big[("pool", j)], _ = _mm_tn(s["diff"], dy, f"pool_wgrad{l}", groups=ng)
            du = _pool_post(dd, f"pool_post{l}")
        elif kind == 1:
            d_mod[l][2] = p_mix
            da, _ = _mm_nt(dy, w_mix[l][1], f"conv_out_bwd{l}")
            gw_out, _ = _mm_tn(s["a_m"], dy, f"conv_out_wgrad{l}")
            db, dc, dv, dtap = _conv_mid_bwd(s["u3"], da, conv_taps_full[j], f"conv_mid_bwd{l}")
            d_taps[j] = _sum_lead(dtap, f"conv_tap_sum{l}")[0:3]
            du3 = jnp.concatenate([db, dc, dv], axis=1)
            gw_in, _ = _mm_tn(s["u"], du3, f"conv_in_wgrad{l}", shard_cols=n3)
            mix_reduce = _Reduce([gw_in, gw_out.reshape(N_CHIPS, nd, D)], f"reduce_conv{l}")
            du, brought = _mm_nt_acc(du3, w_mix[l][0], f"conv_in_bwd{l}", comm=mix_reduce.swap)
            mix_reduce.swapped(brought)
        else:
            d_mod[l][2] = p_mix
            do, _ = _mm_nt(dy, w_mix[l][1], f"sb_out_bwd{l}")
            gw_o, _ = _mm_tn(s["o"], dy, f"sb_out_wgrad{l}")
            dq, dk, dv = _sb_attention_bwd(s["qkv"], s["ltot"], do, f"sb_attn_bwd{l}")
            dqkv = jnp.concatenate([dq, dk, dv], axis=1)
            gw_qkv, _ = _mm_tn(s["u"], dqkv, f"sb_qkv_wgrad{l}", shard_cols=n3)
            mix_reduce = _Reduce([gw_qkv, gw_o.reshape(N_CHIPS, nd, D)], f"reduce_sb{l}")
            du, brought = _mm_nt_acc(dqkv, w_mix[l][0], f"sb_qkv_bwd{l}", comm=mix_reduce.swap)
            mix_reduce.swapped(brought)
        if l > 0:
            prev = (saved[l - 1]["y_f"], mod[l - 1, 5])
            dh, dy, st = _norm_bwd(s["h_in"], gm, sc_m, du, dh, f"norm_mix_bwd{l}", prev=prev)
            p_gate_f = st[3:4]
        else:
            dh, st = _norm_bwd(s["h_in"], gm, sc_m, du, dh, f"norm_mix_bwd{l}")
        d_mod[l][0], d_mod[l][1], d_norm_mix[l] = st[0:1], st[1:2], st[2:3]
    grad_x = dh.reshape(1, S, D)
    if mix_reduce:
        big[("mix", 0)] = mix_reduce.finish()

    gw_pool = jnp.stack([big[("pool", j)] for j in range(n_pool)])
    gw_pool = jnp.transpose(gw_pool.reshape(n_pool, ng, N_CHIPS, pg // N_CHIPS, pg), (2, 0, 1, 3, 4))
    pool_reduce = _Reduce([gw_pool.reshape(N_CHIPS, n_pool * ng * (pg // N_CHIPS), pg)], "reduce_pool")
    (g_pool,) = pool_reduce.swapped().finish()

    rows = [d_final_g] + d_norm_mix + d_norm_ffn + [r for l in range(L) for r in d_mod[l]] + d_pool_scale
    rows += [d_taps[j] for j in range(n_conv)]
    vec = jnp.concatenate(rows, axis=0)
    n_rows = vec.shape[0]
    pad = -n_rows % 8
    vec = jnp.concatenate([vec, jnp.zeros((pad, D), F32)], axis=0) if pad else vec
    vec_all = _allgather8(vec, "gather_small_grads").reshape(N_DEV, n_rows + pad, D)
    tot = _sum_devices(vec_all, "sum_small_grads")
    r0 = 1 + 2 * L
    g_final = tot[0]
    g_norm_mix = tot[1:1 + L]
    g_norm_ffn = tot[1 + L:r0]
    g_b_mod = tot[r0:r0 + N_MOD * L].reshape(L, N_MOD * D)
    r1 = r0 + N_MOD * L
    g_pool_scale = lax.dynamic_slice_in_dim(tot[r1:r1 + n_pool], chip * nd, nd, axis=1)
    g_taps = lax.dynamic_slice_in_dim(tot[r1 + n_pool:r1 + n_pool + 3 * n_conv], chip * nd, nd, axis=1)
    g_conv_w = g_taps.reshape(conv_w.shape)
    dmod_all = vec_all[:, r0:r1].reshape(N_DEV, L, N_MOD * D)
    dmod_cols = jnp.transpose(lax.dynamic_slice_in_dim(dmod_all, chip * nmod, nmod, axis=2), (1, 0, 2))
    g_w_mod, brought_down = _mod_wgrad(jnp.transpose(c_all), dmod_cols, "mod_wgrad", comm=ffn_reduce.scatter([2]))
    w_mod_step, brought = _adamw(w_mod, g_w_mod, m_w_mod, v_w_mod, "adamw_w_mod", comm=ffn_reduce.scatter([0, 1]))
    big[("ffn", 0)] = ffn_reduce.finish(list(brought) + list(brought_down))

    g_ffn_gate = jnp.stack([big[("ffn", l)][0] for l in range(L)])
    g_ffn_up = jnp.stack([big[("ffn", l)][1] for l in range(L)])
    g_ffn_down = jnp.stack([big[("ffn", l)][2] for l in range(L)])
    g_conv_in = jnp.stack([big[("mix", 3 * j + 1)][0] for j in range(n_conv)])
    g_conv_out = jnp.stack([big[("mix", 3 * j + 1)][1] for j in range(n_conv)])
    g_sb_qkv = jnp.stack([big[("mix", 3 * j + 2)][0] for j in range(n_sb)])
    g_sb_o = jnp.stack([big[("mix", 3 * j + 2)][1] for j in range(n_sb)])
    g_pool_w = g_pool.reshape(pool_w.shape)

    grads = [g_norm_mix, g_norm_ffn, g_w_mod, g_b_mod, g_pool_w, g_pool_scale, g_conv_in, g_conv_w, g_conv_out,
             g_sb_qkv, g_sb_o, g_ffn_gate, g_ffn_up, g_ffn_down, g_final]
    weights = [norm_mix_g, norm_ffn_g, w_mod, b_mod, pool_w, pool_scale, conv_w_in, conv_w, conv_w_out,
               sb_w_qkv, sb_w_o, ffn_w_gate, ffn_w_up, ffn_w_down, final_g]
    ms = [m_norm_mix_g, m_norm_ffn_g, m_w_mod, m_b_mod, m_pool_w, m_pool_scale, m_conv_w_in, m_conv_w, m_conv_w_out,
          m_sb_w_qkv, m_sb_w_o, m_ffn_w_gate, m_ffn_w_up, m_ffn_w_down, m_final_g]
    vs = [v_norm_mix_g, v_norm_ffn_g, v_w_mod, v_b_mod, v_pool_w, v_pool_scale, v_conv_w_in, v_conv_w, v_conv_w_out,
          v_sb_w_qkv, v_sb_w_o, v_ffn_w_gate, v_ffn_w_up, v_ffn_w_down, v_final_g]
    deltas, new_ms, new_vs = [], [], []
    for n, (w, g, m, v) in enumerate(zip(weights, grads, ms, vs)):
        if w.ndim == 1:
            w, g, m, v = (a.reshape(1, -1) for a in (w, g, m, v))
        g = g.reshape(w.shape)
        grads[n] = g.reshape(weights[n].shape)
        d, nm, nv = w_mod_step if w is w_mod else _adamw(w, g, m, v, f"adamw{n}")[0]
        deltas.append(d.reshape(weights[n].shape))
        new_ms.append(nm.reshape(weights[n].shape))
        new_vs.append(nv.reshape(weights[n].shape))
    return (loss, grad_x, *grads, *deltas, *new_ms, *new_vs)
```

```python
import functools

import jax
import jax.numpy as jnp
from jax import lax
from jax.experimental import pallas as pl
from jax.experimental.pallas import tpu as pltpu

F32 = jnp.float32
BF16 = jnp.bfloat16
SDS = jax.ShapeDtypeStruct
MESH = pl.DeviceIdType.MESH

RMS_EPS = 1e-6
POOL_WINDOWS = (2, 4, 8, 16)
POOL_HALO = 16
CONV_HALO = 16
HEAD_DIM = 128
N_MOD = 6
N_CHIPS = 4
N_DEV = 8
ADAM_LR = 0.001
ADAM_B1 = 0.9
ADAM_B2 = 0.999
ADAM_EPS = 1e-08
ADAM_WD = 0.01
ADAM_STEP = 10
VMEM_LIMIT_V7X = 52 * 1024 * 1024
ANY = pl.BlockSpec(memory_space=pl.ANY)
VMEM_WHOLE = pl.BlockSpec(memory_space=pltpu.VMEM)


def _cp(*sem):
    return pltpu.CompilerParams(dimension_semantics=sem, vmem_limit_bytes=VMEM_LIMIT_V7X)


def _tile(n, pref, unit):
    if n <= pref:
        return n
    t = (pref // unit) * unit
    while t >= unit:
        if n % t == 0:
            return t
        t -= unit
    return n


def _dot(a, b):
    return jnp.dot(a, b, preferred_element_type=F32)


def _dot_nt(a, b):
    return lax.dot_general(a, b, (((1,), (1,)), ((), ())), preferred_element_type=F32)


def _dot_tn(a, b):
    return lax.dot_general(a, b, (((0,), (0,)), ((), ())), preferred_element_type=F32)


def _sigmoid(x, approx=False):
    return pl.reciprocal(1.0 + jnp.exp(-x), approx=True) if approx else 1.0 / (1.0 + jnp.exp(-x))


def _row_parts(rows, size=256):
    size = size if rows % size == 0 else rows
    return [slice(r, r + size) for r in range(0, rows, size)]


def _my_place():
    return lax.axis_index("x"), lax.axis_index("y"), lax.axis_index("c")


def _allgather8(blk, name):
    m, n = blk.shape

    def body(x_ref, out_ref, send_sems, recv_sems, local_sem):
        x, y, c = _my_place()
        me, sibling = (x, y, c), (x, y, 1 - c)
        chips = [(1 - x, y), (x, 1 - y), (1 - x, 1 - y)]

        def rows(px, py, pc):
            return out_ref.at[pl.ds((4 * px + 2 * py + pc) * m, m), :]

        def copy(k, block, to, src=None):
            return pltpu.make_async_remote_copy(
                src_ref=rows(*block) if src is None else src, dst_ref=rows(*block),
                send_sem=send_sems.at[k], recv_sem=recv_sems.at[k], device_id=to, device_id_type=MESH)

        mine = pltpu.make_async_copy(x_ref, rows(*me), local_sem)
        mine.start()
        first = [copy(0, me, sibling, src=x_ref)]
        first += [copy(1 + j, me, (*chip, c), src=x_ref) for j, chip in enumerate(chips)]
        for cp in first:
            cp.start()
        passed = [copy(4 + j, (*chip, c), sibling) for j, chip in enumerate(chips)]
        for j, chip in enumerate(chips):
            copy(1 + j, (*chip, c), me).wait_recv()
            passed[j].start()
        copy(0, sibling, me).wait_recv()
        for j, chip in enumerate(chips):
            copy(4 + j, (*chip, 1 - c), me).wait_recv()
        for cp in first + passed:
            cp.wait_send()
        mine.wait()

    return pl.pallas_call(
        body, name=name, out_shape=SDS((N_DEV * m, n), blk.dtype),
        in_specs=[VMEM_WHOLE], out_specs=VMEM_WHOLE,
        scratch_shapes=[pltpu.SemaphoreType.DMA((7,)), pltpu.SemaphoreType.DMA((7,)), pltpu.SemaphoreType.DMA],
    )(blk)


class _GatherShards:
    def __init__(self, ws):
        nt = len(ws)
        self.ws = ws
        self.inputs = list(ws)
        self.out_shapes = [SDS((N_CHIPS,) + w.shape, w.dtype) for w in ws]
        self.sem_shapes = [pltpu.SemaphoreType.DMA((6 * nt,)), pltpu.SemaphoreType.DMA((6 * nt,)),
                           pltpu.SemaphoreType.DMA((nt,))]

    def _copies(self, w_refs, out_refs, sems):
        send_sems, recv_sems, local_sems = sems
        x, y, c = _my_place()
        chips = [(1 - x, y), (x, 1 - y), (1 - x, 1 - y)]
        per_tensor = []
        for t, w in enumerate(self.ws):
            half = w.shape[0] // 2
            w_ref, out_ref = w_refs[t], out_refs[t]

            def dst(k, hc, out_ref=out_ref, half=half):
                return out_ref.at[k, pl.ds(hc * half, half), :]

            def copy(s, src, to_dst, to, t=t):
                return pltpu.make_async_remote_copy(
                    src_ref=src, dst_ref=to_dst, send_sem=send_sems.at[6 * t + s], recv_sem=recv_sems.at[6 * t + s],
                    device_id=to, device_id_type=MESH)

            mine = pltpu.make_async_copy(w_ref, out_ref.at[2 * x + y], local_sems.at[t])
            first = [copy(j, w_ref.at[pl.ds(c * half, half), :], dst(2 * x + y, c), (*chip, c))
                     for j, chip in enumerate(chips)]
            landed = [dst(2 * px + py, c) for px, py in chips]
            arrive = [copy(j, landed[j], landed[j], (*chips[j], c)) for j in range(3)]
            passed = [copy(3 + j, landed[j], landed[j], (x, y, 1 - c)) for j in range(3)]
            other = [dst(2 * px + py, 1 - c) for px, py in chips]
            from_sibling = [copy(3 + j, other[j], other[j], (x, y, 1 - c)) for j in range(3)]
            per_tensor.append((mine, first, arrive, passed, from_sibling))
        return per_tensor

    def start(self, w_refs, out_refs, sems):
        for mine, first, _, _, _ in self._copies(w_refs, out_refs, sems):
            mine.start()
            for cp in first:
                cp.start()

    def finish(self, w_refs, out_refs, sems):
        per_tensor = self._copies(w_refs, out_refs, sems)
        for _, _, arrive, passed, _ in per_tensor:
            for j in range(3):
                arrive[j].wait_recv()
                passed[j].start()
        for _, _, _, _, from_sibling in per_tensor:
            for cp in from_sibling:
                cp.wait_recv()
        for mine, first, _, passed, _ in per_tensor:
            for cp in first + passed:
                cp.wait_send()
            mine.wait()


class _ScatterToChips:
    def __init__(self, ps):
        nt = len(ps)
        self.ps = ps
        self.inputs = list(ps)
        self.out_shapes = [SDS((3,) + p.shape[1:], p.dtype) for p in ps]
        self.sem_shapes = [pltpu.SemaphoreType.DMA((3 * nt,)), pltpu.SemaphoreType.DMA((3 * nt,))]

    def _copies(self, p_refs, out_refs, sems):
        send_sems, recv_sems = sems
        x, y, c = _my_place()
        chips = [(1 - x, y), (x, 1 - y), (1 - x, 1 - y)]
        return [pltpu.make_async_remote_copy(
            src_ref=p_refs[t].at[2 * px + py], dst_ref=out_refs[t].at[j], send_sem=send_sems.at[3 * t + j],
            recv_sem=recv_sems.at[3 * t + j], device_id=(px, py, c), device_id_type=MESH)
            for t in range(len(self.ps)) for j, (px, py) in enumerate(chips)]

    def start(self, p_refs, out_refs, sems):
        for cp in self._copies(p_refs, out_refs, sems):
            cp.start()

    def finish(self, p_refs, out_refs, sems):
        for cp in self._copies(p_refs, out_refs, sems):
            cp.wait()


def _run_comm(comm, name):
    ni, no = len(comm.inputs), len(comm.out_shapes)

    def body(*refs):
        comm.start(refs[:ni], refs[ni:ni + no], refs[ni + no:])
        comm.finish(refs[:ni], refs[ni:ni + no], refs[ni + no:])

    return pl.pallas_call(body, name=name, out_shape=comm.out_shapes, in_specs=[ANY] * ni, out_specs=[ANY] * no,
                          scratch_shapes=comm.sem_shapes)(*comm.inputs)


def _launch(body, name, grid, in_specs, out_specs, out_shape, args, sem, scratch_shapes=(), comm=None):
    if comm is None:
        return pl.pallas_call(body, name=name, grid=grid, in_specs=in_specs, out_specs=out_specs,
                              out_shape=out_shape, scratch_shapes=list(scratch_shapes),
                              compiler_params=_cp(*sem))(*args), None
    single = not isinstance(out_shape, (list, tuple))
    out_specs_l = [out_specs] if single else list(out_specs)
    out_shape_l = [out_shape] if single else list(out_shape)
    n_in, n_out, n_scr = len(in_specs), len(out_shape_l), len(scratch_shapes)
    nci, nco = len(comm.inputs), len(comm.out_shapes)

    def carried(*refs):
        ins, refs = refs[:n_in], refs[n_in:]
        cins, refs = refs[:nci], refs[nci:]
        outs, refs = refs[:n_out], refs[n_out:]
        couts, refs = refs[:nco], refs[nco:]
        scr, sems = refs[:n_scr], refs[n_scr:]
        ids = [pl.program_id(ax) for ax in range(len(grid))]
        first = functools.reduce(jnp.logical_and, [i == 0 for i in ids])
        last = functools.reduce(jnp.logical_and, [i == g - 1 for i, g in zip(ids, grid)])

        @pl.when(first)
        def _():
            comm.start(cins, couts, sems)

        body(*ins, *outs, *scr)

        @pl.when(last)
        def _():
            comm.finish(cins, couts, sems)

    res = pl.pallas_call(
        carried, name=name, grid=grid, in_specs=list(in_specs) + [ANY] * nci, out_specs=out_specs_l + [ANY] * nco,
        out_shape=out_shape_l + list(comm.out_shapes), scratch_shapes=list(scratch_shapes) + list(comm.sem_shapes),
        compiler_params=_cp(*["arbitrary"] * len(grid)))(*args, *comm.inputs)
    main = res[:n_out]
    return (main[0] if single else main), res[n_out:]


class _SwapHalves:
    def __init__(self, gs):
        nt = len(gs)
        self.gs = gs
        self.inputs = list(gs)
        self.out_shapes = [SDS((g.shape[0], g.shape[1] // 2, g.shape[2]), g.dtype) for g in gs]
        self.sem_shapes = [pltpu.SemaphoreType.DMA((nt,)), pltpu.SemaphoreType.DMA((nt,))]

    def _copies(self, g_refs, out_refs, sems):
        send_sems, recv_sems = sems
        x, y, c = _my_place()
        cps = []
        for t, g in enumerate(self.gs):
            half = g.shape[1] // 2
            cps.append(pltpu.make_async_remote_copy(
                src_ref=g_refs[t].at[:, pl.ds((1 - c) * half, half), :], dst_ref=out_refs[t],
                send_sem=send_sems.at[t], recv_sem=recv_sems.at[t], device_id=(x, y, 1 - c), device_id_type=MESH))
        return cps

    def start(self, g_refs, out_refs, sems):
        for cp in self._copies(g_refs, out_refs, sems):
            cp.start()

    def finish(self, g_refs, out_refs, sems):
        for cp in self._copies(g_refs, out_refs, sems):
            cp.wait()


def _join_sibling_halves(fs, name):
    nt = len(fs)

    def body(*refs):
        out_refs = refs[nt:2 * nt]
        send_sems, recv_sems = refs[2 * nt:]
        x, y, c = _my_place()
        cps = []
        for t in range(nt):
            r = fs[t].shape[0] // 2
            mine = out_refs[t].at[pl.ds(c * r, r), :]
            cp = pltpu.make_async_remote_copy(
                src_ref=mine, dst_ref=mine, send_sem=send_sems.at[t], recv_sem=recv_sems.at[t],
                device_id=(x, y, 1 - c), device_id_type=MESH)
            cp.start()
            cps.append((cp, r))
        for t, (cp, r) in enumerate(cps):
            cp.wait_send()
            other = out_refs[t].at[pl.ds((1 - c) * r, r), :]
            pltpu.make_async_remote_copy(
                src_ref=other, dst_ref=other, send_sem=send_sems.at[t], recv_sem=recv_sems.at[t],
                device_id=(x, y, 1 - c), device_id_type=MESH).wait_recv()

    return pl.pallas_call(
        body, name=name, out_shape=[SDS(f.shape, f.dtype) for f in fs],
        in_specs=[ANY] * nt, out_specs=[ANY] * nt, input_output_aliases={t: t for t in range(nt)},
        scratch_shapes=[pltpu.SemaphoreType.DMA((nt,)), pltpu.SemaphoreType.DMA((nt,))],
    )(*fs)


def _add_sibling(g, recv, name):
    _, R, C = g.shape
    half = R // 2
    br = _tile(half, max(16, (1 << 19) // C), 16)
    nrb = half // br

    def body(g_ref, r_ref, bf_ref, own_ref):
        s = g_ref[...] + r_ref[...]
        bf_ref[...] = s.astype(BF16)

        @pl.when(pl.program_id(1) == 2 * lax.axis_index("x") + lax.axis_index("y"))
        def _():
            own_ref[...] = s

    return pl.pallas_call(
        body, name=name, grid=(nrb, N_CHIPS),
        in_specs=[pl.BlockSpec((None, br, C), lambda i, k: (k, lax.axis_index("c") * nrb + i, 0)),
                  pl.BlockSpec((None, br, C), lambda i, k: (k, i, 0))],
        out_specs=[pl.BlockSpec((None, br, C), lambda i, k: (k, i, 0)),
                   pl.BlockSpec((br, C), lambda i, k: (i, 0))],
        out_shape=[SDS((N_CHIPS, half, C), BF16), SDS((half, C), F32)],
        compiler_params=_cp("arbitrary", "arbitrary"),
    )(g, recv)


def _add_chips(own, recv, name):
    r, C = own.shape
    br = _tile(r, max(16, (1 << 19) // C), 16)
    nrb = r // br

    def body(own_ref, r_ref, o_ref):
        s = own_ref[...]
        for j in range(3):
            s = s + r_ref[j].astype(F32)
        o_ref[...] = s

    return pl.pallas_call(
        body, name=name, grid=(nrb,),
        in_specs=[pl.BlockSpec((br, C), lambda i: (i, 0)), pl.BlockSpec((3, br, C), lambda i: (0, i, 0))],
        out_specs=pl.BlockSpec((br, C), lambda i: (lax.axis_index("c") * nrb + i, 0)),
        out_shape=SDS((2 * r, C), F32), compiler_params=_cp("arbitrary"),
    )(own, recv)


class _Reduce:
    def __init__(self, gs, name):
        self.gs, self.name = gs, name
        self.swap = _SwapHalves(gs)

    def swapped(self, brought=None):
        if brought is None:
            brought = _run_comm(self.swap, self.name + "_swap")
        self.parts = [_add_sibling(g, r, self.name + "_add1") for g, r in zip(self.gs, brought)]
        return self

    def scatter(self, which=None):
        which = range(len(self.parts)) if which is None else which
        return _ScatterToChips([self.parts[t][0] for t in which])

    def finish(self, brought=None):
        if brought is None:
            brought = _run_comm(self.scatter(), self.name + "_scatter")
        fins = [_add_chips(p[1], r, self.name + "_add2") for p, r in zip(self.parts, brought)]
        return _join_sibling_halves(fins, self.name + "_join")


def _sum_devices(allv, name):
    _, r, n = allv.shape

    def body(a_ref, o_ref):
        s = a_ref[0]
        for d in range(1, N_DEV):
            s = s + a_ref[d]
        o_ref[...] = s

    return pl.pallas_call(body, name=name, out_shape=SDS((r, n), F32), in_specs=[VMEM_WHOLE],
                          out_specs=VMEM_WHOLE)(allv)


def _adamw(w, g, m, v, name, comm=None):
    shape = w.shape
    C = shape[-1]
    R = w.size // C
    args = [a.reshape(R, C) for a in (w, g, m, v)]
    br = _tile(R, max(8, (1 << 18) // C), 8)

    def body(w_ref, g_ref, m_ref, v_ref, d_ref, nm_ref, nv_ref):
        g_ = g_ref[...]
        m_ = ADAM_B1 * m_ref[...] + (1.0 - ADAM_B1) * g_
        v_ = ADAM_B2 * v_ref[...] + (1.0 - ADAM_B2) * (g_ * g_)
        m_hat = m_ / (1.0 - ADAM_B1 ** ADAM_STEP)
        v_hat = v_ / (1.0 - ADAM_B2 ** ADAM_STEP)
        d_ref[...] = -ADAM_LR * (m_hat / (jnp.sqrt(v_hat) + ADAM_EPS) + ADAM_WD * w_ref[...])
        nm_ref[...] = m_
        nv_ref[...] = v_

    spec = pl.BlockSpec((br, C), lambda i: (i, 0))
    outs, brought = _launch(body, name, (R // br,), [spec] * 4, [spec] * 3, [SDS((R, C), F32)] * 3, args,
                            ("parallel",), comm=comm)
    return [o.reshape(shape) for o in outs], brought


def _mod_fwd(c_rows, w_mod, b_cols, name):
    L, D, n = w_mod.shape
    bn = _tile(n, 512, 128)

    def body(c_ref, w_ref, b_ref, o_ref):
        cc = c_ref[...]
        sc = (cc * _sigmoid(cc)).astype(BF16)
        o_ref[...] = _dot(sc, w_ref[...].astype(BF16)) + b_ref[...]

    return pl.pallas_call(
        body, name=name, grid=(L, n // bn),
        in_specs=[pl.BlockSpec((16, D), lambda l, j: (0, 0)),
                  pl.BlockSpec((None, D, bn), lambda l, j: (l, 0, j)),
                  pl.BlockSpec((None, 1, bn), lambda l, j: (l, 0, j))],
        out_specs=pl.BlockSpec((None, 16, bn), lambda l, j: (l, 0, j)),
        out_shape=SDS((L, 16, n), F32), compiler_params=_cp("parallel", "parallel"),
    )(c_rows, w_mod, b_cols)


def _mod_wgrad(c_cols, dmod, name, comm=None):
    D = c_cols.shape[0]
    L, _, n = dmod.shape
    bd = _tile(D, 512, 8)
    bn = _tile(n, 512, 128)

    def body(c_ref, d_ref, o_ref):
        cc = c_ref[...]
        sc = cc * _sigmoid(cc)
        dm = d_ref[...]
        acc = sc[:, 0:1] * dm[0:1, :]
        for b in range(1, N_DEV):
            acc = acc + sc[:, b:b + 1] * dm[b:b + 1, :]
        o_ref[...] = acc

    return _launch(body, name, (L, D // bd, n // bn),
                   [pl.BlockSpec((bd, N_DEV), lambda l, i, j: (i, 0)),
                    pl.BlockSpec((None, N_DEV, bn), lambda l, i, j: (l, 0, j))],
                   pl.BlockSpec((None, bd, bn), lambda l, i, j: (l, i, j)), SDS((L, D, n), F32), (c_cols, dmod),
                   ("parallel", "parallel", "parallel"), comm=comm)


def _mm_in(x, w4, bn, name, gate=None, comm=None):
    M, K = x.shape
    nsh, _, n = w4.shape
    N = nsh * n
    nb = n // bn
    bm = _tile(M, 1024 if gate is None else 512, 16)
    x_spec = pl.BlockSpec((bm, K), lambda j, i: (i, 0))
    w_spec = pl.BlockSpec((None, K, bn), lambda j, i: (j // nb, 0, j % nb))
    o_spec = pl.BlockSpec((bm, bn), lambda j, i: (i, j))
    if gate is None:
        def body(x_ref, w_ref, o_ref):
            o_ref[...] = _dot(x_ref[...], w_ref[...]).astype(BF16)

        return _launch(body, name, (N // bn, M // bm), [x_spec, w_spec], o_spec, SDS((M, N), BF16), (x, w4),
                       ("parallel", "parallel"), comm=comm)

    def body_gated(x_ref, w_ref, g_ref, up_ref, a_ref):
        for rows in _row_parts(bm):
            up = _dot(x_ref[rows, :], w_ref[...])
            g = g_ref[rows, :].astype(F32)
            up_ref[rows, :] = up.astype(BF16)
            a_ref[rows, :] = (g * _sigmoid(g, approx=True) * up).astype(BF16)

    return _launch(body_gated, name, (N // bn, M // bm), [x_spec, w_spec, o_spec], [o_spec, o_spec],
                   [SDS((M, N), BF16)] * 2, (x, w4, gate), ("parallel", "parallel"), comm=comm)


def _mm_out_res(a, w, h, cvec, name, groups=False, comm=None):
    M = a.shape[0]
    N = h.shape[1]
    if groups:
        bn = w.shape[2]
        a_spec = pl.BlockSpec((_tile(M, 512, 16), w.shape[1]), lambda j, i: (i, j))
        w_spec = pl.BlockSpec((None, w.shape[1], bn), lambda j, i: (j, 0, 0))
    else:
        bn = _tile(N, 512, 128)
        a_spec = pl.BlockSpec((_tile(M, 512, 16), a.shape[1]), lambda j, i: (i, 0))
        w_spec = pl.BlockSpec((a.shape[1], bn), lambda j, i: (0, j))
    bm = _tile(M, 512, 16)
    o_spec = pl.BlockSpec((bm, bn), lambda j, i: (i, j))

    def body(a_ref, w_ref, h_ref, c_ref, hn_ref, y_ref):
        y = _dot(a_ref[...], w_ref[...])
        hn_ref[...] = h_ref[...] + c_ref[...] * y
        y_ref[...] = y.astype(BF16)

    return _launch(body, name, (N // bn, M // bm),
                   [a_spec, w_spec, o_spec, pl.BlockSpec((1, bn), lambda j, i: (0, j))], [o_spec, o_spec],
                   [SDS((M, N), F32), SDS((M, N), BF16)], (a, w, h, cvec), ("parallel", "parallel"), comm=comm)


def _mm_nt(dy, w, name, groups=False, swiglu=None, comm=None):
    M = dy.shape[0]
    bm = _tile(M, 1024, 16)
    if groups:
        N = dy.shape[1]
        bn = w.shape[1]
        dy_spec = pl.BlockSpec((bm, w.shape[2]), lambda j, i: (i, j))
        w_spec = pl.BlockSpec((None, bn, w.shape[2]), lambda j, i: (j, 0, 0))
    else:
        N = w.shape[0]
        bn = _tile(N, 512, 128)
        dy_spec = pl.BlockSpec((bm, dy.shape[1]), lambda j, i: (i, 0))
        w_spec = pl.BlockSpec((bn, w.shape[1]), lambda j, i: (j, 0))
    o_spec = pl.BlockSpec((bm, bn), lambda j, i: (i, j))
    if swiglu is None:
        def body(dy_ref, w_ref, o_ref):
            o_ref[...] = _dot_nt(dy_ref[...], w_ref[...]).astype(BF16)

        return _launch(body, name, (N // bn, M // bm), [dy_spec, w_spec], o_spec, SDS((M, N), BF16), (dy, w),
                       ("parallel", "parallel"), comm=comm)

    def body_swiglu(dy_ref, w_ref, g_ref, u_ref, dg_ref, du_ref):
        for rows in _row_parts(bm):
            da = _dot_nt(dy_ref[rows, :], w_ref[...])
            g = g_ref[rows, :].astype(F32)
            sg = _sigmoid(g, approx=True)
            silu = g * sg
            dg_ref[rows, :] = (da * u_ref[rows, :].astype(F32) * (sg + silu * (1.0 - sg))).astype(BF16)
            du_ref[rows, :] = (da * silu).astype(BF16)

    return _launch(body_swiglu, name, (N // bn, M // bm), [dy_spec, w_spec, o_spec, o_spec], [o_spec, o_spec],
                   [SDS((M, N), BF16)] * 2, (dy, w, *swiglu), ("parallel", "parallel"), comm=comm)


def _mm_nt_acc(dx, w4, name, add=None, comm=None):
    M = dx.shape[0]
    nc, K, n = w4.shape
    bm = _tile(M, 512, 16)
    with_add = add is not None

    def body(*refs):
        dx_ref, w_ref = refs[:2]
        o_ref, acc_ref = refs[-2:]
        c = pl.program_id(1)
        s = _dot_nt(dx_ref[...], w_ref[...])

        @pl.when(c == 0)
        def _():
            acc_ref[...] = s + refs[2][...].astype(F32) if with_add else s

        @pl.when(c > 0)
        def _():
            acc_ref[...] += s

        @pl.when(c == nc - 1)
        def _():
            o_ref[...] = acc_ref[...].astype(BF16)

    o_spec = pl.BlockSpec((bm, K), lambda i, c: (i, 0))
    in_specs = [pl.BlockSpec((bm, n), lambda i, c: (i, c)), pl.BlockSpec((None, K, n), lambda i, c: (c, 0, 0))]
    args = [dx, w4]
    if with_add:
        in_specs.append(o_spec)
        args.append(add)
    return _launch(body, name, (M // bm, nc), in_specs, o_spec, SDS((M, K), BF16), args, ("parallel", "arbitrary"),
                   scratch_shapes=[pltpu.VMEM((bm, K), F32)], comm=comm)


def _mm_tn(x, dy, name, shard_cols=None, groups=None, comm=None):
    M, K = x.shape
    N = dy.shape[1]
    bm = _tile(M, 2048, 16)
    if groups is not None:
        kg, ng = K // groups, N // groups
        grid = (groups, 1, M // bm)
        x_spec = pl.BlockSpec((bm, kg), lambda i, j, s: (s, i))
        dy_spec = pl.BlockSpec((bm, ng), lambda i, j, s: (s, i))
        o_spec = pl.BlockSpec((None, kg, ng), lambda i, j, s: (i, 0, 0))
        out_shape = SDS((groups, kg, ng), F32)
    else:
        bko = _tile(K, 1408, 128)
        if shard_cols is not None:
            bn = _tile(shard_cols, 1536, 128)
            nb = shard_cols // bn
            o_spec = pl.BlockSpec((None, bko, bn), lambda i, j, s: (j // nb, i, j % nb))
            out_shape = SDS((N_CHIPS, K, shard_cols), F32)
        else:
            bn = _tile(N, 1024, 128)
            o_spec = pl.BlockSpec((bko, bn), lambda i, j, s: (i, j))
            out_shape = SDS((K, N), F32)
        grid = (K // bko, N // bn, M // bm)
        x_spec = pl.BlockSpec((bm, bko), lambda i, j, s: (s, i))
        dy_spec = pl.BlockSpec((bm, bn), lambda i, j, s: (s, j))

    def body(x_ref, dy_ref, o_ref):
        p = _dot_tn(x_ref[...], dy_ref[...])

        @pl.when(pl.program_id(2) == 0)
        def _():
            o_ref[...] = p

        @pl.when(pl.program_id(2) > 0)
        def _():
            o_ref[...] += p

    return _launch(body, name, grid, [x_spec, dy_spec], o_spec, out_shape, (x, dy),
                   ("parallel", "parallel", "arbitrary"), comm=comm)


def _norm_mod_rows(h, g, scale, shift):
    r = lax.rsqrt(jnp.mean(h * h, axis=-1, keepdims=True) + RMS_EPS)
    return (h * r) * g * (1.0 + scale) + shift


def _vec_spec(D):
    return pl.BlockSpec((1, D), lambda i: (0, 0))


def _norm_mod(h, g, scale, shift, name):
    S, D = h.shape
    bs = _tile(S, 512, 16)

    def body(h_ref, g_ref, sc_ref, sh_ref, u_ref):
        u_ref[...] = _norm_mod_rows(h_ref[...], g_ref[...], sc_ref[...], sh_ref[...]).astype(BF16)

    row = pl.BlockSpec((bs, D), lambda i: (i, 0))
    return pl.pallas_call(
        body, name=name, grid=(S // bs,), in_specs=[row, _vec_spec(D), _vec_spec(D), _vec_spec(D)],
        out_specs=row, out_shape=SDS((S, D), BF16), compiler_params=_cp("parallel"))(h, g, scale, shift)


def _rows_back(x, n):
    return pltpu.roll(x, n, 0)


def _rows_ahead(x, n):
    return pltpu.roll(x, x.shape[0] - n, 0)


def _window_sums(x, w, shift):
    n = 1
    while n < w:
        x = x + shift(x, n)
        n *= 2
    return x


def _pool_pre(h, g, scale, shift, name):
    S, D = h.shape
    ng = len(POOL_WINDOWS)
    pg = D // ng
    bs = _tile(S, 256, POOL_HALO)
    hb = bs // POOL_HALO

    def body(h_ref, hh_ref, g_ref, sc_ref, sh_ref, o_ref):
        i = pl.program_id(0)
        u = _norm_mod_rows(h_ref[...], g_ref[...], sc_ref[...], sh_ref[...])
        uh = _norm_mod_rows(hh_ref[...], g_ref[...], sc_ref[...], sh_ref[...])
        uh = jnp.where(i == 0, 0.0, uh)
        ue = jnp.concatenate([uh, u], axis=0)
        t = i * bs + lax.broadcasted_iota(jnp.int32, (bs, 1), 0)
        for gi, w in enumerate(POOL_WINDOWS):
            cols = slice(gi * pg, (gi + 1) * pg)
            inv = 1.0 / jnp.minimum(t + 1, w).astype(F32)
            sums = _window_sums(ue[:, cols], w, _rows_back)[POOL_HALO:]
            o_ref[:, cols] = (sums * inv - u[:, cols]).astype(BF16)

    row = pl.BlockSpec((bs, D), lambda i: (i, 0))
    halo = pl.BlockSpec((POOL_HALO, D), lambda i: (jnp.maximum(i * hb - 1, 0), 0))
    return pl.pallas_call(
        body, name=name, grid=(S // bs,),
        in_specs=[row, halo, _vec_spec(D), _vec_spec(D), _vec_spec(D)],
        out_specs=row, out_shape=SDS((S, D), BF16), compiler_params=_cp("parallel"))(h, h, g, scale, shift)


def _pool_post(dd, name):
    S, D = dd.shape
    ng = len(POOL_WINDOWS)
    pg = D // ng
    bs = _tile(S, 256, POOL_HALO)
    hb = bs // POOL_HALO
    nblk = S // bs

    def body(d_ref, dn_ref, o_ref):
        i = pl.program_id(0)
        d = d_ref[...].astype(F32)
        dn = jnp.where(i == nblk - 1, 0.0, dn_ref[...].astype(F32))
        de = jnp.concatenate([d, dn], axis=0)
        t = i * bs + lax.broadcasted_iota(jnp.int32, (bs + POOL_HALO, 1), 0)
        for gi, w in enumerate(POOL_WINDOWS):
            cols = slice(gi * pg, (gi + 1) * pg)
            inv = 1.0 / jnp.minimum(t + 1, w).astype(F32)
            sums = _window_sums(de[:, cols] * inv, w, _rows_ahead)[:bs]
            o_ref[:, cols] = (sums - d[:, cols]).astype(BF16)

    row = pl.BlockSpec((bs, D), lambda i: (i, 0))
    nxt = pl.BlockSpec((POOL_HALO, D), lambda i: (jnp.minimum((i + 1) * hb, S // POOL_HALO - 1), 0))
    return pl.pallas_call(
        body, name=name, grid=(nblk,), in_specs=[row, nxt], out_specs=row, out_shape=SDS((S, D), BF16),
        compiler_params=_cp("parallel"))(dd, dd)


def _colsum(x):
    return jnp.sum(x, axis=0, keepdims=True)


def _accumulate_rows(st_ref, rows, first):
    @pl.when(first)
    def _():
        st_ref[...] = jnp.zeros_like(st_ref)

    for r, row in enumerate(rows):
        st_ref[r:r + 1, :] += row


def _norm_bwd(h, g, scale, du, dh_out, name, prev=None, comm=None):
    S, D = h.shape
    bs = _tile(S, 256, 16)
    with_prev = prev is not None

    def body(*refs):
        h_ref, g_ref, sc_ref, du_ref, dho_ref = refs[:5]
        if with_prev:
            y_ref, cv_ref, dh_ref, dy_ref, st_ref = refs[5:]
        else:
            dh_ref, st_ref = refs[5:]
        hh = h_ref[...]
        du_ = du_ref[...].astype(F32)
        r = lax.rsqrt(jnp.mean(hh * hh, axis=-1, keepdims=True) + RMS_EPS)
        xhat = hh * r
        dn = du_ * (1.0 + sc_ref[...])
        dxhat = dn * g_ref[...]
        dh = dho_ref[...] + r * (dxhat - xhat * jnp.mean(dxhat * xhat, axis=-1, keepdims=True))
        dh_ref[...] = dh
        rows = [_colsum(du_), _colsum(du_ * (xhat * g_ref[...])), _colsum(dn * xhat)]
        if with_prev:
            dy_ref[...] = (dh * cv_ref[...]).astype(BF16)
            rows.append(_colsum(dh * y_ref[...].astype(F32)))
        _accumulate_rows(st_ref, rows, pl.program_id(0) == 0)

    row = pl.BlockSpec((bs, D), lambda i: (i, 0))
    st_spec = pl.BlockSpec((8, D), lambda i: (0, 0))
    in_specs = [row, _vec_spec(D), _vec_spec(D), row, row]
    args = [h, g, scale, du, dh_out]
    out_specs, out_shape = [row], [SDS((S, D), F32)]
    if with_prev:
        in_specs += [row, _vec_spec(D)]
        args += list(prev)
        out_specs.append(row)
        out_shape.append(SDS((S, D), BF16))
    out_specs.append(st_spec)
    out_shape.append(SDS((8, D), F32))
    outs, brought = _launch(body, name, (S // bs,), in_specs, out_specs, out_shape, args, ("arbitrary",), comm=comm)
    return (outs, brought) if comm else outs


def _loss_head(h, g, target, y, cvec, name):
    S, D = h.shape
    bs = _tile(S, 256, 16)

    def body(h_ref, g_ref, t_ref, y_ref, cv_ref, dh_ref, dy_ref, st_ref):
        hh = h_ref[...]
        r = lax.rsqrt(jnp.mean(hh * hh, axis=-1, keepdims=True) + RMS_EPS)
        xhat = hh * r
        err = xhat * g_ref[...] - t_ref[...]
        dout = err * (1.0 / D)
        dxhat = dout * g_ref[...]
        dh = r * (dxhat - xhat * jnp.mean(dxhat * xhat, axis=-1, keepdims=True))
        dh_ref[...] = dh
        dy_ref[...] = (dh * cv_ref[...]).astype(BF16)
        rows = [_colsum(dout * xhat), _colsum(dh * y_ref[...].astype(F32)), _colsum(err * err) * (0.5 / D)]
        _accumulate_rows(st_ref, rows, pl.program_id(0) == 0)

    row = pl.BlockSpec((bs, D), lambda i: (i, 0))
    return pl.pallas_call(
        body, name=name, grid=(S // bs,), in_specs=[row, _vec_spec(D), row, row, _vec_spec(D)],
        out_specs=[row, row, pl.BlockSpec((8, D), lambda i: (0, 0))],
        out_shape=[SDS((S, D), F32), SDS((S, D), BF16), SDS((8, D), F32)],
        compiler_params=_cp("arbitrary"))(h, g, target, y, cvec)


def _sum_all(x, name):
    def body(x_ref, o_ref):
        o_ref[...] = jnp.sum(jnp.sum(x_ref[...], axis=1, keepdims=True), axis=0, keepdims=True)

    return pl.pallas_call(body, name=name, out_shape=SDS((1, 1), F32), in_specs=[VMEM_WHOLE],
                          out_specs=VMEM_WHOLE)(x)


def _conv_mid(u3, cw, name):
    S, D3 = u3.shape
    D = D3 // 3
    cb = _tile(D, 512, 128)
    nj = D // cb
    bs = _tile(S, 256, CONV_HALO)
    hb = bs // CONV_HALO

    def body(b_ref, c_ref, v_ref, ch_ref, vh_ref, w_ref, o_ref):
        i = pl.program_id(0)
        z = c_ref[...].astype(F32) * v_ref[...].astype(F32)
        zh = jnp.where(i == 0, 0.0, ch_ref[...].astype(F32) * vh_ref[...].astype(F32))
        ze = jnp.concatenate([zh, z], axis=0)
        w = w_ref[...]
        zc = w[2:3] * z
        zc = zc + w[1:2] * _rows_back(ze, 1)[CONV_HALO:]
        zc = zc + w[0:1] * _rows_back(ze, 2)[CONV_HALO:]
        o_ref[...] = (b_ref[...].astype(F32) * zc).astype(BF16)

    def blk(off):
        return pl.BlockSpec((bs, cb), lambda i, j: (i, off + j))

    def halo(off):
        return pl.BlockSpec((CONV_HALO, cb), lambda i, j: (jnp.maximum(i * hb - 1, 0), off + j))

    return pl.pallas_call(
        body, name=name, grid=(S // bs, nj),
        in_specs=[blk(0), blk(nj), blk(2 * nj), halo(nj), halo(2 * nj), pl.BlockSpec((3, cb), lambda i, j: (0, j))],
        out_specs=pl.BlockSpec((bs, cb), lambda i, j: (i, j)), out_shape=SDS((S, D), BF16),
        compiler_params=_cp("parallel", "parallel"))(u3, u3, u3, u3, u3, cw)


def _conv_mid_bwd(u3, da, cw, name):
    S, D3 = u3.shape
    D = D3 // 3
    cb = _tile(D, 512, 128)
    nj = D // cb
    bs = _tile(S, 256, CONV_HALO)
    hb = bs // CONV_HALO
    nblk = S // bs
    last_halo = S // CONV_HALO - 1

    def body(b_ref, c_ref, v_ref, ch_ref, vh_ref, bn_ref, da_ref, dan_ref, w_ref, db_ref, dc_ref, dv_ref, dw_ref):
        i = pl.program_id(0)
        c = c_ref[...].astype(F32)
        v = v_ref[...].astype(F32)
        b = b_ref[...].astype(F32)
        da_ = da_ref[...].astype(F32)
        z = c * v
        zh = jnp.where(i == 0, 0.0, ch_ref[...].astype(F32) * vh_ref[...].astype(F32))
        ze = jnp.concatenate([zh, z], axis=0)
        z1 = _rows_back(ze, 1)[CONV_HALO:]
        z2 = _rows_back(ze, 2)[CONV_HALO:]
        w = w_ref[...]
        zc = w[2:3] * z + w[1:2] * z1 + w[0:1] * z2
        db_ref[...] = (da_ * zc).astype(BF16)
        dzc = da_ * b
        dzn = jnp.where(i == nblk - 1, 0.0, dan_ref[...].astype(F32) * bn_ref[...].astype(F32))
        dze = jnp.concatenate([dzc, dzn], axis=0)
        dz = w[2:3] * dzc
        dz = dz + w[1:2] * _rows_ahead(dze, 1)[:bs]
        dz = dz + w[0:1] * _rows_ahead(dze, 2)[:bs]
        dc_ref[...] = (dz * v).astype(BF16)
        dv_ref[...] = (dz * c).astype(BF16)
        dw_ref[...] = jnp.zeros_like(dw_ref)
        dw_ref[0:1, :] = _colsum(dzc * z2)
        dw_ref[1:2, :] = _colsum(dzc * z1)
        dw_ref[2:3, :] = _colsum(dzc * z)

    def blk(off):
        return pl.BlockSpec((bs, cb), lambda i, j: (i, off + j))

    def halo(off):
        return pl.BlockSpec((CONV_HALO, cb), lambda i, j: (jnp.maximum(i * hb - 1, 0), off + j))

    def nxt(off):
        return pl.BlockSpec((CONV_HALO, cb), lambda i, j: (jnp.minimum((i + 1) * hb, last_halo), off + j))

    o_spec = pl.BlockSpec((bs, cb), lambda i, j: (i, j))
    return pl.pallas_call(
        body, name=name, grid=(nblk, nj),
        in_specs=[blk(0), blk(nj), blk(2 * nj), halo(nj), halo(2 * nj), nxt(0), o_spec, nxt(0),
                  pl.BlockSpec((3, cb), lambda i, j: (0, j))],
        out_specs=[o_spec, o_spec, o_spec, pl.BlockSpec((None, 8, cb), lambda i, j: (i, 0, j))],
        out_shape=[SDS((S, D), BF16)] * 3 + [SDS((nblk, 8, D), F32)],
        compiler_params=_cp("parallel", "parallel"))(u3, u3, u3, u3, u3, u3, da, da, cw)


def _sum_lead(x, name):
    n, r, C = x.shape

    def body(x_ref, o_ref):
        @pl.when(pl.program_id(0) == 0)
        def _():
            o_ref[...] = x_ref[...]

        @pl.when(pl.program_id(0) > 0)
        def _():
            o_ref[...] += x_ref[...]

    return pl.pallas_call(
        body, name=name, grid=(n,), in_specs=[pl.BlockSpec((None, r, C), lambda i: (i, 0, 0))],
        out_specs=pl.BlockSpec((r, C), lambda i: (0, 0)), out_shape=SDS((r, C), F32),
        compiler_params=_cp("arbitrary"))(x)


def _log_sigmoids(z):
    lb = jnp.minimum(z, 0.0) - jnp.log(1.0 + jnp.exp(-jnp.abs(z)))
    return lb, lb - z


def _attn_blocks(S):
    bk = _tile(S, 256, 128)
    bq = 4 * bk if S % (4 * bk) == 0 else bk
    return bq, bk


def _tri(n, pred):
    rowi = lax.broadcasted_iota(jnp.int32, (n, n), 0)
    coli = lax.broadcasted_iota(jnp.int32, (n, n), 1)
    return jnp.where(pred(rowi, coli), 1.0, 0.0).astype(BF16)


def _causal_mask(bq, bk, m):
    rowi = lax.broadcasted_iota(jnp.int32, (bq, bk), 0)
    coli = lax.broadcasted_iota(jnp.int32, (bq, bk), 1)
    return m * bk + coli < rowi


def _sb_attention(qkv, name, comm=None):
    S, D3 = qkv.shape
    D = D3 // 3
    H = D // HEAD_DIM
    bq, bk = _attn_blocks(S)
    nq, r = S // bq, bq // bk
    unroll = r
    scale = HEAD_DIM ** -0.5

    def body(q_ref, k_ref, v_ref, o_ref, lt_ref):
        i = pl.program_id(1)
        q = q_ref[...]
        after = _tri(bk, lambda j, s: j > s)

        def block(kb, carry, acc, causal):
            rows = pl.ds(pl.multiple_of(kb * bk, bk), bk)
            z = _dot_nt(q, k_ref[rows, :]) * scale
            lb, l1 = _log_sigmoids(z)
            if causal is not None:
                l1 = jnp.where(causal, l1, 0.0)
            a = jnp.exp(lb + (_dot(l1.astype(BF16), after) + carry))
            if causal is not None:
                a = jnp.where(causal, a, 0.0)
            acc = acc + _dot(a.astype(BF16), v_ref[rows, :])
            return carry + jnp.sum(l1, axis=1, keepdims=True), acc

        carry, acc = jnp.zeros((bq, 1), F32), jnp.zeros((bq, HEAD_DIM), F32)
        for m in reversed(range(r)):
            carry, acc = block(i * r + m, carry, acc, _causal_mask(bq, bk, m))

        def step(j, ca):
            for n in range(unroll):
                ca = block(i * r - 1 - unroll * j - n, ca[0], ca[1], None)
            return ca

        carry, acc = lax.fori_loop(0, i * (r // unroll), step, (carry, acc))
        o_ref[...] = acc.astype(BF16)
        lt_ref[...] = jnp.broadcast_to(carry, (bq, HEAD_DIM))

    head_rows = lambda off: pl.BlockSpec((S, HEAD_DIM), lambda hd, i: (0, off + hd))
    blk = pl.BlockSpec((bq, HEAD_DIM), lambda hd, i: (i, hd))
    return _launch(body, name, (H, nq), [blk, head_rows(H), head_rows(2 * H)], [blk, blk],
                   [SDS((S, D), BF16), SDS((S, D), F32)], (qkv, qkv, qkv), ("parallel", "arbitrary"), comm=comm)


def _sb_attention_bwd(qkv, ltot, do, name):
    S, D3 = qkv.shape
    D = D3 // 3
    H = D // HEAD_DIM
    bq, bk = _attn_blocks(S)
    nq, r = S // bq, bq // bk
    unroll = r
    scale = HEAD_DIM ** -0.5

    def body(q_ref, k_ref, v_ref, lt_ref, do_ref, dq_ref, dk_ref, dv_ref, dkt_acc, dvt_acc):
        i = pl.program_id(1)
        q = q_ref[...]
        do_ = do_ref[...]
        qt = jnp.transpose(q.astype(F32)).astype(BF16)
        dot = jnp.transpose(do_.astype(F32)).astype(BF16)
        lt = lt_ref[:, 0:1]
        after = _tri(bk, lambda j, s: j > s)
        before = _tri(bk, lambda j, s: j < s)

        @pl.when(i == 0)
        def _():
            dkt_acc[...] = jnp.zeros_like(dkt_acc)
            dvt_acc[...] = jnp.zeros_like(dvt_acc)

        def block(kb, c1, ce, dq, causal):
            rows = pl.ds(pl.multiple_of(kb * bk, bk), bk)
            k = k_ref[rows, :]
            v = v_ref[rows, :]
            z = _dot_nt(q, k) * scale
            lb, l1 = _log_sigmoids(z)
            sig = jnp.exp(lb)
            if causal is not None:
                l1 = jnp.where(causal, l1, 0.0)
            c1 = c1 + jnp.sum(l1, axis=1, keepdims=True)
            a = jnp.exp(lb + (_dot(l1.astype(BF16), after) + (lt - c1)))
            if causal is not None:
                a = jnp.where(causal, a, 0.0)
            e = a * _dot_nt(do_, v)
            p = _dot(e.astype(BF16), before) + ce
            dz = e - sig * (e + p)
            if causal is not None:
                dz = jnp.where(causal, dz, 0.0)
            dzb = dz.astype(BF16)
            dkt_acc[kb] += _dot(qt, dzb)
            dvt_acc[kb] += _dot(dot, a.astype(BF16))
            dq = dq + _dot(dzb, k)
            return c1, ce + jnp.sum(e, axis=1, keepdims=True), dq

        def step(j, st):
            for n in range(unroll):
                st = block(unroll * j + n, st[0], st[1], st[2], None)
            return st

        zero = jnp.zeros((bq, 1), F32)
        c1, ce, dq = lax.fori_loop(0, i * (r // unroll), step, (zero, zero, jnp.zeros((bq, HEAD_DIM), F32)))
        for m in range(r):
            c1, ce, dq = block(i * r + m, c1, ce, dq, _causal_mask(bq, bk, m))
        dq_ref[...] = (dq * scale).astype(BF16)

        @pl.when(i == nq - 1)
        def _():
            def flush(kb, _):
                rows = pl.ds(pl.multiple_of(kb * bk, bk), bk)
                dk_ref[rows, :] = (jnp.transpose(dkt_acc[kb]) * scale).astype(BF16)
                dv_ref[rows, :] = jnp.transpose(dvt_acc[kb]).astype(BF16)
                return 0

            lax.fori_loop(0, S // bk, flush, 0)

    head_rows = lambda off: pl.BlockSpec((S, HEAD_DIM), lambda hd, i: (0, off + hd))
    blk = pl.BlockSpec((bq, HEAD_DIM), lambda hd, i: (i, hd))
    return pl.pallas_call(
        body, name=name, grid=(H, nq), in_specs=[blk, head_rows(H), head_rows(2 * H), blk, blk],
        out_specs=[blk, head_rows(0), head_rows(0)], out_shape=[SDS((S, D), BF16)] * 3,
        scratch_shapes=[pltpu.VMEM((S // bk, HEAD_DIM, bk), F32), pltpu.VMEM((S // bk, HEAD_DIM, bk), F32)],
        compiler_params=_cp("arbitrary", "arbitrary"))(qkv, qkv, qkv, ltot, do)


def kernel(x, c, norm_mix_g, norm_ffn_g, w_mod, b_mod, pool_w, pool_scale, conv_w_in, conv_w, conv_w_out, sb_w_qkv, sb_w_o, ffn_w_gate, ffn_w_up, ffn_w_down, final_g, loss_target, m_norm_mix_g, m_norm_ffn_g, m_w_mod, m_b_mod, m_pool_w, m_pool_scale, m_conv_w_in, m_conv_w, m_conv_w_out, m_sb_w_qkv, m_sb_w_o, m_ffn_w_gate, m_ffn_w_up, m_ffn_w_down, m_final_g, v_norm_mix_g, v_norm_ffn_g, v_w_mod, v_b_mod, v_pool_w, v_pool_scale, v_conv_w_in, v_conv_w, v_conv_w_out, v_sb_w_qkv, v_sb_w_o, v_ffn_w_gate, v_ffn_w_up, v_ffn_w_down, v_final_g):
    S, D = x.shape[1], x.shape[2]
    L = norm_mix_g.shape[0]
    nmod = w_mod.shape[2]
    nf = ffn_w_gate.shape[2]
    n3 = conv_w_in.shape[2]
    nd = conv_w_out.shape[1]
    cb = n3 // 3
    ng = pool_w.shape[1]
    pg = pool_w.shape[3]
    n_pool = pool_w.shape[0]
    assert D % HEAD_DIM == 0 and S % 256 == 0 and nd == cb and N_CHIPS * nd == D and pg * ng == D

    mx, my, mc = lax.axis_index("x"), lax.axis_index("y"), lax.axis_index("c")
    chip = 2 * mx + my
    dev = 2 * chip + mc
    hx, ht = x[0], loss_target[0]

    c_all = _allgather8(jnp.broadcast_to(c, (8, D)), "gather_c").reshape(N_DEV, 8, D)[:, 0]
    c_rows = jnp.concatenate([c_all, jnp.zeros((8, D), F32)], axis=0)
    b_cols = lax.dynamic_slice_in_dim(b_mod, chip * nmod, nmod, axis=1).reshape(L, 1, nmod)
    mod_cols = _mod_fwd(c_rows, w_mod, b_cols, "mod_fwd")
    mod_all = _allgather8(mod_cols.reshape(L * 16, nmod), "gather_mod").reshape(N_CHIPS, 2, L, 16, nmod)
    mod = lax.dynamic_index_in_dim(mod_all[:, 0], dev, axis=2, keepdims=False)
    mod = jnp.transpose(mod, (1, 0, 2)).reshape(L, N_MOD, 1, D)

    bf = lambda w: w.astype(BF16)
    n_conv, n_sb = conv_w_in.shape[0], sb_w_qkv.shape[0]

    def mixer_shards(l):
        kind, j = l % 3, l // 3
        if kind == 0:
            return [bf(pool_w[j]).reshape(ng * (pg // N_CHIPS), pg)]
        return [bf(conv_w_in[j]), bf(conv_w_out[j])] if kind == 1 else [bf(sb_w_qkv[j]), bf(sb_w_o[j])]

    def mixer_weights(l, got):
        if l % 3 == 0:
            return [jnp.transpose(got[0].reshape(N_CHIPS, ng, pg // N_CHIPS, pg), (1, 0, 2, 3)).reshape(ng, pg, pg)]
        return [got[0], got[1].reshape(D, D)]

    first = mixer_shards(0)
    got = _run_comm(_GatherShards(first + [bf(ffn_w_gate[0]), bf(ffn_w_up[0]), bf(ffn_w_down[0])]), "gather_layer0")
    w_mix = {0: mixer_weights(0, got[:len(first)])}
    w_gate, w_up = {0: got[-3]}, {0: got[-2]}
    w_down = {0: got[-1].reshape(N_CHIPS * nf, D)}
    taps_cols = jnp.concatenate([pool_scale, conv_w.reshape(-1, nd)], axis=0)
    n_small = taps_cols.shape[0]
    small_rows = jnp.concatenate([taps_cols, jnp.zeros((16 - n_small, nd), F32)], axis=0)
    small_all = _allgather8(small_rows, "gather_small").reshape(N_CHIPS, 2, 16, nd)[:, 0]
    small_full = jnp.transpose(small_all, (1, 0, 2)).reshape(16, D)
    pool_scale_full = small_full[:n_pool]
    conv_taps_full = small_full[n_pool:n_small].reshape(n_conv, 3, D)

    saved = []
    h = hx
    for l in range(L):
        kind, j = l % 3, l // 3
        sh_m, sc_m, gt_m, sh_f, sc_f, gt_f = (mod[l, r] for r in range(N_MOD))
        gm = norm_mix_g[l].reshape(1, D)
        gf = norm_ffn_g[l].reshape(1, D)
        s = {"h_in": h}
        if kind == 0:
            s["diff"] = _pool_pre(h, gm, sc_m, sh_m, f"pool_pre{l}")
            s["cvec_m"] = gt_m * pool_scale_full[j].reshape(1, D)
            (h, s["y_m"]), _ = _mm_out_res(s["diff"], w_mix[l][0], h, s["cvec_m"], f"pool_mm{l}", groups=True)
        else:
            down_comm = None if l in w_down else _GatherShards([bf(ffn_w_down[l])])
            s["u"] = _norm_mod(h, gm, sc_m, sh_m, f"norm_mix{l}")
            s["cvec_m"] = gt_m
            if kind == 1:
                s["u3"], got = _mm_in(s["u"], w_mix[l][0], n3, f"conv_in{l}", comm=down_comm)
                s["a_m"] = _conv_mid(s["u3"], conv_taps_full[j], f"conv_mid{l}")
                (h, s["y_m"]), _ = _mm_out_res(s["a_m"], w_mix[l][1], h, gt_m, f"conv_out{l}")
            else:
                s["qkv"], got = _mm_in(s["u"], w_mix[l][0], n3, f"sb_qkv{l}", comm=down_comm)
                ahead = l + 1 < L and (l + 1) % 3 == 0
                (s["o"], s["ltot"]), got_ahead = _sb_attention(
                    s["qkv"], f"sb_attn{l}", comm=_GatherShards([bf(ffn_w_down[l + 1])]) if ahead else None)
                if ahead:
                    w_down[l + 1] = got_ahead[0].reshape(N_CHIPS * nf, D)
                (h, s["y_m"]), _ = _mm_out_res(s["o"], w_mix[l][1], h, gt_m, f"sb_out{l}")
            if down_comm:
                w_down[l] = got[0].reshape(N_CHIPS * nf, D)
        s["h_mid"] = h
        s["u2"] = _norm_mod(h, gf, sc_f, sh_f, f"norm_ffn{l}")
        more = l + 1 < L
        s["gate"], got = _mm_in(s["u2"], w_gate[l], nf, f"ffn_gate{l}",
                                comm=_GatherShards(mixer_shards(l + 1)) if more else None)
        if more:
            w_mix[l + 1] = mixer_weights(l + 1, got)
        (s["up"], s["a_f"]), got = _mm_in(s["u2"], w_up[l], nf, f"ffn_up{l}", gate=s["gate"],
                                          comm=_GatherShards([bf(ffn_w_gate[l + 1])]) if more else None)
        if more:
            w_gate[l + 1] = got[0]
        late = [bf(ffn_w_up[l + 1])] if more else []
        if more and (l + 1) % 3 == 0 and l + 1 not in w_down:
            late.append(bf(ffn_w_down[l + 1]))
        (h, s["y_f"]), got = _mm_out_res(s["a_f"], w_down[l], h, gt_f, f"ffn_down{l}",
                                         comm=_GatherShards(late) if more else None)
        if more:
            w_up[l + 1] = got[0]
            if len(late) > 1:
                w_down[l + 1] = got[1].reshape(N_CHIPS * nf, D)
        saved.append(s)

    gt_f_last = mod[L - 1, 5]
    dh, dy, st = _loss_head(h, final_g.reshape(1, D), ht, saved[-1]["y_f"], gt_f_last, "loss_head")
    loss = lax.psum(_sum_all(st[2:3], "loss_sum")[0, 0], ("x", "y", "c"))
    d_final_g = st[0:1]
    p_gate_f = st[1:2]
    d_norm_mix, d_norm_ffn = [None] * L, [None] * L
    d_mod = [[None] * N_MOD for _ in range(L)]
    d_pool_scale, d_taps = [None] * n_pool, [None] * n_conv
    big = {}
    ffn_reduce = mix_reduce = None
    for l in reversed(range(L)):
        kind, j = l % 3, l // 3
        s = saved[l]
        sh_m, sc_m, gt_m, sh_f, sc_f, gt_f = (mod[l, r] for r in range(N_MOD))
        gm = norm_mix_g[l].reshape(1, D)
        gf = norm_ffn_g[l].reshape(1, D)
        d_mod[l][5] = p_gate_f
        (dgate, dup), brought = _mm_nt(dy, w_down[l], f"ffn_down_bwd{l}", swiglu=(s["gate"], s["up"]),
                                       comm=ffn_reduce.scatter([0, 1]) if ffn_reduce else None)
        gw_down, brought_down = _mm_tn(s["a_f"], dy, f"ffn_down_wgrad{l}",
                                       comm=ffn_reduce.scatter([2]) if ffn_reduce else None)
        if ffn_reduce:
            big[("ffn", l + 1)] = ffn_reduce.finish(list(brought) + list(brought_down))
        gw_gate, brought = _mm_tn(s["u2"], dgate, f"ffn_gate_wgrad{l}", shard_cols=nf,
                                  comm=mix_reduce.scatter() if mix_reduce else None)
        if mix_reduce:
            big[("mix", l + 1)] = mix_reduce.finish(brought)
        gw_up, _ = _mm_tn(s["u2"], dup, f"ffn_up_wgrad{l}", shard_cols=nf)
        ffn_reduce = _Reduce([gw_gate, gw_up, gw_down.reshape(N_CHIPS, nf, D)], f"reduce_ffn{l}")
        du2, brought = _mm_nt_acc(dgate, w_gate[l], f"ffn_gate_bwd{l}", comm=ffn_reduce.swap)
        ffn_reduce.swapped(brought)
        mix_reduce = None
        if l > 0:
            du2, _ = _mm_nt_acc(dup, w_up[l], f"ffn_up_bwd{l}", add=du2)
            dh, dy, st = _norm_bwd(s["h_mid"], gf, sc_f, du2, dh, f"norm_ffn_bwd{l}", prev=(s["y_m"], s["cvec_m"]))
        else:
            du2, brought = _mm_nt_acc(dup, w_up[l], f"ffn_up_bwd{l}", add=du2, comm=ffn_reduce.scatter([0, 1]))
            (dh, dy, st), brought_down = _norm_bwd(s["h_mid"], gf, sc_f, du2, dh, f"norm_ffn_bwd{l}",
                                                   prev=(s["y_m"], s["cvec_m"]), comm=ffn_reduce.scatter([2]))
            big[("ffn", l)] = ffn_reduce.finish(list(brought) + list(brought_down))
        d_mod[l][3], d_mod[l][4], d_norm_ffn[l] = st[0:1], st[1:2], st[2:3]
        p_mix = st[3:4]
        if kind == 0:
            d_mod[l][2] = p_mix * pool_scale_full[j].reshape(1, D)
            d_pool_scale[j] = p_mix * gt_m
            dd, _ = _mm_nt(dy, w_mix[l][0], f"pool_mm_bwd{l}", groups=True)
            big[("pool", j)], _ = _mm_tn(s["diff"], dy, f"pool_wgrad{l}", groups=ng)
            du = _pool_post(dd, f"pool_post{l}")
        elif kind == 1:
            d_mod[l][2] = p_mix
            da, _ = _mm_nt(dy, w_mix[l][1], f"conv_out_bwd{l}")
            gw_out, _ = _mm_tn(s["a_m"], dy, f"conv_out_wgrad{l}")
            db, dc, dv, dtap = _conv_mid_bwd(s["u3"], da, conv_taps_full[j], f"conv_mid_bwd{l}")
            d_taps[j] = _sum_lead(dtap, f"conv_tap_sum{l}")[0:3]
            du3 = jnp.concatenate([db, dc, dv], axis=1)
            gw_in, _ = _mm_tn(s["u"], du3, f"conv_in_wgrad{l}", shard_cols=n3)
            mix_reduce = _Reduce([gw_in, gw_out.reshape(N_CHIPS, nd, D)], f"reduce_conv{l}")
            du, brought = _mm_nt_acc(du3, w_mix[l][0], f"conv_in_bwd{l}", comm=mix_reduce.swap)
            mix_reduce.swapped(brought)
        else:
            d_mod[l][2] = p_mix
            do, _ = _mm_nt(dy, w_mix[l][1], f"sb_out_bwd{l}")
            gw_o, _ = _mm_tn(s["o"], dy, f"sb_out_wgrad{l}")
            dq, dk, dv = _sb_attention_bwd(s["qkv"], s["ltot"], do, f"sb_attn_bwd{l}")
            dqkv = jnp.concatenate([dq, dk, dv], axis=1)
            gw_qkv, _ = _mm_tn(s["u"], dqkv, f"sb_qkv_wgrad{l}", shard_cols=n3)
            mix_reduce = _Reduce([gw_qkv, gw_o.reshape(N_CHIPS, nd, D)], f"reduce_sb{l}")
            du, brought = _mm_nt_acc(dqkv, w_mix[l][0], f"sb_qkv_bwd{l}", comm=mix_reduce.swap)
            mix_reduce.swapped(brought)
        if l > 0:
            prev = (saved[l - 1]["y_f"], mod[l - 1, 5])
            dh, dy, st = _norm_bwd(s["h_in"], gm, sc_m, du, dh, f"norm_mix_bwd{l}", prev=prev)
            p_gate_f = st[3:4]
        else:
            dh, st = _norm_bwd(s["h_in"], gm, sc_m, du, dh, f"norm_mix_bwd{l}")
        d_mod[l][0], d_mod[l][1], d_norm_mix[l] = st[0:1], st[1:2], st[2:3]
    grad_x = dh.reshape(1, S, D)
    if mix_reduce:
        big[("mix", 0)] = mix_reduce.finish()

    gw_pool = jnp.stack([big[("pool", j)] for j in range(n_pool)])
    gw_pool = jnp.transpose(gw_pool.reshape(n_pool, ng, N_CHIPS, pg // N_CHIPS, pg), (2, 0, 1, 3, 4))
    pool_reduce = _Reduce([gw_pool.reshape(N_CHIPS, n_pool * ng * (pg // N_CHIPS), pg)], "reduce_pool")
    (g_pool,) = pool_reduce.swapped().finish()

    rows = [d_final_g] + d_norm_mix + d_norm_ffn + [r for l in range(L) for r in d_mod[l]] + d_pool_scale
    rows += [d_taps[j] for j in range(n_conv)]
    vec = jnp.concatenate(rows, axis=0)
    n_rows = vec.shape[0]
    pad = -n_rows % 8
    vec = jnp.concatenate([vec, jnp.zeros((pad, D), F32)], axis=0) if pad else vec
    vec_all = _allgather8(vec, "gather_small_grads").reshape(N_DEV, n_rows + pad, D)
    tot = _sum_devices(vec_all, "sum_small_grads")
    r0 = 1 + 2 * L
    g_final = tot[0]
    g_norm_mix = tot[1:1 + L]
    g_norm_ffn = tot[1 + L:r0]
    g_b_mod = tot[r0:r0 + N_MOD * L].reshape(L, N_MOD * D)
    r1 = r0 + N_MOD * L
    g_pool_scale = lax.dynamic_slice_in_dim(tot[r1:r1 + n_pool], chip * nd, nd, axis=1)
    g_taps = lax.dynamic_slice_in_dim(tot[r1 + n_pool:r1 + n_pool + 3 * n_conv], chip * nd, nd, axis=1)
    g_conv_w = g_taps.reshape(conv_w.shape)
    dmod_all = vec_all[:, r0:r1].reshape(N_DEV, L, N_MOD * D)
    dmod_cols = jnp.transpose(lax.dynamic_slice_in_dim(dmod_all, chip * nmod, nmod, axis=2), (1, 0, 2))
    g_w_mod, _ = _mod_wgrad(jnp.transpose(c_all), dmod_cols, "mod_wgrad")

    g_ffn_gate = jnp.stack([big[("ffn", l)][0] for l in range(L)])
    g_ffn_up = jnp.stack([big[("ffn", l)][1] for l in range(L)])
    g_ffn_down = jnp.stack([big[("ffn", l)][2] for l in range(L)])
    g_conv_in = jnp.stack([big[("mix", 3 * j + 1)][0] for j in range(n_conv)])
    g_conv_out = jnp.stack([big[("mix", 3 * j + 1)][1] for j in range(n_conv)])
    g_sb_qkv = jnp.stack([big[("mix", 3 * j + 2)][0] for j in range(n_sb)])
    g_sb_o = jnp.stack([big[("mix", 3 * j + 2)][1] for j in range(n_sb)])
    g_pool_w = g_pool.reshape(pool_w.shape)

    grads = [g_norm_mix, g_norm_ffn, g_w_mod, g_b_mod, g_pool_w, g_pool_scale, g_conv_in, g_conv_w, g_conv_out,
             g_sb_qkv, g_sb_o, g_ffn_gate, g_ffn_up, g_ffn_down, g_final]
    weights = [norm_mix_g, norm_ffn_g, w_mod, b_mod, pool_w, pool_scale, conv_w_in, conv_w, conv_w_out,
               sb_w_qkv, sb_w_o, ffn_w_gate, ffn_w_up, ffn_w_down, final_g]
    ms = [m_norm_mix_g, m_norm_ffn_g, m_w_mod, m_b_mod, m_pool_w, m_pool_scale, m_conv_w_in, m_conv_w, m_conv_w_out,
          m_sb_w_qkv, m_sb_w_o, m_ffn_w_gate, m_ffn_w_up, m_ffn_w_down, m_final_g]
    vs = [v_norm_mix_g, v_norm_ffn_g, v_w_mod, v_b_mod, v_pool_w, v_pool_scale, v_conv_w_in, v_conv_w, v_conv_w_out,
          v_sb_w_qkv, v_sb_w_o, v_ffn_w_gate, v_ffn_w_up, v_ffn_w_down, v_final_g]
    deltas, new_ms, new_vs = [], [], []
    for n, (w, g, m, v) in enumerate(zip(weights, grads, ms, vs)):
        if w.ndim == 1:
            w, g, m, v = (a.reshape(1, -1) for a in (w, g, m, v))
        g = g.reshape(w.shape)
        grads[n] = g.reshape(weights[n].shape)
        (d, nm, nv), _ = _adamw(w, g, m, v, f"adamw{n}")
        deltas.append(d.reshape(weights[n].shape))
        new_ms.append(nm.reshape(weights[n].shape))
        new_vs.append(nv.reshape(weights[n].shape))
    return (loss, grad_x, *grads, *deltas, *new_ms, *new_vs)
```

```python
import functools

import jax
import jax.numpy as jnp
from jax import lax
from jax.experimental import pallas as pl
from jax.experimental.pallas import tpu as pltpu

F32 = jnp.float32
BF16 = jnp.bfloat16
SDS = jax.ShapeDtypeStruct
MESH = pl.DeviceIdType.MESH

RMS_EPS = 1e-6
POOL_WINDOWS = (2, 4, 8, 16)
POOL_HALO = 16
CONV_HALO = 16
HEAD_DIM = 128
N_MOD = 6
N_CHIPS = 4
N_DEV = 8
ADAM_LR = 0.001
ADAM_B1 = 0.9
ADAM_B2 = 0.999
ADAM_EPS = 1e-08
ADAM_WD = 0.01
ADAM_STEP = 10
VMEM_LIMIT_V7X = 52 * 1024 * 1024
ANY = pl.BlockSpec(memory_space=pl.ANY)
VMEM_WHOLE = pl.BlockSpec(memory_space=pltpu.VMEM)


def _cp(*sem):
    return pltpu.CompilerParams(dimension_semantics=sem, vmem_limit_bytes=VMEM_LIMIT_V7X)


def _tile(n, pref, unit):
    if n <= pref:
        return n
    t = (pref // unit) * unit
    while t >= unit:
        if n % t == 0:
            return t
        t -= unit
    return n


def _dot(a, b):
    return jnp.dot(a, b, preferred_element_type=F32)


def _dot_nt(a, b):
    return lax.dot_general(a, b, (((1,), (1,)), ((), ())), preferred_element_type=F32)


def _dot_tn(a, b):
    return lax.dot_general(a, b, (((0,), (0,)), ((), ())), preferred_element_type=F32)


def _sigmoid(x, approx=False):
    return pl.reciprocal(1.0 + jnp.exp(-x), approx=True) if approx else 1.0 / (1.0 + jnp.exp(-x))


def _row_parts(rows, size=256):
    size = size if rows % size == 0 else rows
    return [slice(r, r + size) for r in range(0, rows, size)]


def _my_place():
    return lax.axis_index("x"), lax.axis_index("y"), lax.axis_index("c")


def _allgather8(blk, name):
    m, n = blk.shape

    def body(x_ref, out_ref, send_sems, recv_sems, local_sem):
        x, y, c = _my_place()
        me, sibling = (x, y, c), (x, y, 1 - c)
        chips = [(1 - x, y), (x, 1 - y), (1 - x, 1 - y)]

        def rows(px, py, pc):
            return out_ref.at[pl.ds((4 * px + 2 * py + pc) * m, m), :]

        def copy(k, block, to, src=None):
            return pltpu.make_async_remote_copy(
                src_ref=rows(*block) if src is None else src, dst_ref=rows(*block),
                send_sem=send_sems.at[k], recv_sem=recv_sems.at[k], device_id=to, device_id_type=MESH)

        mine = pltpu.make_async_copy(x_ref, rows(*me), local_sem)
        mine.start()
        first = [copy(0, me, sibling, src=x_ref)]
        first += [copy(1 + j, me, (*chip, c), src=x_ref) for j, chip in enumerate(chips)]
        for cp in first:
            cp.start()
        passed = [copy(4 + j, (*chip, c), sibling) for j, chip in enumerate(chips)]
        for j, chip in enumerate(chips):
            copy(1 + j, (*chip, c), me).wait_recv()
            passed[j].start()
        copy(0, sibling, me).wait_recv()
        for j, chip in enumerate(chips):
            copy(4 + j, (*chip, 1 - c), me).wait_recv()
        for cp in first + passed:
            cp.wait_send()
        mine.wait()

    return pl.pallas_call(
        body, name=name, out_shape=SDS((N_DEV * m, n), blk.dtype),
        in_specs=[VMEM_WHOLE], out_specs=VMEM_WHOLE,
        scratch_shapes=[pltpu.SemaphoreType.DMA((7,)), pltpu.SemaphoreType.DMA((7,)), pltpu.SemaphoreType.DMA],
    )(blk)


class _GatherShards:
    def __init__(self, ws):
        nt = len(ws)
        self.ws = ws
        self.inputs = list(ws)
        self.out_shapes = [SDS((N_CHIPS,) + w.shape, w.dtype) for w in ws]
        self.sem_shapes = [pltpu.SemaphoreType.DMA((6 * nt,)), pltpu.SemaphoreType.DMA((6 * nt,)),
                           pltpu.SemaphoreType.DMA((nt,))]

    def _copies(self, w_refs, out_refs, sems):
        send_sems, recv_sems, local_sems = sems
        x, y, c = _my_place()
        chips = [(1 - x, y), (x, 1 - y), (1 - x, 1 - y)]
        per_tensor = []
        for t, w in enumerate(self.ws):
            half = w.shape[0] // 2
            w_ref, out_ref = w_refs[t], out_refs[t]

            def dst(k, hc, out_ref=out_ref, half=half):
                return out_ref.at[k, pl.ds(hc * half, half), :]

            def copy(s, src, to_dst, to, t=t):
                return pltpu.make_async_remote_copy(
                    src_ref=src, dst_ref=to_dst, send_sem=send_sems.at[6 * t + s], recv_sem=recv_sems.at[6 * t + s],
                    device_id=to, device_id_type=MESH)

            mine = pltpu.make_async_copy(w_ref, out_ref.at[2 * x + y], local_sems.at[t])
            first = [copy(j, w_ref.at[pl.ds(c * half, half), :], dst(2 * x + y, c), (*chip, c))
                     for j, chip in enumerate(chips)]
            landed = [dst(2 * px + py, c) for px, py in chips]
            arrive = [copy(j, landed[j], landed[j], (*chips[j], c)) for j in range(3)]
            passed = [copy(3 + j, landed[j], landed[j], (x, y, 1 - c)) for j in range(3)]
            other = [dst(2 * px + py, 1 - c) for px, py in chips]
            from_sibling = [copy(3 + j, other[j], other[j], (x, y, 1 - c)) for j in range(3)]
            per_tensor.append((mine, first, arrive, passed, from_sibling))
        return per_tensor

    def start(self, w_refs, out_refs, sems):
        for mine, first, _, _, _ in self._copies(w_refs, out_refs, sems):
            mine.start()
            for cp in first:
                cp.start()

    def finish(self, w_refs, out_refs, sems):
        per_tensor = self._copies(w_refs, out_refs, sems)
        for _, _, arrive, passed, _ in per_tensor:
            for j in range(3):
                arrive[j].wait_recv()
                passed[j].start()
        for _, _, _, _, from_sibling in per_tensor:
            for cp in from_sibling:
                cp.wait_recv()
        for mine, first, _, passed, _ in per_tensor:
            for cp in first + passed:
                cp.wait_send()
            mine.wait()


class _ScatterToChips:
    def __init__(self, ps):
        nt = len(ps)
        self.ps = ps
        self.inputs = list(ps)
        self.out_shapes = [SDS((3,) + p.shape[1:], p.dtype) for p in ps]
        self.sem_shapes = [pltpu.SemaphoreType.DMA((3 * nt,)), pltpu.SemaphoreType.DMA((3 * nt,))]

    def _copies(self, p_refs, out_refs, sems):
        send_sems, recv_sems = sems
        x, y, c = _my_place()
        chips = [(1 - x, y), (x, 1 - y), (1 - x, 1 - y)]
        return [pltpu.make_async_remote_copy(
            src_ref=p_refs[t].at[2 * px + py], dst_ref=out_refs[t].at[j], send_sem=send_sems.at[3 * t + j],
            recv_sem=recv_sems.at[3 * t + j], device_id=(px, py, c), device_id_type=MESH)
            for t in range(len(self.ps)) for j, (px, py) in enumerate(chips)]

    def start(self, p_refs, out_refs, sems):
        for cp in self._copies(p_refs, out_refs, sems):
            cp.start()

    def finish(self, p_refs, out_refs, sems):
        for cp in self._copies(p_refs, out_refs, sems):
            cp.wait()


def _run_comm(comm, name):
    ni, no = len(comm.inputs), len(comm.out_shapes)

    def body(*refs):
        comm.start(refs[:ni], refs[ni:ni + no], refs[ni + no:])
        comm.finish(refs[:ni], refs[ni:ni + no], refs[ni + no:])

    return pl.pallas_call(body, name=name, out_shape=comm.out_shapes, in_specs=[ANY] * ni, out_specs=[ANY] * no,
                          scratch_shapes=comm.sem_shapes)(*comm.inputs)


def _launch(body, name, grid, in_specs, out_specs, out_shape, args, sem, scratch_shapes=(), comm=None):
    if comm is None:
        return pl.pallas_call(body, name=name, grid=grid, in_specs=in_specs, out_specs=out_specs,
                              out_shape=out_shape, scratch_shapes=list(scratch_shapes),
                              compiler_params=_cp(*sem))(*args), None
    single = not isinstance(out_shape, (list, tuple))
    out_specs_l = [out_specs] if single else list(out_specs)
    out_shape_l = [out_shape] if single else list(out_shape)
    n_in, n_out, n_scr = len(in_specs), len(out_shape_l), len(scratch_shapes)
    nci, nco = len(comm.inputs), len(comm.out_shapes)

    def carried(*refs):
        ins, refs = refs[:n_in], refs[n_in:]
        cins, refs = refs[:nci], refs[nci:]
        outs, refs = refs[:n_out], refs[n_out:]
        couts, refs = refs[:nco], refs[nco:]
        scr, sems = refs[:n_scr], refs[n_scr:]
        ids = [pl.program_id(ax) for ax in range(len(grid))]
        first = functools.reduce(jnp.logical_and, [i == 0 for i in ids])
        last = functools.reduce(jnp.logical_and, [i == g - 1 for i, g in zip(ids, grid)])

        @pl.when(first)
        def _():
            comm.start(cins, couts, sems)

        body(*ins, *outs, *scr)

        @pl.when(last)
        def _():
            comm.finish(cins, couts, sems)

    res = pl.pallas_call(
        carried, name=name, grid=grid, in_specs=list(in_specs) + [ANY] * nci, out_specs=out_specs_l + [ANY] * nco,
        out_shape=out_shape_l + list(comm.out_shapes), scratch_shapes=list(scratch_shapes) + list(comm.sem_shapes),
        compiler_params=_cp(*["arbitrary"] * len(grid)))(*args, *comm.inputs)
    main = res[:n_out]
    return (main[0] if single else main), res[n_out:]


class _SwapHalves:
    def __init__(self, gs):
        nt = len(gs)
        self.gs = gs
        self.inputs = list(gs)
        self.out_shapes = [SDS((g.shape[0], g.shape[1] // 2, g.shape[2]), g.dtype) for g in gs]
        self.sem_shapes = [pltpu.SemaphoreType.DMA((nt,)), pltpu.SemaphoreType.DMA((nt,))]

    def _copies(self, g_refs, out_refs, sems):
        send_sems, recv_sems = sems
        x, y, c = _my_place()
        cps = []
        for t, g in enumerate(self.gs):
            half = g.shape[1] // 2
            cps.append(pltpu.make_async_remote_copy(
                src_ref=g_refs[t].at[:, pl.ds((1 - c) * half, half), :], dst_ref=out_refs[t],
                send_sem=send_sems.at[t], recv_sem=recv_sems.at[t], device_id=(x, y, 1 - c), device_id_type=MESH))
        return cps

    def start(self, g_refs, out_refs, sems):
        for cp in self._copies(g_refs, out_refs, sems):
            cp.start()

    def finish(self, g_refs, out_refs, sems):
        for cp in self._copies(g_refs, out_refs, sems):
            cp.wait()


def _join_sibling_halves(fs, name):
    nt = len(fs)

    def body(*refs):
        out_refs = refs[nt:2 * nt]
        send_sems, recv_sems = refs[2 * nt:]
        x, y, c = _my_place()
        cps = []
        for t in range(nt):
            r = fs[t].shape[0] // 2
            mine = out_refs[t].at[pl.ds(c * r, r), :]
            cp = pltpu.make_async_remote_copy(
                src_ref=mine, dst_ref=mine, send_sem=send_sems.at[t], recv_sem=recv_sems.at[t],
                device_id=(x, y, 1 - c), device_id_type=MESH)
            cp.start()
            cps.append((cp, r))
        for t, (cp, r) in enumerate(cps):
            cp.wait_send()
            other = out_refs[t].at[pl.ds((1 - c) * r, r), :]
            pltpu.make_async_remote_copy(
                src_ref=other, dst_ref=other, send_sem=send_sems.at[t], recv_sem=recv_sems.at[t],
                device_id=(x, y, 1 - c), device_id_type=MESH).wait_recv()

    return pl.pallas_call(
        body, name=name, out_shape=[SDS(f.shape, f.dtype) for f in fs],
        in_specs=[ANY] * nt, out_specs=[ANY] * nt, input_output_aliases={t: t for t in range(nt)},
        scratch_shapes=[pltpu.SemaphoreType.DMA((nt,)), pltpu.SemaphoreType.DMA((nt,))],
    )(*fs)


def _add_sibling(g, recv, name):
    _, R, C = g.shape
    half = R // 2
    br = _tile(half, max(16, (1 << 20) // C), 16)
    nrb = half // br

    def body(g_ref, r_ref, bf_ref, own_ref):
        s = g_ref[...] + r_ref[...]
        bf_ref[...] = s.astype(BF16)

        @pl.when(pl.program_id(1) == 2 * lax.axis_index("x") + lax.axis_index("y"))
        def _():
            own_ref[...] = s

    return pl.pallas_call(
        body, name=name, grid=(nrb, N_CHIPS),
        in_specs=[pl.BlockSpec((None, br, C), lambda i, k: (k, lax.axis_index("c") * nrb + i, 0)),
                  pl.BlockSpec((None, br, C), lambda i, k: (k, i, 0))],
        out_specs=[pl.BlockSpec((None, br, C), lambda i, k: (k, i, 0)),
                   pl.BlockSpec((br, C), lambda i, k: (i, 0))],
        out_shape=[SDS((N_CHIPS, half, C), BF16), SDS((half, C), F32)],
        compiler_params=_cp("arbitrary", "arbitrary"),
    )(g, recv)


def _add_chips(own, recv, name):
    r, C = own.shape
    br = _tile(r, max(16, (1 << 20) // C), 16)
    nrb = r // br

    def body(own_ref, r_ref, o_ref):
        s = own_ref[...]
        for j in range(3):
            s = s + r_ref[j].astype(F32)
        o_ref[...] = s

    return pl.pallas_call(
        body, name=name, grid=(nrb,),
        in_specs=[pl.BlockSpec((br, C), lambda i: (i, 0)), pl.BlockSpec((3, br, C), lambda i: (0, i, 0))],
        out_specs=pl.BlockSpec((br, C), lambda i: (lax.axis_index("c") * nrb + i, 0)),
        out_shape=SDS((2 * r, C), F32), compiler_params=_cp("arbitrary"),
    )(own, recv)


class _Reduce:
    def __init__(self, gs, name):
        self.gs, self.name = gs, name
        self.swap = _SwapHalves(gs)

    def swapped(self, brought=None):
        if brought is None:
            brought = _run_comm(self.swap, self.name + "_swap")
        self.parts = [_add_sibling(g, r, self.name + "_add1") for g, r in zip(self.gs, brought)]
        return self

    def scatter(self, which=None):
        which = range(len(self.parts)) if which is None else which
        return _ScatterToChips([self.parts[t][0] for t in which])

    def finish(self, brought=None):
        if brought is None:
            brought = _run_comm(self.scatter(), self.name + "_scatter")
        fins = [_add_chips(p[1], r, self.name + "_add2") for p, r in zip(self.parts, brought)]
        return _join_sibling_halves(fins, self.name + "_join")


def _sum_devices(allv, name):
    _, r, n = allv.shape

    def body(a_ref, o_ref):
        s = a_ref[0]
        for d in range(1, N_DEV):
            s = s + a_ref[d]
        o_ref[...] = s

    return pl.pallas_call(body, name=name, out_shape=SDS((r, n), F32), in_specs=[VMEM_WHOLE],
                          out_specs=VMEM_WHOLE)(allv)


def _adamw(w, g, m, v, name, comm=None):
    shape = w.shape
    C = shape[-1]
    R = w.size // C
    args = [a.reshape(R, C) for a in (w, g, m, v)]
    br = _tile(R, max(8, (1 << 18) // C), 8)

    def body(w_ref, g_ref, m_ref, v_ref, d_ref, nm_ref, nv_ref):
        g_ = g_ref[...]
        m_ = ADAM_B1 * m_ref[...] + (1.0 - ADAM_B1) * g_
        v_ = ADAM_B2 * v_ref[...] + (1.0 - ADAM_B2) * (g_ * g_)
        m_hat = m_ / (1.0 - ADAM_B1 ** ADAM_STEP)
        v_hat = v_ / (1.0 - ADAM_B2 ** ADAM_STEP)
        d_ref[...] = -ADAM_LR * (m_hat / (jnp.sqrt(v_hat) + ADAM_EPS) + ADAM_WD * w_ref[...])
        nm_ref[...] = m_
        nv_ref[...] = v_

    spec = pl.BlockSpec((br, C), lambda i: (i, 0))
    outs, brought = _launch(body, name, (R // br,), [spec] * 4, [spec] * 3, [SDS((R, C), F32)] * 3, args,
                            ("parallel",), comm=comm)
    return [o.reshape(shape) for o in outs], brought


def _mod_fwd(c_rows, w_mod, b_cols, name):
    L, D, n = w_mod.shape
    bn = _tile(n, 512, 128)

    def body(c_ref, w_ref, b_ref, o_ref):
        cc = c_ref[...]
        sc = (cc * _sigmoid(cc)).astype(BF16)
        o_ref[...] = _dot(sc, w_ref[...].astype(BF16)) + b_ref[...]

    return pl.pallas_call(
        body, name=name, grid=(L, n // bn),
        in_specs=[pl.BlockSpec((16, D), lambda l, j: (0, 0)),
                  pl.BlockSpec((None, D, bn), lambda l, j: (l, 0, j)),
                  pl.BlockSpec((None, 1, bn), lambda l, j: (l, 0, j))],
        out_specs=pl.BlockSpec((None, 16, bn), lambda l, j: (l, 0, j)),
        out_shape=SDS((L, 16, n), F32), compiler_params=_cp("parallel", "parallel"),
    )(c_rows, w_mod, b_cols)


def _mod_wgrad(c_cols, dmod, name, comm=None):
    D = c_cols.shape[0]
    L, _, n = dmod.shape
    bd = _tile(D, 512, 8)
    bn = _tile(n, 512, 128)

    def body(c_ref, d_ref, o_ref):
        cc = c_ref[...]
        sc = cc * _sigmoid(cc)
        dm = d_ref[...]
        acc = sc[:, 0:1] * dm[0:1, :]
        for b in range(1, N_DEV):
            acc = acc + sc[:, b:b + 1] * dm[b:b + 1, :]
        o_ref[...] = acc

    return _launch(body, name, (L, D // bd, n // bn),
                   [pl.BlockSpec((bd, N_DEV), lambda l, i, j: (i, 0)),
                    pl.BlockSpec((None, N_DEV, bn), lambda l, i, j: (l, 0, j))],
                   pl.BlockSpec((None, bd, bn), lambda l, i, j: (l, i, j)), SDS((L, D, n), F32), (c_cols, dmod),
                   ("parallel", "parallel", "parallel"), comm=comm)


def _mm_in(x, w4, bn, name, gate=None, comm=None):
    M, K = x.shape
    nsh, _, n = w4.shape
    N = nsh * n
    nb = n // bn
    bm = _tile(M, 1024 if gate is None else 512, 16)
    x_spec = pl.BlockSpec((bm, K), lambda j, i: (i, 0))
    w_spec = pl.BlockSpec((None, K, bn), lambda j, i: (j // nb, 0, j % nb))
    o_spec = pl.BlockSpec((bm, bn), lambda j, i: (i, j))
    if gate is None:
        def body(x_ref, w_ref, o_ref):
            o_ref[...] = _dot(x_ref[...], w_ref[...]).astype(BF16)

        return _launch(body, name, (N // bn, M // bm), [x_spec, w_spec], o_spec, SDS((M, N), BF16), (x, w4),
                       ("parallel", "parallel"), comm=comm)

    def body_gated(x_ref, w_ref, g_ref, up_ref, a_ref):
        for rows in _row_parts(bm):
            up = _dot(x_ref[rows, :], w_ref[...])
            g = g_ref[rows, :].astype(F32)
            up_ref[rows, :] = up.astype(BF16)
            a_ref[rows, :] = (g * _sigmoid(g, approx=True) * up).astype(BF16)

    return _launch(body_gated, name, (N // bn, M // bm), [x_spec, w_spec, o_spec], [o_spec, o_spec],
                   [SDS((M, N), BF16)] * 2, (x, w4, gate), ("parallel", "parallel"), comm=comm)


def _mm_out_res(a, w, h, cvec, name, groups=False, comm=None):
    M = a.shape[0]
    N = h.shape[1]
    if groups:
        bn = w.shape[2]
        a_spec = pl.BlockSpec((_tile(M, 512, 16), w.shape[1]), lambda j, i: (i, j))
        w_spec = pl.BlockSpec((None, w.shape[1], bn), lambda j, i: (j, 0, 0))
    else:
        bn = _tile(N, 512, 128)
        a_spec = pl.BlockSpec((_tile(M, 512, 16), a.shape[1]), lambda j, i: (i, 0))
        w_spec = pl.BlockSpec((a.shape[1], bn), lambda j, i: (0, j))
    bm = _tile(M, 512, 16)
    o_spec = pl.BlockSpec((bm, bn), lambda j, i: (i, j))

    def body(a_ref, w_ref, h_ref, c_ref, hn_ref, y_ref):
        y = _dot(a_ref[...], w_ref[...])
        hn_ref[...] = h_ref[...] + c_ref[...] * y
        y_ref[...] = y.astype(BF16)

    return _launch(body, name, (N // bn, M // bm),
                   [a_spec, w_spec, o_spec, pl.BlockSpec((1, bn), lambda j, i: (0, j))], [o_spec, o_spec],
                   [SDS((M, N), F32), SDS((M, N), BF16)], (a, w, h, cvec), ("parallel", "parallel"), comm=comm)


def _mm_nt(dy, w, name, groups=False, swiglu=None, comm=None):
    M = dy.shape[0]
    bm = _tile(M, 1024, 16)
    if groups:
        N = dy.shape[1]
        bn = w.shape[1]
        dy_spec = pl.BlockSpec((bm, w.shape[2]), lambda j, i: (i, j))
        w_spec = pl.BlockSpec((None, bn, w.shape[2]), lambda j, i: (j, 0, 0))
    else:
        N = w.shape[0]
        bn = _tile(N, 512, 128)
        dy_spec = pl.BlockSpec((bm, dy.shape[1]), lambda j, i: (i, 0))
        w_spec = pl.BlockSpec((bn, w.shape[1]), lambda j, i: (j, 0))
    o_spec = pl.BlockSpec((bm, bn), lambda j, i: (i, j))
    if swiglu is None:
        def body(dy_ref, w_ref, o_ref):
            o_ref[...] = _dot_nt(dy_ref[...], w_ref[...]).astype(BF16)

        return _launch(body, name, (N // bn, M // bm), [dy_spec, w_spec], o_spec, SDS((M, N), BF16), (dy, w),
                       ("parallel", "parallel"), comm=comm)

    def body_swiglu(dy_ref, w_ref, g_ref, u_ref, dg_ref, du_ref):
        for rows in _row_parts(bm):
            da = _dot_nt(dy_ref[rows, :], w_ref[...])
            g = g_ref[rows, :].astype(F32)
            sg = _sigmoid(g, approx=True)
            silu = g * sg
            dg_ref[rows, :] = (da * u_ref[rows, :].astype(F32) * (sg + silu * (1.0 - sg))).astype(BF16)
            du_ref[rows, :] = (da * silu).astype(BF16)

    return _launch(body_swiglu, name, (N // bn, M // bm), [dy_spec, w_spec, o_spec, o_spec], [o_spec, o_spec],
                   [SDS((M, N), BF16)] * 2, (dy, w, *swiglu), ("parallel", "parallel"), comm=comm)


def _mm_nt_acc(dx, w4, name, add=None, comm=None):
    M = dx.shape[0]
    nc, K, n = w4.shape
    with_add = add is not None
    bm = _tile(M, 512 if with_add else 1024, 16)

    def body(*refs):
        dx_ref, w_ref = refs[:2]
        o_ref, acc_ref = refs[-2:]
        c = pl.program_id(1)
        s = _dot_nt(dx_ref[...], w_ref[...])

        @pl.when(c == 0)
        def _():
            acc_ref[...] = s + refs[2][...].astype(F32) if with_add else s

        @pl.when(c > 0)
        def _():
            acc_ref[...] += s

        @pl.when(c == nc - 1)
        def _():
            o_ref[...] = acc_ref[...].astype(BF16)

    o_spec = pl.BlockSpec((bm, K), lambda i, c: (i, 0))
    in_specs = [pl.BlockSpec((bm, n), lambda i, c: (i, c)), pl.BlockSpec((None, K, n), lambda i, c: (c, 0, 0))]
    args = [dx, w4]
    if with_add:
        in_specs.append(o_spec)
        args.append(add)
    return _launch(body, name, (M // bm, nc), in_specs, o_spec, SDS((M, K), BF16), args, ("parallel", "arbitrary"),
                   scratch_shapes=[pltpu.VMEM((bm, K), F32)], comm=comm)


def _mm_tn(x, dy, name, shard_cols=None, groups=None, comm=None):
    M, K = x.shape
    N = dy.shape[1]
    bm = _tile(M, 2048, 16)
    if groups is not None:
        kg, ng = K // groups, N // groups
        grid = (groups, 1, M // bm)
        x_spec = pl.BlockSpec((bm, kg), lambda i, j, s: (s, i))
        dy_spec = pl.BlockSpec((bm, ng), lambda i, j, s: (s, i))
        o_spec = pl.BlockSpec((None, kg, ng), lambda i, j, s: (i, 0, 0))
        out_shape = SDS((groups, kg, ng), F32)
    else:
        bko = _tile(K, 1408, 128)
        if shard_cols is not None:
            bn = _tile(shard_cols, 1536, 128)
            nb = shard_cols // bn
            o_spec = pl.BlockSpec((None, bko, bn), lambda i, j, s: (j // nb, i, j % nb))
            out_shape = SDS((N_CHIPS, K, shard_cols), F32)
        else:
            bn = _tile(N, 1024, 128)
            o_spec = pl.BlockSpec((bko, bn), lambda i, j, s: (i, j))
            out_shape = SDS((K, N), F32)
        grid = (K // bko, N // bn, M // bm)
        x_spec = pl.BlockSpec((bm, bko), lambda i, j, s: (s, i))
        dy_spec = pl.BlockSpec((bm, bn), lambda i, j, s: (s, j))

    def body(x_ref, dy_ref, o_ref):
        p = _dot_tn(x_ref[...], dy_ref[...])

        @pl.when(pl.program_id(2) == 0)
        def _():
            o_ref[...] = p

        @pl.when(pl.program_id(2) > 0)
        def _():
            o_ref[...] += p

    return _launch(body, name, grid, [x_spec, dy_spec], o_spec, out_shape, (x, dy),
                   ("parallel", "parallel", "arbitrary"), comm=comm)


def _norm_mod_rows(h, g, scale, shift):
    r = lax.rsqrt(jnp.mean(h * h, axis=-1, keepdims=True) + RMS_EPS)
    return (h * r) * g * (1.0 + scale) + shift


def _vec_spec(D):
    return pl.BlockSpec((1, D), lambda i: (0, 0))


def _norm_mod(h, g, scale, shift, name):
    S, D = h.shape
    bs = _tile(S, 512, 16)

    def body(h_ref, g_ref, sc_ref, sh_ref, u_ref):
        u_ref[...] = _norm_mod_rows(h_ref[...], g_ref[...], sc_ref[...], sh_ref[...]).astype(BF16)

    row = pl.BlockSpec((bs, D), lambda i: (i, 0))
    return pl.pallas_call(
        body, name=name, grid=(S // bs,), in_specs=[row, _vec_spec(D), _vec_spec(D), _vec_spec(D)],
        out_specs=row, out_shape=SDS((S, D), BF16), compiler_params=_cp("parallel"))(h, g, scale, shift)


def _rows_back(x, n):
    return pltpu.roll(x, n, 0)


def _rows_ahead(x, n):
    return pltpu.roll(x, x.shape[0] - n, 0)


def _window_sums(x, w, shift):
    n = 1
    while n < w:
        x = x + shift(x, n)
        n *= 2
    return x


def _pool_pre(h, g, scale, shift, name):
    S, D = h.shape
    ng = len(POOL_WINDOWS)
    pg = D // ng
    bs = _tile(S, 256, POOL_HALO)
    hb = bs // POOL_HALO

    def body(h_ref, hh_ref, g_ref, sc_ref, sh_ref, o_ref):
        i = pl.program_id(0)
        u = _norm_mod_rows(h_ref[...], g_ref[...], sc_ref[...], sh_ref[...])
        uh = _norm_mod_rows(hh_ref[...], g_ref[...], sc_ref[...], sh_ref[...])
        uh = jnp.where(i == 0, 0.0, uh)
        ue = jnp.concatenate([uh, u], axis=0)
        t = i * bs + lax.broadcasted_iota(jnp.int32, (bs, 1), 0)
        for gi, w in enumerate(POOL_WINDOWS):
            cols = slice(gi * pg, (gi + 1) * pg)
            inv = 1.0 / jnp.minimum(t + 1, w).astype(F32)
            sums = _window_sums(ue[:, cols], w, _rows_back)[POOL_HALO:]
            o_ref[:, cols] = (sums * inv - u[:, cols]).astype(BF16)

    row = pl.BlockSpec((bs, D), lambda i: (i, 0))
    halo = pl.BlockSpec((POOL_HALO, D), lambda i: (jnp.maximum(i * hb - 1, 0), 0))
    return pl.pallas_call(
        body, name=name, grid=(S // bs,),
        in_specs=[row, halo, _vec_spec(D), _vec_spec(D), _vec_spec(D)],
        out_specs=row, out_shape=SDS((S, D), BF16), compiler_params=_cp("parallel"))(h, h, g, scale, shift)


def _pool_post(dd, name):
    S, D = dd.shape
    ng = len(POOL_WINDOWS)
    pg = D // ng
    bs = _tile(S, 256, POOL_HALO)
    hb = bs // POOL_HALO
    nblk = S // bs

    def body(d_ref, dn_ref, o_ref):
        i = pl.program_id(0)
        d = d_ref[...].astype(F32)
        dn = jnp.where(i == nblk - 1, 0.0, dn_ref[...].astype(F32))
        de = jnp.concatenate([d, dn], axis=0)
        t = i * bs + lax.broadcasted_iota(jnp.int32, (bs + POOL_HALO, 1), 0)
        for gi, w in enumerate(POOL_WINDOWS):
            cols = slice(gi * pg, (gi + 1) * pg)
            inv = 1.0 / jnp.minimum(t + 1, w).astype(F32)
            sums = _window_sums(de[:, cols] * inv, w, _rows_ahead)[:bs]
            o_ref[:, cols] = (sums - d[:, cols]).astype(BF16)

    row = pl.BlockSpec((bs, D), lambda i: (i, 0))
    nxt = pl.BlockSpec((POOL_HALO, D), lambda i: (jnp.minimum((i + 1) * hb, S // POOL_HALO - 1), 0))
    return pl.pallas_call(
        body, name=name, grid=(nblk,), in_specs=[row, nxt], out_specs=row, out_shape=SDS((S, D), BF16),
        compiler_params=_cp("parallel"))(dd, dd)


def _colsum(x):
    return jnp.sum(x, axis=0, keepdims=True)


def _accumulate_rows(st_ref, rows, first):
    @pl.when(first)
    def _():
        st_ref[...] = jnp.zeros_like(st_ref)

    for r, row in enumerate(rows):
        st_ref[r:r + 1, :] += row


def _norm_bwd(h, g, scale, du, dh_out, name, prev=None, comm=None):
    S, D = h.shape
    bs = _tile(S, 256, 16)
    with_prev = prev is not None

    def body(*refs):
        h_ref, g_ref, sc_ref, du_ref, dho_ref = refs[:5]
        if with_prev:
            y_ref, cv_ref, dh_ref, dy_ref, st_ref = refs[5:]
        else:
            dh_ref, st_ref = refs[5:]
        hh = h_ref[...]
        du_ = du_ref[...].astype(F32)
        r = lax.rsqrt(jnp.mean(hh * hh, axis=-1, keepdims=True) + RMS_EPS)
        xhat = hh * r
        dn = du_ * (1.0 + sc_ref[...])
        dxhat = dn * g_ref[...]
        dh = dho_ref[...] + r * (dxhat - xhat * jnp.mean(dxhat * xhat, axis=-1, keepdims=True))
        dh_ref[...] = dh
        rows = [_colsum(du_), _colsum(du_ * (xhat * g_ref[...])), _colsum(dn * xhat)]
        if with_prev:
            dy_ref[...] = (dh * cv_ref[...]).astype(BF16)
            rows.append(_colsum(dh * y_ref[...].astype(F32)))
        _accumulate_rows(st_ref, rows, pl.program_id(0) == 0)

    row = pl.BlockSpec((bs, D), lambda i: (i, 0))
    st_spec = pl.BlockSpec((8, D), lambda i: (0, 0))
    in_specs = [row, _vec_spec(D), _vec_spec(D), row, row]
    args = [h, g, scale, du, dh_out]
    out_specs, out_shape = [row], [SDS((S, D), F32)]
    if with_prev:
        in_specs += [row, _vec_spec(D)]
        args += list(prev)
        out_specs.append(row)
        out_shape.append(SDS((S, D), BF16))
    out_specs.append(st_spec)
    out_shape.append(SDS((8, D), F32))
    outs, brought = _launch(body, name, (S // bs,), in_specs, out_specs, out_shape, args, ("arbitrary",), comm=comm)
    return (outs, brought) if comm else outs


def _loss_head(h, g, target, y, cvec, name):
    S, D = h.shape
    bs = _tile(S, 256, 16)

    def body(h_ref, g_ref, t_ref, y_ref, cv_ref, dh_ref, dy_ref, st_ref):
        hh = h_ref[...]
        r = lax.rsqrt(jnp.mean(hh * hh, axis=-1, keepdims=True) + RMS_EPS)
        xhat = hh * r
        err = xhat * g_ref[...] - t_ref[...]
        dout = err * (1.0 / D)
        dxhat = dout * g_ref[...]
        dh = r * (dxhat - xhat * jnp.mean(dxhat * xhat, axis=-1, keepdims=True))
        dh_ref[...] = dh
        dy_ref[...] = (dh * cv_ref[...]).astype(BF16)
        rows = [_colsum(dout * xhat), _colsum(dh * y_ref[...].astype(F32)), _colsum(err * err) * (0.5 / D)]
        _accumulate_rows(st_ref, rows, pl.program_id(0) == 0)

    row = pl.BlockSpec((bs, D), lambda i: (i, 0))
    return pl.pallas_call(
        body, name=name, grid=(S // bs,), in_specs=[row, _vec_spec(D), row, row, _vec_spec(D)],
        out_specs=[row, row, pl.BlockSpec((8, D), lambda i: (0, 0))],
        out_shape=[SDS((S, D), F32), SDS((S, D), BF16), SDS((8, D), F32)],
        compiler_params=_cp("arbitrary"))(h, g, target, y, cvec)


def _sum_all(x, name):
    def body(x_ref, o_ref):
        o_ref[...] = jnp.sum(jnp.sum(x_ref[...], axis=1, keepdims=True), axis=0, keepdims=True)

    return pl.pallas_call(body, name=name, out_shape=SDS((1, 1), F32), in_specs=[VMEM_WHOLE],
                          out_specs=VMEM_WHOLE)(x)


def _conv_mid(u3, cw, name):
    S, D3 = u3.shape
    D = D3 // 3
    cb = _tile(D, 512, 128)
    nj = D // cb
    bs = _tile(S, 256, CONV_HALO)
    hb = bs // CONV_HALO

    def body(b_ref, c_ref, v_ref, ch_ref, vh_ref, w_ref, o_ref):
        i = pl.program_id(0)
        z = c_ref[...].astype(F32) * v_ref[...].astype(F32)
        zh = jnp.where(i == 0, 0.0, ch_ref[...].astype(F32) * vh_ref[...].astype(F32))
        ze = jnp.concatenate([zh, z], axis=0)
        w = w_ref[...]
        zc = w[2:3] * z
        zc = zc + w[1:2] * _rows_back(ze, 1)[CONV_HALO:]
        zc = zc + w[0:1] * _rows_back(ze, 2)[CONV_HALO:]
        o_ref[...] = (b_ref[...].astype(F32) * zc).astype(BF16)

    def blk(off):
        return pl.BlockSpec((bs, cb), lambda i, j: (i, off + j))

    def halo(off):
        return pl.BlockSpec((CONV_HALO, cb), lambda i, j: (jnp.maximum(i * hb - 1, 0), off + j))

    return pl.pallas_call(
        body, name=name, grid=(S // bs, nj),
        in_specs=[blk(0), blk(nj), blk(2 * nj), halo(nj), halo(2 * nj), pl.BlockSpec((3, cb), lambda i, j: (0, j))],
        out_specs=pl.BlockSpec((bs, cb), lambda i, j: (i, j)), out_shape=SDS((S, D), BF16),
        compiler_params=_cp("parallel", "parallel"))(u3, u3, u3, u3, u3, cw)


def _conv_mid_bwd(u3, da, cw, name):
    S, D3 = u3.shape
    D = D3 // 3
    cb = _tile(D, 512, 128)
    nj = D // cb
    bs = _tile(S, 256, CONV_HALO)
    hb = bs // CONV_HALO
    nblk = S // bs
    last_halo = S // CONV_HALO - 1

    def body(b_ref, c_ref, v_ref, ch_ref, vh_ref, bn_ref, da_ref, dan_ref, w_ref, db_ref, dc_ref, dv_ref, dw_ref):
        i = pl.program_id(0)
        c = c_ref[...].astype(F32)
        v = v_ref[...].astype(F32)
        b = b_ref[...].astype(F32)
        da_ = da_ref[...].astype(F32)
        z = c * v
        zh = jnp.where(i == 0, 0.0, ch_ref[...].astype(F32) * vh_ref[...].astype(F32))
        ze = jnp.concatenate([zh, z], axis=0)
        z1 = _rows_back(ze, 1)[CONV_HALO:]
        z2 = _rows_back(ze, 2)[CONV_HALO:]
        w = w_ref[...]
        zc = w[2:3] * z + w[1:2] * z1 + w[0:1] * z2
        db_ref[...] = (da_ * zc).astype(BF16)
        dzc = da_ * b
        dzn = jnp.where(i == nblk - 1, 0.0, dan_ref[...].astype(F32) * bn_ref[...].astype(F32))
        dze = jnp.concatenate([dzc, dzn], axis=0)
        dz = w[2:3] * dzc
        dz = dz + w[1:2] * _rows_ahead(dze, 1)[:bs]
        dz = dz + w[0:1] * _rows_ahead(dze, 2)[:bs]
        dc_ref[...] = (dz * v).astype(BF16)
        dv_ref[...] = (dz * c).astype(BF16)
        dw_ref[...] = jnp.zeros_like(dw_ref)
        dw_ref[0:1, :] = _colsum(dzc * z2)
        dw_ref[1:2, :] = _colsum(dzc * z1)
        dw_ref[2:3, :] = _colsum(dzc * z)

    def blk(off):
        return pl.BlockSpec((bs, cb), lambda i, j: (i, off + j))

    def halo(off):
        return pl.BlockSpec((CONV_HALO, cb), lambda i, j: (jnp.maximum(i * hb - 1, 0), off + j))

    def nxt(off):
        return pl.BlockSpec((CONV_HALO, cb), lambda i, j: (jnp.minimum((i + 1) * hb, last_halo), off + j))

    o_spec = pl.BlockSpec((bs, cb), lambda i, j: (i, j))
    return pl.pallas_call(
        body, name=name, grid=(nblk, nj),
        in_specs=[blk(0), blk(nj), blk(2 * nj), halo(nj), halo(2 * nj), nxt(0), o_spec, nxt(0),
                  pl.BlockSpec((3, cb), lambda i, j: (0, j))],
        out_specs=[o_spec, o_spec, o_spec, pl.BlockSpec((None, 8, cb), lambda i, j: (i, 0, j))],
        out_shape=[SDS((S, D), BF16)] * 3 + [SDS((nblk, 8, D), F32)],
        compiler_params=_cp("parallel", "parallel"))(u3, u3, u3, u3, u3, u3, da, da, cw)


def _sum_lead(x, name):
    n, r, C = x.shape

    def body(x_ref, o_ref):
        @pl.when(pl.program_id(0) == 0)
        def _():
            o_ref[...] = x_ref[...]

        @pl.when(pl.program_id(0) > 0)
        def _():
            o_ref[...] += x_ref[...]

    return pl.pallas_call(
        body, name=name, grid=(n,), in_specs=[pl.BlockSpec((None, r, C), lambda i: (i, 0, 0))],
        out_specs=pl.BlockSpec((r, C), lambda i: (0, 0)), out_shape=SDS((r, C), F32),
        compiler_params=_cp("arbitrary"))(x)


def _log_sigmoids(z):
    lb = jnp.minimum(z, 0.0) - jnp.log(1.0 + jnp.exp(-jnp.abs(z)))
    return lb, lb - z


def _attn_blocks(S):
    bk = _tile(S, 256, 128)
    bq = 4 * bk if S % (4 * bk) == 0 else bk
    return bq, bk


def _tri(n, pred):
    rowi = lax.broadcasted_iota(jnp.int32, (n, n), 0)
    coli = lax.broadcasted_iota(jnp.int32, (n, n), 1)
    return jnp.where(pred(rowi, coli), 1.0, 0.0).astype(BF16)


def _causal_mask(bq, bk, m):
    rowi = lax.broadcasted_iota(jnp.int32, (bq, bk), 0)
    coli = lax.broadcasted_iota(jnp.int32, (bq, bk), 1)
    return m * bk + coli < rowi


def _sb_attention(qkv, name, comm=None):
    S, D3 = qkv.shape
    D = D3 // 3
    H = D // HEAD_DIM
    bq, bk = _attn_blocks(S)
    nq, r = S // bq, bq // bk
    unroll = r
    scale = HEAD_DIM ** -0.5

    def body(q_ref, k_ref, v_ref, o_ref, lt_ref):
        i = pl.program_id(1)
        q = q_ref[...]
        after = _tri(bk, lambda j, s: j > s)

        def block(kb, carry, acc, causal):
            rows = pl.ds(pl.multiple_of(kb * bk, bk), bk)
            z = _dot_nt(q, k_ref[rows, :]) * scale
            lb, l1 = _log_sigmoids(z)
            if causal is not None:
                l1 = jnp.where(causal, l1, 0.0)
            a = jnp.exp(lb + (_dot(l1.astype(BF16), after) + carry))
            if causal is not None:
                a = jnp.where(causal, a, 0.0)
            acc = acc + _dot(a.astype(BF16), v_ref[rows, :])
            return carry + jnp.sum(l1, axis=1, keepdims=True), acc

        carry, acc = jnp.zeros((bq, 1), F32), jnp.zeros((bq, HEAD_DIM), F32)
        for m in reversed(range(r)):
            carry, acc = block(i * r + m, carry, acc, _causal_mask(bq, bk, m))

        def step(j, ca):
            for n in range(unroll):
                ca = block(i * r - 1 - unroll * j - n, ca[0], ca[1], None)
            return ca

        carry, acc = lax.fori_loop(0, i * (r // unroll), step, (carry, acc))
        o_ref[...] = acc.astype(BF16)
        lt_ref[...] = jnp.broadcast_to(carry, (bq, HEAD_DIM))

    head_rows = lambda off: pl.BlockSpec((S, HEAD_DIM), lambda hd, i: (0, off + hd))
    blk = pl.BlockSpec((bq, HEAD_DIM), lambda hd, i: (i, hd))
    return _launch(body, name, (H, nq), [blk, head_rows(H), head_rows(2 * H)], [blk, blk],
                   [SDS((S, D), BF16), SDS((S, D), F32)], (qkv, qkv, qkv), ("parallel", "arbitrary"), comm=comm)


def _sb_attention_bwd(qkv, ltot, do, name):
    S, D3 = qkv.shape
    D = D3 // 3
    H = D // HEAD_DIM
    bq, bk = _attn_blocks(S)
    nq, r = S // bq, bq // bk
    unroll = r
    scale = HEAD_DIM ** -0.5

    def body(q_ref, k_ref, v_ref, lt_ref, do_ref, dq_ref, dk_ref, dv_ref, dkt_acc, dvt_acc):
        i = pl.program_id(1)
        q = q_ref[...]
        do_ = do_ref[...]
        qt = jnp.transpose(q.astype(F32)).astype(BF16)
        dot = jnp.transpose(do_.astype(F32)).astype(BF16)
        lt = lt_ref[:, 0:1]
        after = _tri(bk, lambda j, s: j > s)
        before = _tri(bk, lambda j, s: j < s)

        @pl.when(i == 0)
        def _():
            dkt_acc[...] = jnp.zeros_like(dkt_acc)
            dvt_acc[...] = jnp.zeros_like(dvt_acc)

        def block(kb, c1, ce, dq, causal):
            rows = pl.ds(pl.multiple_of(kb * bk, bk), bk)
            k = k_ref[rows, :]
            v = v_ref[rows, :]
            z = _dot_nt(q, k) * scale
            lb, l1 = _log_sigmoids(z)
            sig = jnp.exp(lb)
            if causal is not None:
                l1 = jnp.where(causal, l1, 0.0)
            c1 = c1 + jnp.sum(l1, axis=1, keepdims=True)
            a = jnp.exp(lb + (_dot(l1.astype(BF16), after) + (lt - c1)))
            if causal is not None:
                a = jnp.where(causal, a, 0.0)
            e = a * _dot_nt(do_, v)
            p = _dot(e.astype(BF16), before) + ce
            dz = e - sig * (e + p)
            if causal is not None:
                dz = jnp.where(causal, dz, 0.0)
            dzb = dz.astype(BF16)
            dkt_acc[kb] += _dot(qt, dzb)
            dvt_acc[kb] += _dot(dot, a.astype(BF16))
            dq = dq + _dot(dzb, k)
            return c1, ce + jnp.sum(e, axis=1, keepdims=True), dq

        def step(j, st):
            for n in range(unroll):
                st = block(unroll * j + n, st[0], st[1], st[2], None)
            return st

        zero = jnp.zeros((bq, 1), F32)
        c1, ce, dq = lax.fori_loop(0, i * (r // unroll), step, (zero, zero, jnp.zeros((bq, HEAD_DIM), F32)))
        for m in range(r):
            c1, ce, dq = block(i * r + m, c1, ce, dq, _causal_mask(bq, bk, m))
        dq_ref[...] = (dq * scale).astype(BF16)

        @pl.when(i == nq - 1)
        def _():
            def flush(kb, _):
                rows = pl.ds(pl.multiple_of(kb * bk, bk), bk)
                dk_ref[rows, :] = (jnp.transpose(dkt_acc[kb]) * scale).astype(BF16)
                dv_ref[rows, :] = jnp.transpose(dvt_acc[kb]).astype(BF16)
                return 0

            lax.fori_loop(0, S // bk, flush, 0)

    head_rows = lambda off: pl.BlockSpec((S, HEAD_DIM), lambda hd, i: (0, off + hd))
    blk = pl.BlockSpec((bq, HEAD_DIM), lambda hd, i: (i, hd))
    return pl.pallas_call(
        body, name=name, grid=(H, nq), in_specs=[blk, head_rows(H), head_rows(2 * H), blk, blk],
        out_specs=[blk, head_rows(0), head_rows(0)], out_shape=[SDS((S, D), BF16)] * 3,
        scratch_shapes=[pltpu.VMEM((S // bk, HEAD_DIM, bk), F32), pltpu.VMEM((S // bk, HEAD_DIM, bk), F32)],
        compiler_params=_cp("arbitrary", "arbitrary"))(qkv, qkv, qkv, ltot, do)


def kernel(x, c, norm_mix_g, norm_ffn_g, w_mod, b_mod, pool_w, pool_scale, conv_w_in, conv_w, conv_w_out, sb_w_qkv, sb_w_o, ffn_w_gate, ffn_w_up, ffn_w_down, final_g, loss_target, m_norm_mix_g, m_norm_ffn_g, m_w_mod, m_b_mod, m_pool_w, m_pool_scale, m_conv_w_in, m_conv_w, m_conv_w_out, m_sb_w_qkv, m_sb_w_o, m_ffn_w_gate, m_ffn_w_up, m_ffn_w_down, m_final_g, v_norm_mix_g, v_norm_ffn_g, v_w_mod, v_b_mod, v_pool_w, v_pool_scale, v_conv_w_in, v_conv_w, v_conv_w_out, v_sb_w_qkv, v_sb_w_o, v_ffn_w_gate, v_ffn_w_up, v_ffn_w_down, v_final_g):
    S, D = x.shape[1], x.shape[2]
    L = norm_mix_g.shape[0]
    nmod = w_mod.shape[2]
    nf = ffn_w_gate.shape[2]
    n3 = conv_w_in.shape[2]
    nd = conv_w_out.shape[1]
    cb = n3 // 3
    ng = pool_w.shape[1]
    pg = pool_w.shape[3]
    n_pool = pool_w.shape[0]
    assert D % HEAD_DIM == 0 and S % 256 == 0 and nd == cb and N_CHIPS * nd == D and pg * ng == D

    mx, my, mc = lax.axis_index("x"), lax.axis_index("y"), lax.axis_index("c")
    chip = 2 * mx + my
    dev = 2 * chip + mc
    hx, ht = x[0], loss_target[0]

    c_all = _allgather8(jnp.broadcast_to(c, (8, D)), "gather_c").reshape(N_DEV, 8, D)[:, 0]
    c_rows = jnp.concatenate([c_all, jnp.zeros((8, D), F32)], axis=0)
    b_cols = lax.dynamic_slice_in_dim(b_mod, chip * nmod, nmod, axis=1).reshape(L, 1, nmod)
    mod_cols = _mod_fwd(c_rows, w_mod, b_cols, "mod_fwd")
    mod_all = _allgather8(mod_cols.reshape(L * 16, nmod), "gather_mod").reshape(N_CHIPS, 2, L, 16, nmod)
    mod = lax.dynamic_index_in_dim(mod_all[:, 0], dev, axis=2, keepdims=False)
    mod = jnp.transpose(mod, (1, 0, 2)).reshape(L, N_MOD, 1, D)

    bf = lambda w: w.astype(BF16)
    n_conv, n_sb = conv_w_in.shape[0], sb_w_qkv.shape[0]

    def mixer_shards(l):
        kind, j = l % 3, l // 3
        if kind == 0:
            return [bf(pool_w[j]).reshape(ng * (pg // N_CHIPS), pg)]
        return [bf(conv_w_in[j]), bf(conv_w_out[j])] if kind == 1 else [bf(sb_w_qkv[j]), bf(sb_w_o[j])]

    def mixer_weights(l, got):
        if l % 3 == 0:
            return [jnp.transpose(got[0].reshape(N_CHIPS, ng, pg // N_CHIPS, pg), (1, 0, 2, 3)).reshape(ng, pg, pg)]
        return [got[0], got[1].reshape(D, D)]

    first = mixer_shards(0)
    got = _run_comm(_GatherShards(first + [bf(ffn_w_gate[0]), bf(ffn_w_up[0]), bf(ffn_w_down[0])]), "gather_layer0")
    w_mix = {0: mixer_weights(0, got[:len(first)])}
    w_gate, w_up = {0: got[-3]}, {0: got[-2]}
    w_down = {0: got[-1].reshape(N_CHIPS * nf, D)}
    taps_cols = jnp.concatenate([pool_scale, conv_w.reshape(-1, nd)], axis=0)
    n_small = taps_cols.shape[0]
    small_rows = jnp.concatenate([taps_cols, jnp.zeros((16 - n_small, nd), F32)], axis=0)
    small_all = _allgather8(small_rows, "gather_small").reshape(N_CHIPS, 2, 16, nd)[:, 0]
    small_full = jnp.transpose(small_all, (1, 0, 2)).reshape(16, D)
    pool_scale_full = small_full[:n_pool]
    conv_taps_full = small_full[n_pool:n_small].reshape(n_conv, 3, D)

    saved = []
    h = hx
    for l in range(L):
        kind, j = l % 3, l // 3
        sh_m, sc_m, gt_m, sh_f, sc_f, gt_f = (mod[l, r] for r in range(N_MOD))
        gm = norm_mix_g[l].reshape(1, D)
        gf = norm_ffn_g[l].reshape(1, D)
        s = {"h_in": h}
        if kind == 0:
            s["diff"] = _pool_pre(h, gm, sc_m, sh_m, f"pool_pre{l}")
            s["cvec_m"] = gt_m * pool_scale_full[j].reshape(1, D)
            (h, s["y_m"]), _ = _mm_out_res(s["diff"], w_mix[l][0], h, s["cvec_m"], f"pool_mm{l}", groups=True)
        else:
            down_comm = None if l in w_down else _GatherShards([bf(ffn_w_down[l])])
            s["u"] = _norm_mod(h, gm, sc_m, sh_m, f"norm_mix{l}")
            s["cvec_m"] = gt_m
            if kind == 1:
                s["u3"], got = _mm_in(s["u"], w_mix[l][0], n3, f"conv_in{l}", comm=down_comm)
                s["a_m"] = _conv_mid(s["u3"], conv_taps_full[j], f"conv_mid{l}")
                (h, s["y_m"]), _ = _mm_out_res(s["a_m"], w_mix[l][1], h, gt_m, f"conv_out{l}")
            else:
                s["qkv"], got = _mm_in(s["u"], w_mix[l][0], n3, f"sb_qkv{l}", comm=down_comm)
                ahead = l + 1 < L and (l + 1) % 3 == 0
                (s["o"], s["ltot"]), got_ahead = _sb_attention(
                    s["qkv"], f"sb_attn{l}", comm=_GatherShards([bf(ffn_w_down[l + 1])]) if ahead else None)
                if ahead:
                    w_down[l + 1] = got_ahead[0].reshape(N_CHIPS * nf, D)
                (h, s["y_m"]), _ = _mm_out_res(s["o"], w_mix[l][1], h, gt_m, f"sb_out{l}")
            if down_comm:
                w_down[l] = got[0].reshape(N_CHIPS * nf, D)
        s["h_mid"] = h
        s["u2"] = _norm_mod(h, gf, sc_f, sh_f, f"norm_ffn{l}")
        more = l + 1 < L
        s["gate"], got = _mm_in(s["u2"], w_gate[l], nf, f"ffn_gate{l}",
                                comm=_GatherShards(mixer_shards(l + 1)) if more else None)
        if more:
            w_mix[l + 1] = mixer_weights(l + 1, got)
        (s["up"], s["a_f"]), got = _mm_in(s["u2"], w_up[l], nf, f"ffn_up{l}", gate=s["gate"],
                                          comm=_GatherShards([bf(ffn_w_gate[l + 1])]) if more else None)
        if more:
            w_gate[l + 1] = got[0]
        late = [bf(ffn_w_up[l + 1])] if more else []
        if more and (l + 1) % 3 == 0 and l + 1 not in w_down:
            late.append(bf(ffn_w_down[l + 1]))
        (h, s["y_f"]), got = _mm_out_res(s["a_f"], w_down[l], h, gt_f, f"ffn_down{l}",
                                         comm=_GatherShards(late) if more else None)
        if more:
            w_up[l + 1] = got[0]
            if len(late) > 1:
                w_down[l + 1] = got[1].reshape(N_CHIPS * nf, D)
        saved.append(s)

    gt_f_last = mod[L - 1, 5]
    dh, dy, st = _loss_head(h, final_g.reshape(1, D), ht, saved[-1]["y_f"], gt_f_last, "loss_head")
    loss = lax.psum(_sum_all(st[2:3], "loss_sum")[0, 0], ("x", "y", "c"))
    d_final_g = st[0:1]
    p_gate_f = st[1:2]
    d_norm_mix, d_norm_ffn = [None] * L, [None] * L
    d_mod = [[None] * N_MOD for _ in range(L)]
    d_pool_scale, d_taps = [None] * n_pool, [None] * n_conv
    big = {}
    ffn_reduce = mix_reduce = None
    for l in reversed(range(L)):
        kind, j = l % 3, l // 3
        s = saved[l]
        sh_m, sc_m, gt_m, sh_f, sc_f, gt_f = (mod[l, r] for r in range(N_MOD))
        gm = norm_mix_g[l].reshape(1, D)
        gf = norm_ffn_g[l].reshape(1, D)
        d_mod[l][5] = p_gate_f
        (dgate, dup), brought = _mm_nt(dy, w_down[l], f"ffn_down_bwd{l}", swiglu=(s["gate"], s["up"]),
                                       comm=ffn_reduce.scatter([0, 1]) if ffn_reduce else None)
        gw_down, brought_down = _mm_tn(s["a_f"], dy, f"ffn_down_wgrad{l}",
                                       comm=ffn_reduce.scatter([2]) if ffn_reduce else None)
        if ffn_reduce:
            big[("ffn", l + 1)] = ffn_reduce.finish(list(brought) + list(brought_down))
        gw_gate, brought = _mm_tn(s["u2"], dgate, f"ffn_gate_wgrad{l}", shard_cols=nf,
                                  comm=mix_reduce.scatter() if mix_reduce else None)
        if mix_reduce:
            big[("mix", l + 1)] = mix_reduce.finish(brought)
        gw_up, _ = _mm_tn(s["u2"], dup, f"ffn_up_wgrad{l}", shard_cols=nf)
        ffn_reduce = _Reduce([gw_gate, gw_up, gw_down.reshape(N_CHIPS, nf, D)], f"reduce_ffn{l}")
        du2, brought = _mm_nt_acc(dgate, w_gate[l], f"ffn_gate_bwd{l}", comm=ffn_reduce.swap)
        ffn_reduce.swapped(brought)
        mix_reduce = None
        if l > 0:
            du2, _ = _mm_nt_acc(dup, w_up[l], f"ffn_up_bwd{l}", add=du2)
            dh, dy, st = _norm_bwd(s["h_mid"], gf, sc_f, du2, dh, f"norm_ffn_bwd{l}", prev=(s["y_m"], s["cvec_m"]))
        else:
            du2, brought = _mm_nt_acc(dup, w_up[l], f"ffn_up_bwd{l}", add=du2, comm=ffn_reduce.scatter([0, 1]))
            (dh, dy, st), brought_down = _norm_bwd(s["h_mid"], gf, sc_f, du2, dh, f"norm_ffn_bwd{l}",
                                                   prev=(s["y_m"], s["cvec_m"]), comm=ffn_reduce.scatter([2]))
            big[("ffn", l)] = ffn_reduce.finish(list(brought) + list(brought_down))
        d_mod[l][3], d_mod[l][4], d_norm_ffn[l] = st[0:1], st[1:2], st[2:3]
        p_mix = st[3:4]
        if kind == 0:
            d_mod[l][2] = p_mix * pool_scale_full[j].reshape(1, D)
            d_pool_scale[j] = p_mix * gt_m
            dd, _ = _mm_nt(dy, w_mix[l][0], f"pool_mm_bwd{l}", groups=True)
            big[("pool", j)], _ = _mm_tn(s["diff"], dy, f"pool_wgrad{l}", groups=ng)
            du = _pool_post(dd, f"pool_post{l}")
        elif kind == 1:
            d_mod[l][2] = p_mix
            da, _ = _mm_nt(dy, w_mix[l][1], f"conv_out_bwd{l}")
            gw_out, _ = _mm_tn(s["a_m"], dy, f"conv_out_wgrad{l}")
            db, dc, dv, dtap = _conv_mid_bwd(s["u3"], da, conv_taps_full[j], f"conv_mid_bwd{l}")
            d_taps[j] = _sum_lead(dtap, f"conv_tap_sum{l}")[0:3]
            du3 = jnp.concatenate([db, dc, dv], axis=1)
            gw_in, _ = _mm_tn(s["u"], du3, f"conv_in_wgrad{l}", shard_cols=n3)
            mix_reduce = _Reduce([gw_in, gw_out.reshape(N_CHIPS, nd, D)], f"reduce_conv{l}")
            du, brought = _mm_nt_acc(du3, w_mix[l][0], f"conv_in_bwd{l}", comm=mix_reduce.swap)
            mix_reduce.swapped(brought)
        else:
            d_mod[l][2] = p_mix
            do, _ = _mm_nt(dy, w_mix[l][1], f"sb_out_bwd{l}")
            gw_o, _ = _mm_tn(s["o"], dy, f"sb_out_wgrad{l}")
            dq, dk, dv = _sb_attention_bwd(s["qkv"], s["ltot"], do, f"sb_attn_bwd{l}")
            dqkv = jnp.concatenate([dq, dk, dv], axis=1)
            gw_qkv, _ = _mm_tn(s["u"], dqkv, f"sb_qkv_wgrad{l}", shard_cols=n3)
            mix_reduce = _Reduce([gw_qkv, gw_o.reshape(N_CHIPS, nd, D)], f"reduce_sb{l}")
            du, brought = _mm_nt_acc(dqkv, w_mix[l][0], f"sb_qkv_bwd{l}", comm=mix_reduce.swap)
            mix_reduce.swapped(brought)
        if l > 0:
            prev = (saved[l - 1]["y_f"], mod[l - 1, 5])
            dh, dy, st = _norm_bwd(s["h_in"], gm, sc_m, du, dh, f"norm_mix_bwd{l}", prev=prev)
            p_gate_f = st[3:4]
        else:
            dh, st = _norm_bwd(s["h_in"], gm, sc_m, du, dh, f"norm_mix_bwd{l}")
        d_mod[l][0], d_mod[l][1], d_norm_mix[l] = st[0:1], st[1:2], st[2:3]
    grad_x = dh.reshape(1, S, D)
    if mix_reduce:
        big[("mix", 0)] = mix_reduce.finish()

    gw_pool = jnp.stack([big[("pool", j)] for j in range(n_pool)])
    gw_pool = jnp.transpose(gw_pool.reshape(n_pool, ng, N_CHIPS, pg // N_CHIPS, pg), (2, 0, 1, 3, 4))
    pool_reduce = _Reduce([gw_pool.reshape(N_CHIPS, n_pool * ng * (pg // N_CHIPS), pg)], "reduce_pool")
    (g_pool,) = pool_reduce.swapped().finish()

    rows = [d_final_g] + d_norm_mix + d_norm_ffn + [r for l in range(L) for r in d_mod[l]] + d_pool_scale
    rows += [d_taps[j] for j in range(n_conv)]
    vec = jnp.concatenate(rows, axis=0)
    n_rows = vec.shape[0]
    pad = -n_rows % 8
    vec = jnp.concatenate([vec, jnp.zeros((pad, D), F32)], axis=0) if pad else vec
    vec_all = _allgather8(vec, "gather_small_grads").reshape(N_DEV, n_rows + pad, D)
    tot = _sum_devices(vec_all, "sum_small_grads")
    r0 = 1 + 2 * L
    g_final = tot[0]
    g_norm_mix = tot[1:1 + L]
    g_norm_ffn = tot[1 + L:r0]
    g_b_mod = tot[r0:r0 + N_MOD * L].reshape(L, N_MOD * D)
    r1 = r0 + N_MOD * L
    g_pool_scale = lax.dynamic_slice_in_dim(tot[r1:r1 + n_pool], chip * nd, nd, axis=1)
    g_taps = lax.dynamic_slice_in_dim(tot[r1 + n_pool:r1 + n_pool + 3 * n_conv], chip * nd, nd, axis=1)
    g_conv_w = g_taps.reshape(conv_w.shape)
    dmod_all = vec_all[:, r0:r1].reshape(N_DEV, L, N_MOD * D)
    dmod_cols = jnp.transpose(lax.dynamic_slice_in_dim(dmod_all, chip * nmod, nmod, axis=2), (1, 0, 2))
    g_w_mod, _ = _mod_wgrad(jnp.transpose(c_all), dmod_cols, "mod_wgrad")

    g_ffn_gate = jnp.stack([big[("ffn", l)][0] for l in range(L)])
    g_ffn_up = jnp.stack([big[("ffn", l)][1] for l in range(L)])
    g_ffn_down = jnp.stack([big[("ffn", l)][2] for l in range(L)])
    g_conv_in = jnp.stack([big[("mix", 3 * j + 1)][0] for j in range(n_conv)])
    g_conv_out = jnp.stack([big[("mix", 3 * j + 1)][1] for j in range(n_conv)])
    g_sb_qkv = jnp.stack([big[("mix", 3 * j + 2)][0] for j in range(n_sb)])
    g_sb_o = jnp.stack([big[("mix", 3 * j + 2)][1] for j in range(n_sb)])
    g_pool_w = g_pool.reshape(pool_w.shape)

    grads = [g_norm_mix, g_norm_ffn, g_w_mod, g_b_mod, g_pool_w, g_pool_scale, g_conv_in, g_conv_w, g_conv_out,
             g_sb_qkv, g_sb_o, g_ffn_gate, g_ffn_up, g_ffn_down, g_final]
    weights = [norm_mix_g, norm_ffn_g, w_mod, b_mod, pool_w, pool_scale, conv_w_in, conv_w, conv_w_out,
               sb_w_qkv, sb_w_o, ffn_w_gate, ffn_w_up, ffn_w_down, final_g]
    ms = [m_norm_mix_g, m_norm_ffn_g, m_w_mod, m_b_mod, m_pool_w, m_pool_scale, m_conv_w_in, m_conv_w, m_conv_w_out,
          m_sb_w_qkv, m_sb_w_o, m_ffn_w_gate, m_ffn_w_up, m_ffn_w_down, m_final_g]
    vs = [v_norm_mix_g, v_norm_ffn_g, v_w_mod, v_b_mod, v_pool_w, v_pool_scale, v_conv_w_in, v_conv_w, v_conv_w_out,
          v_sb_w_qkv, v_sb_w_o, v_ffn_w_gate, v_ffn_w_up, v_ffn_w_down, v_final_g]
    deltas, new_ms, new_vs = [], [], []
    for n, (w, g, m, v) in enumerate(zip(weights, grads, ms, vs)):
        if w.ndim == 1:
            w, g, m, v = (a.reshape(1, -1) for a in (w, g, m, v))
        g = g.reshape(w.shape)
        grads[n] = g.reshape(weights[n].shape)
        (d, nm, nv), _ = _adamw(w, g, m, v, f"adamw{n}")
        deltas.append(d.reshape(weights[n].shape))
        new_ms.append(nm.reshape(weights[n].shape))
        new_vs.append(nv.reshape(weights[n].shape))
    return (loss, grad_x, *grads, *deltas, *new_ms, *new_vs)
```
